```python
import math
import jax, jax.numpy as jnp
from jax import lax
import numpy as np

D_MODEL = 1024
BATCH = 2
SEQ = 8192
DEPTH = 2

CTX_LEN = 256
GRID_W = 64

A_HEADS = 4
A_QK_DIM = 64
A_V_DIM = 2 * A_QK_DIM
A_WIDTH = A_HEADS * A_V_DIM
B_GROUPS = 4
B_GROUP_DIM = 64
B_WIDTH = B_GROUPS * B_GROUP_DIM
C_GROUPS = 4
C_GROUP_DIM = 64
C_WIDTH = C_GROUPS * C_GROUP_DIM
CHUNK = 128

N_BRANCH = 3
Q_BLOCK = 128
ROPE_BASE = 10000.0
AXIS_DIM = A_QK_DIM // 2
EPS = 1e-6

OFF_Q = 0
OFF_K = OFF_Q + A_WIDTH
OFF_V = OFF_K + A_WIDTH
OFF_F = OFF_V + A_WIDTH
OFF_U = OFF_F + B_WIDTH
OFF_VC = OFF_U + C_WIDTH
OFF_GATE = OFF_VC + C_WIDTH
GATE_WIDTH = A_WIDTH + B_WIDTH + C_WIDTH
OFF_MERGE = OFF_GATE + GATE_WIDTH
IN_WIDTH = OFF_MERGE + N_BRANCH * D_MODEL

kernel_name = "hybrid_diffattn_fnet_gmlp_prefix_dit"


def rms_norm(x, g):
    xf = x.astype(jnp.float32)
    y = xf * lax.rsqrt(jnp.mean(xf * xf, axis=-1, keepdims=True) + EPS)
    return (y * g.astype(jnp.float32)).astype(x.dtype)


def axial_rope_tables(n_tok):
    n_rows = n_tok // GRID_W
    row = jnp.repeat(jnp.arange(n_rows), GRID_W).astype(jnp.float32)
    col = jnp.tile(jnp.arange(GRID_W), n_rows).astype(jnp.float32)
    freqs = ROPE_BASE ** (-jnp.arange(0, AXIS_DIM, 2, dtype=jnp.float32) / AXIS_DIM)
    ang_r = row[:, None] * freqs
    ang_c = col[:, None] * freqs
    ang = jnp.concatenate([ang_r, ang_r, ang_c, ang_c], axis=-1)
    return jnp.cos(ang), jnp.sin(ang)


def rotate_half_axial(x):
    xs = x.reshape(x.shape[:-1] + (2, 2, AXIS_DIM // 2))
    x1, x2 = xs[..., 0, :], xs[..., 1, :]
    return jnp.stack([-x2, x1], axis=-2).reshape(x.shape)


def apply_rope(t, cos, sin):
    cos = cos[None, :, None, None, :].astype(t.dtype)
    sin = sin[None, :, None, None, :].astype(t.dtype)
    return t * cos + rotate_half_axial(t) * sin


def diff_attend(q, k, v, lam):
    s = jnp.einsum('bqhcd,bkhcd->bhcqk', q, k).astype(jnp.float32) * (A_QK_DIM ** -0.5)
    a = jax.nn.softmax(s, axis=-1)
    w = a[:, :, 0] - lam * a[:, :, 1]
    return jnp.einsum('bhqk,bkhe->bqhe', w.astype(v.dtype), v)


def blocked_latent_attention(q, k, v, lam):
    b, n = q.shape[:2]
    qb = q.reshape(b, n // Q_BLOCK, Q_BLOCK, A_HEADS, 2, A_QK_DIM).swapaxes(0, 1)
    ob = lax.map(lambda blk: diff_attend(blk, k, v, lam), qb)
    return ob.swapaxes(0, 1).reshape(b, n, A_HEADS, A_V_DIM)


def fourier_mix(f, w_f, b_f):
    b, n, _ = f.shape
    fg = f.reshape(b, n, B_GROUPS, B_GROUP_DIM).astype(jnp.float32)
    fr = jnp.fft.fft2(fg, axes=(1, 3), norm="ortho").real.astype(f.dtype)
    y = jnp.einsum('bngc,gcd->bngd', fr, w_f) + b_f
    return y.reshape(b, n, B_WIDTH)


def chunk_sgu(u, vc, ln_g, ln_b, w_s, b_s):
    b, n, _ = vc.shape
    vf = vc.astype(jnp.float32)
    mu = jnp.mean(vf, axis=-1, keepdims=True)
    var = jnp.mean(jnp.square(vf - mu), axis=-1, keepdims=True)
    vn = ((vf - mu) * lax.rsqrt(var + EPS) * ln_g.astype(jnp.float32)
          + ln_b.astype(jnp.float32)).astype(vc.dtype)
    vn = vn.reshape(b, n // CHUNK, CHUNK, C_GROUPS, C_GROUP_DIM)
    s = jnp.einsum('gpq,bnqgc->bnpgc', w_s, vn) + b_s.T[:, :, None]
    return u * s.reshape(b, n, C_WIDTH)


def merge_branches(p, o_a, w_f, b_f, ln_g, ln_b, w_s, b_s, w_br_a, w_br_b, w_br_c, w_out):
    o_b = fourier_mix(p[..., OFF_F:OFF_U], w_f, b_f)
    o_c = chunk_sgu(p[..., OFF_U:OFF_VC], p[..., OFF_VC:OFF_GATE], ln_g, ln_b, w_s, b_s)
    gate = jax.nn.silu(p[..., OFF_GATE:OFF_MERGE])
    ya = (o_a * gate[..., :A_WIDTH]) @ w_br_a
    yb = (o_b * gate[..., A_WIDTH:A_WIDTH + B_WIDTH]) @ w_br_b
    yc = (o_c * gate[..., A_WIDTH + B_WIDTH:]) @ w_br_c
    m = jax.nn.sigmoid(p[..., OFF_MERGE:])
    y = (m[..., :D_MODEL] * ya + m[..., D_MODEL:2 * D_MODEL] * yb
         + m[..., 2 * D_MODEL:] * yc)
    return y @ w_out


def setup_inputs(seed: int = 0) -> dict:
    key = jax.random.key(seed)
    ks = jax.random.split(key, 28)
    L, D = DEPTH, D_MODEL

    def nrm(k, shape, scale):
        return jax.random.normal(k, shape, jnp.float32) * scale

    return {
        "x": nrm(ks[0], (BATCH, SEQ, D), 1.0),
        "c": nrm(ks[1], (BATCH, D), 1.0),
        "ctx": nrm(ks[2], (BATCH, CTX_LEN, D), 1.0),
        "c_ctx": nrm(ks[3], (D,), 1.0),
        "w_ada": nrm(ks[4], (L, D, 3 * D), 0.5 * D ** -0.5),
        "b_ada": nrm(ks[5], (L, 3 * D), 0.02),
        "g_norm": 1.0 + nrm(ks[6], (L, D), 0.02),
        "w_in": nrm(ks[7], (L, D, IN_WIDTH), D ** -0.5),
        "g_q": 1.0 + nrm(ks[8], (L, A_QK_DIM), 0.02),
        "g_k": 1.0 + nrm(ks[9], (L, A_QK_DIM), 0.02),
        "lam_q1": nrm(ks[10], (L, A_QK_DIM), 0.1),
        "lam_k1": nrm(ks[11], (L, A_QK_DIM), 0.1),
        "lam_q2": nrm(ks[12], (L, A_QK_DIM), 0.1),
        "lam_k2": nrm(ks[13], (L, A_QK_DIM), 0.1),
        "g_sub": 1.0 + nrm(ks[14], (L, A_V_DIM), 0.02),
        "w_f": nrm(ks[15], (L, B_GROUPS, B_GROUP_DIM, B_GROUP_DIM), B_GROUP_DIM ** -0.5),
        "b_f": nrm(ks[16], (L, B_GROUPS, B_GROUP_DIM), 0.02),
        "ln_g": 1.0 + nrm(ks[17], (L, C_WIDTH), 0.02),
        "ln_b": nrm(ks[18], (L, C_WIDTH), 0.02),
        "w_s": nrm(ks[19], (L, C_GROUPS, CHUNK, CHUNK), CHUNK ** -0.5),
        "b_s": 1.0 + nrm(ks[20], (L, C_GROUPS, CHUNK), 0.02),
        "w_br_a": nrm(ks[21], (L, A_WIDTH, D), A_WIDTH ** -0.5),
        "w_br_b": nrm(ks[22], (L, B_WIDTH, D), B_WIDTH ** -0.5),
        "w_br_c": nrm(ks[23], (L, C_WIDTH, D), C_WIDTH ** -0.5),
        "w_out": nrm(ks[24], (L, D, D), D ** -0.5),
    }


def reference(x, c, ctx, c_ctx, w_ada, b_ada, g_norm, w_in, g_q, g_k,
              lam_q1, lam_k1, lam_q2, lam_k2, g_sub, w_f, b_f, ln_g, ln_b,
              w_s, b_s, w_br_a, w_br_b, w_br_c, w_out):
    b, n, _ = x.shape
    n_ctx = ctx.shape[1]
    cos, sin = axial_rope_tables(n)
    for l in range(DEPTH):
        last = l == DEPTH - 1
        lam_init = 0.8 - 0.6 * math.exp(-0.3 * l)
        lam = (jnp.exp(jnp.sum(lam_q1[l].astype(jnp.float32) * lam_k1[l].astype(jnp.float32)))
               - jnp.exp(jnp.sum(lam_q2[l].astype(jnp.float32) * lam_k2[l].astype(jnp.float32)))
               + lam_init)

        shift, scale, gate = jnp.split(jax.nn.silu(c) @ w_ada[l] + b_ada[l], 3, axis=-1)
        shift_c, scale_c, gate_c = jnp.split(jax.nn.silu(c_ctx) @ w_ada[l] + b_ada[l], 3, axis=-1)
        h = rms_norm(x, g_norm[l]) * (1.0 + scale[:, None, :]) + shift[:, None, :]
        h_ctx = rms_norm(ctx, g_norm[l]) * (1.0 + scale_c) + shift_c

        if last:
            kv_ctx = h_ctx @ w_in[l][:, OFF_K:OFF_F]
        else:
            p_ctx = h_ctx @ w_in[l]
            kv_ctx = p_ctx[..., OFF_K:OFF_F]
        k_ctx = rms_norm(kv_ctx[..., :A_WIDTH].reshape(b, n_ctx, A_HEADS, 2, A_QK_DIM), g_k[l])
        v_ctx = kv_ctx[..., A_WIDTH:].reshape(b, n_ctx, A_HEADS, A_V_DIM)

        p = h @ w_in[l]
        q = apply_rope(rms_norm(p[..., OFF_Q:OFF_K].reshape(b, n, A_HEADS, 2, A_QK_DIM), g_q[l]), cos, sin)
        k = apply_rope(rms_norm(p[..., OFF_K:OFF_V].reshape(b, n, A_HEADS, 2, A_QK_DIM), g_k[l]), cos, sin)
        v = p[..., OFF_V:OFF_F].reshape(b, n, A_HEADS, A_V_DIM)
        k_all = jnp.concatenate([k, k_ctx], axis=1)
        v_all = jnp.concatenate([v, v_ctx], axis=1)
        o_a = blocked_latent_attention(q, k_all, v_all, lam)
        o_a = (rms_norm(o_a, g_sub[l]) * (1.0 - lam_init)).reshape(b, n, A_WIDTH)
        out = merge_branches(p, o_a, w_f[l], b_f[l], ln_g[l], ln_b[l], w_s[l], b_s[l],
                             w_br_a[l], w_br_b[l], w_br_c[l], w_out[l])

        if not last:
            q_c = rms_norm(p_ctx[..., OFF_Q:OFF_K].reshape(b, n_ctx, A_HEADS, 2, A_QK_DIM), g_q[l])
            o_ac = diff_attend(q_c, k_ctx, v_ctx, lam)
            o_ac = (rms_norm(o_ac, g_sub[l]) * (1.0 - lam_init)).reshape(b, n_ctx, A_WIDTH)
            out_c = merge_branches(p_ctx, o_ac, w_f[l], b_f[l], ln_g[l], ln_b[l], w_s[l], b_s[l],
                                   w_br_a[l], w_br_b[l], w_br_c[l], w_out[l])
            ctx = ctx + gate_c * out_c

        x = x + gate[:, None, :] * out
    return x
```

```python
import functools
import math

import numpy as np
import jax
import jax.numpy as jnp
from jax import lax
from jax.experimental import pallas as pl
from jax.experimental.pallas import tpu as pltpu

A_HEADS = 4
A_QK_DIM = 64
A_V_DIM = 2 * A_QK_DIM
A_WIDTH = A_HEADS * A_V_DIM
B_GROUPS = 4
B_GROUP_DIM = 64
B_WIDTH = B_GROUPS * B_GROUP_DIM
C_GROUPS = 4
C_GROUP_DIM = 64
C_WIDTH = C_GROUPS * C_GROUP_DIM
CHUNK = 128
GRID_W = 64
ROPE_BASE = 10000.0
AXIS_DIM = A_QK_DIM // 2
EPS = 1e-6

OFF_Q = 0
OFF_K = OFF_Q + A_WIDTH
OFF_V = OFF_K + A_WIDTH
OFF_F = OFF_V + A_WIDTH
OFF_U = OFF_F + B_WIDTH
OFF_VC = OFF_U + C_WIDTH
OFF_GATE = OFF_VC + C_WIDTH
GATE_WIDTH = A_WIDTH + B_WIDTH + C_WIDTH
OFF_MERGE = OFF_GATE + GATE_WIDTH

V7X_LANES = 128
V7X_SUBLANES = 8
V7X_VMEM_LIMIT_BYTES = 56 * 1024 * 1024

LOG2E = math.log2(math.e)
Q_SCALE = (A_QK_DIM ** -0.5) * LOG2E

_F32 = jnp.float32
_BF16 = jnp.bfloat16
_NT = (((1,), (1,)), ((), ()))


def _sigmoid(x):
    return 1.0 / (1.0 + jnp.exp(-x))


def _params(n_axes):
    return pltpu.CompilerParams(dimension_semantics=("arbitrary",) * n_axes,
                                vmem_limit_bytes=V7X_VMEM_LIMIT_BYTES)


def _mod_kernel(c_ref, w_ref, b_ref, o_ref):
    cc = c_ref[...]
    s = cc * _sigmoid(cc)
    o_ref[0] = jnp.dot(s, w_ref[0], preferred_element_type=_F32) + b_ref[0]


def _mod_call(rows, w_ada, b_ada):
    depth, d, d3 = w_ada.shape
    nb = d3 // d
    return pl.pallas_call(
        _mod_kernel,
        grid=(depth, nb),
        in_specs=[pl.BlockSpec((rows.shape[0], d), lambda l, j: (0, 0)),
                  pl.BlockSpec((1, d, d), lambda l, j: (l, 0, j)),
                  pl.BlockSpec((1, 1, d), lambda l, j: (l, 0, j))],
        out_specs=pl.BlockSpec((1, rows.shape[0], d), lambda l, j: (l, 0, j)),
        out_shape=jax.ShapeDtypeStruct((depth, rows.shape[0], d3), _F32),
        compiler_params=_params(2),
        name="adaln_mod",
    )(rows, w_ada, b_ada.reshape(depth, 1, d3))


def _inproj_kernel(x_ref, mod_ref, gn_ref, w_ref, wvt_ref, gq_ref, gk_ref, cos_ref, sa_ref, sb_ref,
                   bmat_ref, lng_ref, lnb_ref, ws_ref, bsm_ref,
                   q2_ref, k_ref, vt_ref, f_ref, oc_ref, ga_ref, gb_ref, m_ref, *, tm, d):
    x = x_ref[0]
    mod = mod_ref[0]
    shift = mod[:, :d]
    scale = mod[:, d:2 * d]
    ms = jnp.mean(x * x, axis=-1, keepdims=True)
    h = (x * lax.rsqrt(ms + EPS) * gn_ref[...]) * (1.0 + scale) + shift
    hb = h.astype(_BF16)

    def proj(off, width):
        return jnp.dot(hb, w_ref[:, off:off + width], preferred_element_type=_F32)

    cos = cos_ref[...]
    sa = sa_ref[...]
    sb = sb_ref[...]
    bmat = bmat_ref[...]
    lane = lax.broadcasted_iota(jnp.int32, (tm, V7X_LANES), 1)
    lo = lane < A_QK_DIM

    def qk_head(p_h, g):
        msq = jnp.dot((p_h * p_h).astype(_BF16), bmat, preferred_element_type=_F32)
        t = p_h * lax.rsqrt(msq + EPS) * g
        return t * cos + pltpu.roll(t, V7X_LANES - AXIS_DIM // 2, 1) * sa + pltpu.roll(t, AXIS_DIM // 2, 1) * sb

    pq = proj(OFF_Q, A_WIDTH)
    gq = gq_ref[...]
    for hh in range(A_HEADS):
        t = qk_head(pq[:, hh * A_V_DIM:(hh + 1) * A_V_DIM], gq) * Q_SCALE
        q2_ref[0, hh, 0] = jnp.where(lo, t, 0.0).astype(_BF16)
        q2_ref[0, hh, 1] = jnp.where(lo, 0.0, t).astype(_BF16)

    pk = proj(OFF_K, A_WIDTH)
    gk = gk_ref[...]
    for hh in range(A_HEADS):
        k_ref[0, hh] = qk_head(pk[:, hh * A_V_DIM:(hh + 1) * A_V_DIM], gk).astype(_BF16)

    for hh in range(A_HEADS):
        vt = lax.dot_general(wvt_ref[hh * A_V_DIM:(hh + 1) * A_V_DIM, :], hb, _NT,
                             preferred_element_type=_F32)
        vt_ref[0, hh, 0] = vt.astype(_BF16)

    f_ref[0] = proj(OFF_F, B_WIDTH)

    u = proj(OFF_U, C_WIDTH)
    vc = proj(OFF_VC, C_WIDTH)
    mu = jnp.mean(vc, axis=-1, keepdims=True)
    dv = vc - mu
    var = jnp.mean(dv * dv, axis=-1, keepdims=True)
    vn = (dv * lax.rsqrt(var + EPS) * lng_ref[...] + lnb_ref[...]).astype(_BF16)

    pg = proj(OFF_GATE, GATE_WIDTH)
    gate = pg * _sigmoid(pg)
    ga_ref[0] = gate[:, :A_WIDTH].astype(_BF16)
    gb_ref[0] = gate[:, A_WIDTH:A_WIDTH + B_WIDTH].astype(_BF16)
    gate_c = gate[:, A_WIDTH + B_WIDTH:]

    lo_c = lax.broadcasted_iota(jnp.int32, (CHUNK, V7X_LANES), 1) < C_GROUP_DIM
    bsm = bsm_ref[...]
    for t in range(tm // CHUNK):
        rows = slice(t * CHUNK, (t + 1) * CHUNK)
        vl = vn[rows, :V7X_LANES]
        vr = vn[rows, V7X_LANES:]
        s_l = jnp.where(lo_c, jnp.dot(ws_ref[0], vl, preferred_element_type=_F32),
                        jnp.dot(ws_ref[1], vl, preferred_element_type=_F32))
        s_r = jnp.where(lo_c, jnp.dot(ws_ref[2], vr, preferred_element_type=_F32),
                        jnp.dot(ws_ref[3], vr, preferred_element_type=_F32))
        s = jnp.concatenate([s_l, s_r], axis=1) + bsm
        oc_ref[0, rows, :] = (u[rows] * s * gate_c[rows]).astype(_BF16)

    for j in range(3):
        pm = proj(OFF_MERGE + j * d, d)
        m_ref[0, :, j * d:(j + 1) * d] = _sigmoid(pm).astype(_BF16)


def _inproj_call(x, mod, gn, w_bf, wvt_bf, gq, gk, cos, sa, sb, bmat, lng, lnb, ws_bf, bsm, *, tm):
    b, n, d = x.shape
    nt = n // tm
    in_w = w_bf.shape[1]
    full = lambda *shape: pl.BlockSpec(shape, lambda bi, i: (0,) * len(shape))
    in_specs = [
        pl.BlockSpec((1, tm, d), lambda bi, i: (bi, i, 0)),
        pl.BlockSpec((1, 1, 3 * d), lambda bi, i: (bi, 0, 0)),
        full(1, d),
        full(d, in_w),
        full(A_WIDTH, d),
        full(1, V7X_LANES),
        full(1, V7X_LANES),
        pl.BlockSpec((tm, V7X_LANES), lambda bi, i: (i, 0)),
        pl.BlockSpec((tm, V7X_LANES), lambda bi, i: (i, 0)),
        pl.BlockSpec((tm, V7X_LANES), lambda bi, i: (i, 0)),
        full(V7X_LANES, V7X_LANES),
        full(1, C_WIDTH),
        full(1, C_WIDTH),
        full(C_GROUPS, CHUNK, CHUNK),
        full(CHUNK, C_WIDTH),
    ]
    out_shape = [
        jax.ShapeDtypeStruct((b, A_HEADS, 2, n, A_V_DIM), _BF16),
        jax.ShapeDtypeStruct((b, A_HEADS, n, A_V_DIM), _BF16),
        jax.ShapeDtypeStruct((b, A_HEADS, nt, A_V_DIM, tm), _BF16),
        jax.ShapeDtypeStruct((b, n, B_WIDTH), _F32),
        jax.ShapeDtypeStruct((b, n, C_WIDTH), _BF16),
        jax.ShapeDtypeStruct((b, n, A_WIDTH), _BF16),
        jax.ShapeDtypeStruct((b, n, B_WIDTH), _BF16),
        jax.ShapeDtypeStruct((b, n, 3 * d), _BF16),
    ]
    out_specs = [
        pl.BlockSpec((1, A_HEADS, 2, tm, A_V_DIM), lambda bi, i: (bi, 0, 0, i, 0)),
        pl.BlockSpec((1, A_HEADS, tm, A_V_DIM), lambda bi, i: (bi, 0, i, 0)),
        pl.BlockSpec((1, A_HEADS, 1, A_V_DIM, tm), lambda bi, i: (bi, 0, i, 0, 0)),
        pl.BlockSpec((1, tm, B_WIDTH), lambda bi, i: (bi, i, 0)),
        pl.BlockSpec((1, tm, C_WIDTH), lambda bi, i: (bi, i, 0)),
        pl.BlockSpec((1, tm, A_WIDTH), lambda bi, i: (bi, i, 0)),
        pl.BlockSpec((1, tm, B_WIDTH), lambda bi, i: (bi, i, 0)),
        pl.BlockSpec((1, tm, 3 * d), lambda bi, i: (bi, i, 0)),
    ]
    return pl.pallas_call(
        functools.partial(_inproj_kernel, tm=tm, d=d),
        grid=(b, nt),
        in_specs=in_specs,
        out_specs=out_specs,
        out_shape=out_shape,
        compiler_params=_params(2),
        name="inproj",
    )(x, mod, gn, w_bf, wvt_bf, gq, gk, cos, sa, sb, bmat, lng, lnb, ws_bf, bsm)


def _attn_kernel(*refs, tq, n_src, tiles, lam_init):
    lam_ref, q2_ref, ga_ref, gsub_ref = refs[:4]
    src_refs = refs[4:4 + 2 * n_src]
    out_ref = refs[4 + 2 * n_src]
    m_sc, l_sc, acc_sc = refs[5 + 2 * n_src:]

    q2 = q2_ref[0, 0].reshape(2 * tq, A_V_DIM)
    m_sc[...] = jnp.full(m_sc.shape, -jnp.inf, _F32)
    l_sc[...] = jnp.zeros(l_sc.shape, _F32)
    acc_sc[...] = jnp.zeros(acc_sc.shape, _F32)

    def step(k_t, vt_t):
        s = lax.dot_general(k_t, q2, _NT, preferred_element_type=_F32)
        m_prev = m_sc[...]
        m_new = jnp.maximum(m_prev, jnp.max(s, axis=0, keepdims=True))
        alpha = jnp.exp2(m_prev - m_new)
        p = jnp.exp2(s - m_new)
        l_sc[...] = alpha * l_sc[...] + jnp.sum(p, axis=0, keepdims=True)
        acc_sc[...] = alpha * acc_sc[...] + jnp.dot(vt_t, p.astype(_BF16), preferred_element_type=_F32)
        m_sc[...] = m_new

    for si in range(n_src):
        k_ref = src_refs[2 * si]
        vt_ref = src_refs[2 * si + 1]
        tk = vt_ref.shape[-1]
        if tiles[si] == 1:
            step(k_ref[0, 0], vt_ref[0, 0, 0])
        else:
            def body(t, carry, k_ref=k_ref, vt_ref=vt_ref, tk=tk):
                step(k_ref[0, 0, pl.ds(pl.multiple_of(t * tk, tk), tk), :], vt_ref[0, 0, t])
                return carry
            lax.fori_loop(0, tiles[si], body, 0)

    lv = lam_ref[...]
    lam = (jnp.exp(jnp.sum(lv[0:1] * lv[1:2], axis=-1, keepdims=True))
           - jnp.exp(jnp.sum(lv[2:3] * lv[3:4], axis=-1, keepdims=True)) + lam_init)
    inv = 1.0 / l_sc[...]
    acc = acc_sc[...]
    o = acc[:, :tq] * inv[:, :tq] - lam * (acc[:, tq:] * inv[:, tq:])
    ms = jnp.mean(o * o, axis=0, keepdims=True)
    on = o * lax.rsqrt(ms + EPS)
    out = on.T * (gsub_ref[...] * (1.0 - lam_init)) * ga_ref[0].astype(_F32)
    out_ref[0] = out.astype(_BF16)


def _attn_call(lamv, q2, ga, gsub, sources, *, tq, lam_init):
    b, _, _, n, _ = q2.shape
    in_specs = [
        pl.BlockSpec(lamv.shape, lambda bi, hi, i: (0, 0)),
        pl.BlockSpec((1, 1, 2, tq, A_V_DIM), lambda bi, hi, i: (bi, hi, 0, i, 0)),
        pl.BlockSpec((1, tq, A_V_DIM), lambda bi, hi, i: (bi, i, hi)),
        pl.BlockSpec((1, A_V_DIM), lambda bi, hi, i: (0, 0)),
    ]
    args = [lamv, q2, ga, gsub]
    tiles = []
    for k_s, vt_s in sources:
        ls = k_s.shape[2]
        nt, _, tk = vt_s.shape[2:]
        tiles.append(nt)
        in_specs.append(pl.BlockSpec((1, 1, ls, A_V_DIM), lambda bi, hi, i: (bi, hi, 0, 0)))
        in_specs.append(pl.BlockSpec((1, 1, nt, A_V_DIM, tk), lambda bi, hi, i: (bi, hi, 0, 0, 0)))
        args += [k_s, vt_s]
    return pl.pallas_call(
        functools.partial(_attn_kernel, tq=tq, n_src=len(sources), tiles=tuple(tiles), lam_init=lam_init),
        grid=(b, A_HEADS, n // tq),
        in_specs=in_specs,
        out_specs=pl.BlockSpec((1, tq, A_V_DIM), lambda bi, hi, i: (bi, i, hi)),
        out_shape=jax.ShapeDtypeStruct((b, n, A_WIDTH), _BF16),
        scratch_shapes=[pltpu.VMEM((1, 2 * tq), _F32), pltpu.VMEM((1, 2 * tq), _F32),
                        pltpu.VMEM((A_V_DIM, 2 * tq), _F32)],
        compiler_params=_params(3),
        name="diff_attn",
    )(*args)


def _dft1_kernel(f1_ref, x_ref, o_ref):
    o_ref[0] = jnp.dot(f1_ref[...], x_ref[0].astype(_BF16), preferred_element_type=_F32)


def _dft1_call(f1, x2d, *, tc):
    b, n1, w = x2d.shape
    return pl.pallas_call(
        _dft1_kernel,
        grid=(b, w // tc),
        in_specs=[pl.BlockSpec((2 * n1, n1), lambda bi, j: (0, 0)),
                  pl.BlockSpec((1, n1, tc), lambda bi, j: (bi, 0, j))],
        out_specs=pl.BlockSpec((1, 2 * n1, tc), lambda bi, j: (bi, 0, j)),
        out_shape=jax.ShapeDtypeStruct((b, 2 * n1, w), _F32),
        compiler_params=_params(2),
        name="dft_stage1",
    )(f1, x2d)


def _dft2_kernel(*refs, g, n2, two_stage, norm):
    if two_stage:
        ar_ref, ai_ref, twc_ref, tws_ref, w2_ref, cbd_ref, sbd_ref, wf_ref, bf_ref, gb_ref, out_ref = refs
    else:
        ar_ref, w2_ref, cbd_ref, sbd_ref, wf_ref, bf_ref, gb_ref, out_ref = refs
    w2 = w2_ref[...]
    cbd = cbd_ref[...]
    sbd = sbd_ref[...]
    wf = wf_ref[...]
    bias = bf_ref[...]
    for j in range(g):
        ar = ar_ref[0, 0, j]
        if two_stage:
            ai = ai_ref[0, 0, j]
            tc = twc_ref[j]
            ts = tws_ref[j]
            c2 = jnp.concatenate([tc, tc], axis=1)
            s2 = jnp.concatenate([ts, ts], axis=1)
            a = jnp.concatenate([ar * c2 - ai * s2, ar * s2 + ai * c2], axis=0).astype(_BF16)
        else:
            a = ar.astype(_BF16)
        z = jnp.dot(w2, a, preferred_element_type=_F32)
        p = z[:n2].astype(_BF16)
        q = z[n2:].astype(_BF16)
        fr = (jnp.dot(p, cbd, preferred_element_type=_F32) - jnp.dot(q, sbd, preferred_element_type=_F32)) * norm
        y = jnp.dot(fr.astype(_BF16), wf, preferred_element_type=_F32) + bias
        cols = slice(j * B_WIDTH, (j + 1) * B_WIDTH)
        out_ref[0, :, cols] = (y * gb_ref[0, :, cols].astype(_F32)).astype(_BF16)


def _dft_tables(n1, n2):
    n = n1 * n2
    k1 = np.arange(n1)
    a1 = 2.0 * np.pi * ((k1[:, None] * k1[None, :]) % n1) / n1
    f1 = np.concatenate([np.cos(a1), np.sin(a1)], axis=0)
    k2 = np.arange(n2)
    a2 = 2.0 * np.pi * ((k2[:, None] * k2[None, :]) % n2) / n2
    c2, s2 = np.cos(a2), np.sin(a2)
    w2 = np.block([[c2, -s2], [s2, c2]]) if n1 > 1 else np.concatenate([c2, s2], axis=0)
    gch = np.arange(B_GROUP_DIM)
    ag = 2.0 * np.pi * ((gch[:, None] * gch[None, :]) % B_GROUP_DIM) / B_GROUP_DIM
    eye = np.eye(B_GROUPS)
    cbd = np.kron(eye, np.cos(ag))
    sbd = np.kron(eye, np.sin(ag))
    to_bf16 = lambda t: jnp.asarray(t, _F32).astype(_BF16)
    return to_bf16(f1), to_bf16(w2), to_bf16(cbd), to_bf16(sbd), 1.0 / math.sqrt(n * B_GROUP_DIM)


def _twiddles(n1, n2):
    k1 = jnp.arange(n1, dtype=jnp.int32)[:, None]
    m2 = jnp.arange(n2, dtype=jnp.int32)[None, :]
    ang = (k1 * m2).astype(_F32) * (2.0 * math.pi / (n1 * n2))
    shape = (n1, n2, V7X_LANES)
    return (jnp.broadcast_to(jnp.cos(ang)[:, :, None], shape), jnp.broadcast_to(jnp.sin(ang)[:, :, None], shape))


def _fourier(f, gb, wf_bd, bf_row):
    b, n, _ = f.shape
    two_stage = n > 2 * V7X_LANES
    n2 = V7X_LANES if two_stage else n
    n1 = n // n2
    g = min(n1, V7X_SUBLANES)
    f1, w2, cbd, sbd, norm = _dft_tables(n1, n2)
    full = lambda *shape: pl.BlockSpec(shape, lambda bi, j: (0,) * len(shape))
    gate_spec = pl.BlockSpec((1, n2, g * B_WIDTH), lambda bi, j: (bi, 0, j))
    consts = [w2, cbd, sbd, wf_bd, bf_row]
    const_specs = [full(*w2.shape), full(B_WIDTH, B_WIDTH), full(B_WIDTH, B_WIDTH), full(B_WIDTH, B_WIDTH),
                   full(1, B_WIDTH)]
    if two_stage:
        tc = min(n2 * B_WIDTH, 4096)
        a = _dft1_call(f1, f.reshape(b, n1, n2 * B_WIDTH), tc=tc)
        a5 = a.reshape(b, 2, n1, n2, B_WIDTH)
        twc, tws = _twiddles(n1, n2)
        args = [a5, a5, twc, tws] + consts
        in_specs = [pl.BlockSpec((1, 1, g, n2, B_WIDTH), lambda bi, j: (bi, 0, j, 0, 0)),
                    pl.BlockSpec((1, 1, g, n2, B_WIDTH), lambda bi, j: (bi, 1, j, 0, 0)),
                    pl.BlockSpec((g, n2, V7X_LANES), lambda bi, j: (j, 0, 0)),
                    pl.BlockSpec((g, n2, V7X_LANES), lambda bi, j: (j, 0, 0))] + const_specs
    else:
        args = [f.reshape(b, 1, 1, n2, B_WIDTH)] + consts
        in_specs = [pl.BlockSpec((1, 1, 1, n2, B_WIDTH), lambda bi, j: (bi, 0, 0, 0, 0))] + const_specs
    args.append(gb.reshape(b, n2, n1 * B_WIDTH))
    in_specs.append(gate_spec)
    out = pl.pallas_call(
        functools.partial(_dft2_kernel, g=g, n2=n2, two_stage=two_stage, norm=norm),
        grid=(b, n1 // g),
        in_specs=in_specs,
        out_specs=gate_spec,
        out_shape=jax.ShapeDtypeStruct((b, n2, n1 * B_WIDTH), _BF16),
        compiler_params=_params(2),
        name="dft_stage2",
    )(*args)
    return out.reshape(b, n, B_WIDTH)


def _merge_kernel(oa_ref, ob_ref, oc_ref, m_ref, x_ref, mod_ref, wa_ref, wb_ref, wc_ref, wo_ref, o_ref, *, d):
    ya = jnp.dot(oa_ref[0], wa_ref[...], preferred_element_type=_F32)
    yb = jnp.dot(ob_ref[0], wb_ref[...], preferred_element_type=_F32)
    yc = jnp.dot(oc_ref[0], wc_ref[...], preferred_element_type=_F32)
    m = m_ref[0]
    y = (m[:, :d].astype(_F32) * ya + m[:, d:2 * d].astype(_F32) * yb + m[:, 2 * d:].astype(_F32) * yc)
    out = jnp.dot(y.astype(_BF16), wo_ref[...], preferred_element_type=_F32)
    gate = mod_ref[0][:, 2 * d:]
    o_ref[0] = x_ref[0] + gate * out


def _merge_call(oa, ob, oc, m, x, mod, wa, wb, wc, wo, *, tm):
    b, n, d = x.shape
    row = lambda w: pl.BlockSpec((1, tm, w), lambda bi, i: (bi, i, 0))
    full = lambda *shape: pl.BlockSpec(shape, lambda bi, i: (0,) * len(shape))
    return pl.pallas_call(
        functools.partial(_merge_kernel, d=d),
        grid=(b, n // tm),
        in_specs=[row(A_WIDTH), row(B_WIDTH), row(C_WIDTH), row(3 * d), row(d),
                  pl.BlockSpec((1, 1, 3 * d), lambda bi, i: (bi, 0, 0)),
                  full(A_WIDTH, d), full(B_WIDTH, d), full(C_WIDTH, d), full(d, d)],
        out_specs=row(d),
        out_shape=jax.ShapeDtypeStruct((b, n, d), _F32),
        compiler_params=_params(2),
        name="merge",
    )(oa, ob, oc, m, x, mod, wa, wb, wc, wo)


def _rope_tables(n):
    pos = jnp.arange(n, dtype=jnp.int32)
    row = (pos // GRID_W).astype(_F32)
    col = (pos % GRID_W).astype(_F32)
    freqs = ROPE_BASE ** (-jnp.arange(0, AXIS_DIM, 2, dtype=_F32) / AXIS_DIM)
    ang_r = row[:, None] * freqs
    ang_c = col[:, None] * freqs
    ang = jnp.concatenate([ang_r, ang_r, ang_c, ang_c], axis=-1)
    cos, sin = jnp.cos(ang), jnp.sin(ang)
    first_half = (jnp.arange(A_QK_DIM) % AXIS_DIM) < AXIS_DIM // 2
    sa = jnp.where(first_half, -sin, 0.0)
    sb = jnp.where(first_half, 0.0, sin)
    tile2 = lambda t: jnp.concatenate([t, t], axis=-1)
    return tile2(cos), tile2(sa), tile2(sb)


def _no_rope_tables(n):
    return (jnp.ones((n, V7X_LANES), _F32), jnp.zeros((n, V7X_LANES), _F32), jnp.zeros((n, V7X_LANES), _F32))


def kernel(x, c, ctx, c_ctx, w_ada, b_ada, g_norm, w_in, g_q, g_k, lam_q1, lam_k1, lam_q2, lam_k2, g_sub,
           w_f, b_f, ln_g, ln_b, w_s, b_s, w_br_a, w_br_b, w_br_c, w_out):
    b, n, d = x.shape
    n_ctx = ctx.shape[1]
    depth = w_in.shape[0]
    tm_lat = min(n, 512)
    tm_ctx = min(n_ctx, 512)

    n_rows = V7X_SUBLANES * pl.cdiv(b + 1, V7X_SUBLANES)
    rows = jnp.concatenate([c, c_ctx[None, :], jnp.zeros((n_rows - b - 1, d), _F32)], axis=0)
    mod_all = _mod_call(rows, w_ada, b_ada)

    rope_lat = _rope_tables(n)
    rope_ctx = _no_rope_tables(n_ctx)
    group_mean = np.kron(np.eye(V7X_LANES // A_QK_DIM), np.full((A_QK_DIM, A_QK_DIM), 1.0 / A_QK_DIM))
    bmat = jnp.asarray(group_mean, _BF16)

    for l in range(depth):
        last = l == depth - 1
        lam_init = 0.8 - 0.6 * math.exp(-0.3 * l)
        w_bf = w_in[l].astype(_BF16)
        wvt_bf = w_in[l][:, OFF_V:OFF_F].T.astype(_BF16)
        lamv = jnp.stack([lam_q1[l], lam_k1[l], lam_q2[l], lam_k2[l]], axis=0)
        shared = dict(
            gn=g_norm[l][None, :], w_bf=w_bf, wvt_bf=wvt_bf,
            gq=jnp.tile(g_q[l], 2)[None, :], gk=jnp.tile(g_k[l], 2)[None, :], bmat=bmat,
            lng=ln_g[l][None, :], lnb=ln_b[l][None, :], ws_bf=w_s[l].astype(_BF16),
            bsm=jnp.repeat(b_s[l].T, C_GROUP_DIM, axis=1))
        gsub = g_sub[l][None, :]
        wf_bd = jax.scipy.linalg.block_diag(*[w_f[l, gi] for gi in range(B_GROUPS)]).astype(_BF16)
        bf_row = b_f[l].reshape(1, B_WIDTH)
        merge_w = (w_br_a[l].astype(_BF16), w_br_b[l].astype(_BF16), w_br_c[l].astype(_BF16),
                   w_out[l].astype(_BF16))

        mod_lat = mod_all[l, :b][:, None, :]
        mod_ctx = jnp.broadcast_to(mod_all[l, b][None, None, :], (b, 1, 3 * d))

        cq2, ck, cvt, cf, coc, cga, cgb, cmg = _inproj_call(
            ctx, mod_ctx, cos=rope_ctx[0], sa=rope_ctx[1], sb=rope_ctx[2], tm=tm_ctx, **shared)
        q2, k, vt, f, oc, ga, gb, mg = _inproj_call(
            x, mod_lat, cos=rope_lat[0], sa=rope_lat[1], sb=rope_lat[2], tm=tm_lat, **shared)

        oa = _attn_call(lamv, q2, ga, gsub, [(k, vt), (ck, cvt)], tq=tm_lat, lam_init=lam_init)
        ob = _fourier(f, gb, wf_bd, bf_row)
        x_new = _merge_call(oa, ob, oc, mg, x, mod_lat, *merge_w, tm=tm_lat)

        if not last:
            coa = _attn_call(lamv, cq2, cga, gsub, [(ck, cvt)], tq=tm_ctx, lam_init=lam_init)
            cob = _fourier(cf, cgb, wf_bd, bf_row)
            ctx = _merge_call(coa, cob, coc, cmg, ctx, mod_ctx, *merge_w, tm=tm_ctx)
        x = x_new
    return x
```

```python
import functools
import math

import numpy as np
import jax
import jax.numpy as jnp
from jax import lax
from jax.experimental import pallas as pl
from jax.experimental.pallas import tpu as pltpu

A_HEADS = 4
A_QK_DIM = 64
A_V_DIM = 2 * A_QK_DIM
A_WIDTH = A_HEADS * A_V_DIM
B_GROUPS = 4
B_GROUP_DIM = 64
B_WIDTH = B_GROUPS * B_GROUP_DIM
C_GROUPS = 4
C_GROUP_DIM = 64
C_WIDTH = C_GROUPS * C_GROUP_DIM
CHUNK = 128
GRID_W = 64
ROPE_BASE = 10000.0
AXIS_DIM = A_QK_DIM // 2
EPS = 1e-6

OFF_Q = 0
OFF_K = OFF_Q + A_WIDTH
OFF_V = OFF_K + A_WIDTH
OFF_F = OFF_V + A_WIDTH
OFF_U = OFF_F + B_WIDTH
OFF_VC = OFF_U + C_WIDTH
OFF_GATE = OFF_VC + C_WIDTH
GATE_WIDTH = A_WIDTH + B_WIDTH + C_WIDTH
OFF_MERGE = OFF_GATE + GATE_WIDTH

V7X_LANES = 128
V7X_SUBLANES = 8
V7X_MXU_DIM = 256
V7X_VMEM_LIMIT_BYTES = 56 * 1024 * 1024

LOG2E = math.log2(math.e)
Q_SCALE = (A_QK_DIM ** -0.5) * LOG2E

_F32 = jnp.float32
_BF16 = jnp.bfloat16
_NT = (((1,), (1,)), ((), ()))


def _sigmoid(x):
    return 1.0 / (1.0 + jnp.exp(-x))


def _params(n_axes):
    return pltpu.CompilerParams(dimension_semantics=("arbitrary",) * n_axes,
                                vmem_limit_bytes=V7X_VMEM_LIMIT_BYTES)


def _mod_kernel(c_ref, w_ref, b_ref, o_ref):
    cc = c_ref[...]
    s = cc * _sigmoid(cc)
    o_ref[0] = jnp.dot(s, w_ref[0], preferred_element_type=_F32) + b_ref[0]


def _mod_call(rows, w_ada, b_ada):
    depth, d, d3 = w_ada.shape
    nb = d3 // d
    return pl.pallas_call(
        _mod_kernel,
        grid=(depth, nb),
        in_specs=[pl.BlockSpec((rows.shape[0], d), lambda l, j: (0, 0)),
                  pl.BlockSpec((1, d, d), lambda l, j: (l, 0, j)),
                  pl.BlockSpec((1, 1, d), lambda l, j: (l, 0, j))],
        out_specs=pl.BlockSpec((1, rows.shape[0], d), lambda l, j: (l, 0, j)),
        out_shape=jax.ShapeDtypeStruct((depth, rows.shape[0], d3), _F32),
        compiler_params=_params(2),
        name="adaln_mod",
    )(rows, w_ada, b_ada.reshape(depth, 1, d3))


def _inproj_kernel(x_ref, mod_ref, gn_ref, w_ref, wvt_ref, gq_ref, gk_ref, cos_ref, sa_ref, sb_ref,
                   bmat_ref, lng_ref, lnb_ref, ws_ref, bsm_ref,
                   q2_ref, k_ref, vt_ref, f_ref, oc_ref, ga_ref, gb_ref, m_ref, *, tm, d):
    x = x_ref[0]
    mod = mod_ref[0]
    shift = mod[:, :d]
    scale = mod[:, d:2 * d]
    ms = jnp.mean(x * x, axis=-1, keepdims=True)
    h = (x * lax.rsqrt(ms + EPS) * gn_ref[...]) * (1.0 + scale) + shift
    hb = h.astype(_BF16)

    def proj(off, width):
        return jnp.dot(hb, w_ref[:, off:off + width], preferred_element_type=_F32)

    cos = cos_ref[...]
    sa = sa_ref[...]
    sb = sb_ref[...]
    bmat = bmat_ref[...]
    lane = lax.broadcasted_iota(jnp.int32, (tm, V7X_LANES), 1)
    lo = lane < A_QK_DIM

    def qk_head(p_h, g):
        msq = jnp.dot((p_h * p_h).astype(_BF16), bmat, preferred_element_type=_F32)
        t = p_h * lax.rsqrt(msq + EPS) * g
        return t * cos + pltpu.roll(t, V7X_LANES - AXIS_DIM // 2, 1) * sa + pltpu.roll(t, AXIS_DIM // 2, 1) * sb

    pq = proj(OFF_Q, A_WIDTH)
    gq = gq_ref[...]
    for hh in range(A_HEADS):
        t = qk_head(pq[:, hh * A_V_DIM:(hh + 1) * A_V_DIM], gq) * Q_SCALE
        q2_ref[0, hh, 0] = jnp.where(lo, t, 0.0).astype(_BF16)
        q2_ref[0, hh, 1] = jnp.where(lo, 0.0, t).astype(_BF16)

    pk = proj(OFF_K, A_WIDTH)
    gk = gk_ref[...]
    for hh in range(A_HEADS):
        k_ref[0, hh] = qk_head(pk[:, hh * A_V_DIM:(hh + 1) * A_V_DIM], gk).astype(_BF16)

    for hh in range(A_HEADS):
        vt = lax.dot_general(wvt_ref[hh * A_V_DIM:(hh + 1) * A_V_DIM, :], hb, _NT,
                             preferred_element_type=_F32)
        vt_ref[0, hh, 0] = vt.astype(_BF16)

    f_ref[0] = proj(OFF_F, B_WIDTH)

    u = proj(OFF_U, C_WIDTH)
    vc = proj(OFF_VC, C_WIDTH)
    mu = jnp.mean(vc, axis=-1, keepdims=True)
    dv = vc - mu
    var = jnp.mean(dv * dv, axis=-1, keepdims=True)
    vn = (dv * lax.rsqrt(var + EPS) * lng_ref[...] + lnb_ref[...]).astype(_BF16)

    pg = proj(OFF_GATE, GATE_WIDTH)
    gate = pg * _sigmoid(pg)
    ga_ref[0] = gate[:, :A_WIDTH].astype(_BF16)
    gb_ref[0] = gate[:, A_WIDTH:A_WIDTH + B_WIDTH].astype(_BF16)
    gate_c = gate[:, A_WIDTH + B_WIDTH:]

    lo_c = lax.broadcasted_iota(jnp.int32, (CHUNK, V7X_LANES), 1) < C_GROUP_DIM
    bsm = bsm_ref[...]
    for t in range(tm // CHUNK):
        rows = slice(t * CHUNK, (t + 1) * CHUNK)
        vl = vn[rows, :V7X_LANES]
        vr = vn[rows, V7X_LANES:]
        s_l = jnp.where(lo_c, jnp.dot(ws_ref[0], vl, preferred_element_type=_F32),
                        jnp.dot(ws_ref[1], vl, preferred_element_type=_F32))
        s_r = jnp.where(lo_c, jnp.dot(ws_ref[2], vr, preferred_element_type=_F32),
                        jnp.dot(ws_ref[3], vr, preferred_element_type=_F32))
        s = jnp.concatenate([s_l, s_r], axis=1) + bsm
        oc_ref[0, rows, :] = (u[rows] * s * gate_c[rows]).astype(_BF16)

    for j in range(3):
        pm = proj(OFF_MERGE + j * d, d)
        m_ref[0, :, j * d:(j + 1) * d] = _sigmoid(pm).astype(_BF16)


def _inproj_call(x, mod, gn, w_bf, wvt_bf, gq, gk, cos, sa, sb, bmat, lng, lnb, ws_bf, bsm, *, tm):
    b, n, d = x.shape
    nt = n // tm
    in_w = w_bf.shape[1]
    full = lambda *shape: pl.BlockSpec(shape, lambda bi, i: (0,) * len(shape))
    in_specs = [
        pl.BlockSpec((1, tm, d), lambda bi, i: (bi, i, 0)),
        pl.BlockSpec((1, 1, 3 * d), lambda bi, i: (bi, 0, 0)),
        full(1, d),
        full(d, in_w),
        full(A_WIDTH, d),
        full(1, V7X_LANES),
        full(1, V7X_LANES),
        pl.BlockSpec((tm, V7X_LANES), lambda bi, i: (i, 0)),
        pl.BlockSpec((tm, V7X_LANES), lambda bi, i: (i, 0)),
        pl.BlockSpec((tm, V7X_LANES), lambda bi, i: (i, 0)),
        full(V7X_LANES, V7X_LANES),
        full(1, C_WIDTH),
        full(1, C_WIDTH),
        full(C_GROUPS, CHUNK, CHUNK),
        full(CHUNK, C_WIDTH),
    ]
    out_shape = [
        jax.ShapeDtypeStruct((b, A_HEADS, 2, n, A_V_DIM), _BF16),
        jax.ShapeDtypeStruct((b, A_HEADS, n, A_V_DIM), _BF16),
        jax.ShapeDtypeStruct((b, A_HEADS, nt, A_V_DIM, tm), _BF16),
        jax.ShapeDtypeStruct((b, n, B_WIDTH), _F32),
        jax.ShapeDtypeStruct((b, n, C_WIDTH), _BF16),
        jax.ShapeDtypeStruct((b, n, A_WIDTH), _BF16),
        jax.ShapeDtypeStruct((b, n, B_WIDTH), _BF16),
        jax.ShapeDtypeStruct((b, n, 3 * d), _BF16),
    ]
    out_specs = [
        pl.BlockSpec((1, A_HEADS, 2, tm, A_V_DIM), lambda bi, i: (bi, 0, 0, i, 0)),
        pl.BlockSpec((1, A_HEADS, tm, A_V_DIM), lambda bi, i: (bi, 0, i, 0)),
        pl.BlockSpec((1, A_HEADS, 1, A_V_DIM, tm), lambda bi, i: (bi, 0, i, 0, 0)),
        pl.BlockSpec((1, tm, B_WIDTH), lambda bi, i: (bi, i, 0)),
        pl.BlockSpec((1, tm, C_WIDTH), lambda bi, i: (bi, i, 0)),
        pl.BlockSpec((1, tm, A_WIDTH), lambda bi, i: (bi, i, 0)),
        pl.BlockSpec((1, tm, B_WIDTH), lambda bi, i: (bi, i, 0)),
        pl.BlockSpec((1, tm, 3 * d), lambda bi, i: (bi, i, 0)),
    ]
    return pl.pallas_call(
        functools.partial(_inproj_kernel, tm=tm, d=d),
        grid=(b, nt),
        in_specs=in_specs,
        out_specs=out_specs,
        out_shape=out_shape,
        compiler_params=_params(2),
        name="inproj",
    )(x, mod, gn, w_bf, wvt_bf, gq, gk, cos, sa, sb, bmat, lng, lnb, ws_bf, bsm)


def _attn_kernel(*refs, tq, n_src, tiles, lam_init, bounded):
    lam_ref, q2_ref, ga_ref, gsub_ref = refs[:4]
    src_refs = refs[4:4 + 2 * n_src]
    out_ref = refs[4 + 2 * n_src]
    p_sc, m_sc, l_sc, acc_sc = refs[5 + 2 * n_src:]

    q2 = q2_ref[0, 0].reshape(2 * tq, A_V_DIM)
    l_sc[...] = jnp.zeros(l_sc.shape, _F32)
    acc_sc[...] = jnp.zeros(acc_sc.shape, _F32)

    def tile_len(si):
        return src_refs[2 * si + 1].shape[-1]

    def k_tile(si, t):
        tk = tile_len(si)
        if isinstance(t, int):
            return src_refs[2 * si][0, 0, t * tk:(t + 1) * tk, :]
        return src_refs[2 * si][0, 0, pl.ds(pl.multiple_of(t * tk, tk), tk), :]

    def vt_tile(si, t):
        return src_refs[2 * si + 1][0, 0, t]

    if bounded:
        def exponentials(slot, k_t):
            p = jnp.exp2(lax.dot_general(k_t, q2, _NT, preferred_element_type=_F32))
            l_sc[...] += jnp.sum(p, axis=0, keepdims=True)
            p_sc[slot, :k_t.shape[0], :] = p.astype(_BF16)

        def weighted_values(slot, vt_t):
            acc_sc[...] += jnp.dot(vt_t, p_sc[slot, :vt_t.shape[1], :], preferred_element_type=_F32)

        flat = [(si, t) for si in range(n_src) for t in range(tiles[si])]
        exponentials(0, k_tile(*flat[0]))
        n_pairs = (tiles[0] - 1) // 2
        if n_pairs > 1:
            def body(i, carry):
                exponentials(1, k_tile(0, 2 * i + 1))
                weighted_values(0, vt_tile(0, 2 * i))
                exponentials(0, k_tile(0, 2 * i + 2))
                weighted_values(1, vt_tile(0, 2 * i + 1))
                return carry
            lax.fori_loop(0, n_pairs, body, 0)
        else:
            n_pairs = 0
        for j in range(2 * n_pairs, len(flat)):
            if j + 1 < len(flat):
                exponentials((j + 1) % 2, k_tile(*flat[j + 1]))
            weighted_values(j % 2, vt_tile(*flat[j]))
    else:
        m_sc[...] = jnp.full(m_sc.shape, -jnp.inf, _F32)

        def step(k_t, vt_t):
            s = lax.dot_general(k_t, q2, _NT, preferred_element_type=_F32)
            m_prev = m_sc[...]
            m_new = jnp.maximum(m_prev, jnp.max(s, axis=0, keepdims=True))
            alpha = jnp.exp2(m_prev - m_new)
            p = jnp.exp2(s - m_new)
            l_sc[...] = alpha * l_sc[...] + jnp.sum(p, axis=0, keepdims=True)
            acc_sc[...] = alpha * acc_sc[...] + jnp.dot(vt_t, p.astype(_BF16), preferred_element_type=_F32)
            m_sc[...] = m_new

        for si in range(n_src):
            if tiles[si] == 1:
                step(k_tile(si, 0), vt_tile(si, 0))
            else:
                def body(t, carry, si=si):
                    step(k_tile(si, t), vt_tile(si, t))
                    return carry
                lax.fori_loop(0, tiles[si], body, 0)

    lv = lam_ref[...]
    lam = (jnp.exp(jnp.sum(lv[0:1] * lv[1:2], axis=-1, keepdims=True))
           - jnp.exp(jnp.sum(lv[2:3] * lv[3:4], axis=-1, keepdims=True)) + lam_init)
    inv = 1.0 / l_sc[...]
    acc = acc_sc[...]
    o = acc[:, :tq] * inv[:, :tq] - lam * (acc[:, tq:] * inv[:, tq:])
    ms = jnp.mean(o * o, axis=0, keepdims=True)
    on = o * lax.rsqrt(ms + EPS)
    out = on.T * (gsub_ref[...] * (1.0 - lam_init)) * ga_ref[0].astype(_F32)
    out_ref[0] = out.astype(_BF16)


def _attn_call(lamv, q2, ga, gsub, sources, *, tq, lam_init, bounded):
    b, _, _, n, _ = q2.shape
    in_specs = [
        pl.BlockSpec(lamv.shape, lambda bi, hi, i: (0, 0)),
        pl.BlockSpec((1, 1, 2, tq, A_V_DIM), lambda bi, hi, i: (bi, hi, 0, i, 0)),
        pl.BlockSpec((1, tq, A_V_DIM), lambda bi, hi, i: (bi, i, hi)),
        pl.BlockSpec((1, A_V_DIM), lambda bi, hi, i: (0, 0)),
    ]
    args = [lamv, q2, ga, gsub]
    tiles = []
    for k_s, vt_s in sources:
        ls = k_s.shape[2]
        nt, _, tk = vt_s.shape[2:]
        tiles.append(nt)
        in_specs.append(pl.BlockSpec((1, 1, ls, A_V_DIM), lambda bi, hi, i: (bi, hi, 0, 0)))
        in_specs.append(pl.BlockSpec((1, 1, nt, A_V_DIM, tk), lambda bi, hi, i: (bi, hi, 0, 0, 0)))
        args += [k_s, vt_s]
    tk_max = max(vt_s.shape[-1] for _, vt_s in sources)
    return pl.pallas_call(
        functools.partial(_attn_kernel, tq=tq, n_src=len(sources), tiles=tuple(tiles),
                          lam_init=lam_init, bounded=bounded),
        grid=(b, A_HEADS, n // tq),
        in_specs=in_specs,
        out_specs=pl.BlockSpec((1, tq, A_V_DIM), lambda bi, hi, i: (bi, i, hi)),
        out_shape=jax.ShapeDtypeStruct((b, n, A_WIDTH), _BF16),
        scratch_shapes=[pltpu.VMEM((2, tk_max, 2 * tq), _BF16),
                        pltpu.VMEM((1, 2 * tq), _F32),
                        pltpu.VMEM((1, 2 * tq), _F32),
                        pltpu.VMEM((A_V_DIM, 2 * tq), _F32)],
        compiler_params=_params(3),
        name="diff_attn" if bounded else "diff_attn_general",
    )(*args)


SAFE_EXP2_RANGE = 60.0


def _attention(lamv, q2, ga, gsub, sources, gq, gk, *, tq, lam_init):
    bound = A_QK_DIM * Q_SCALE * jnp.max(jnp.abs(gq)) * jnp.max(jnp.abs(gk))
    call = lambda bounded: functools.partial(_attn_call, tq=tq, lam_init=lam_init, bounded=bounded)
    return lax.cond(bound <= SAFE_EXP2_RANGE, call(True), call(False), lamv, q2, ga, gsub, sources)


def _dft1_kernel(f1_ref, x_ref, o_ref):
    o_ref[0] = jnp.dot(f1_ref[...], x_ref[0].astype(_BF16), preferred_element_type=_F32)


def _dft1_call(f1, x2d, *, tc):
    b, n1, w = x2d.shape
    return pl.pallas_call(
        _dft1_kernel,
        grid=(b, w // tc),
        in_specs=[pl.BlockSpec((2 * n1, n1), lambda bi, j: (0, 0)),
                  pl.BlockSpec((1, n1, tc), lambda bi, j: (bi, 0, j))],
        out_specs=pl.BlockSpec((1, 2 * n1, tc), lambda bi, j: (bi, 0, j)),
        out_shape=jax.ShapeDtypeStruct((b, 2 * n1, w), _F32),
        compiler_params=_params(2),
        name="dft_stage1",
    )(f1, x2d)


def _dft2_kernel(*refs, g, n2, two_stage, norm):
    if two_stage:
        ar_ref, ai_ref, twc_ref, tws_ref, w2_ref, cbd_ref, sbd_ref, wf_ref, bf_ref, gb_ref, out_ref = refs
    else:
        ar_ref, w2_ref, cbd_ref, sbd_ref, wf_ref, bf_ref, gb_ref, out_ref = refs
    w2 = w2_ref[...]
    cbd = cbd_ref[...]
    sbd = sbd_ref[...]
    wf = wf_ref[...]
    bias = bf_ref[...]
    for j in range(g):
        ar = ar_ref[0, 0, j]
        if two_stage:
            ai = ai_ref[0, 0, j]
            tc = twc_ref[j]
            ts = tws_ref[j]
            c2 = jnp.concatenate([tc, tc], axis=1)
            s2 = jnp.concatenate([ts, ts], axis=1)
            a = jnp.concatenate([ar * c2 - ai * s2, ar * s2 + ai * c2], axis=0).astype(_BF16)
        else:
            a = ar.astype(_BF16)
        z = jnp.dot(w2, a, preferred_element_type=_F32)
        p = z[:n2].astype(_BF16)
        q = z[n2:].astype(_BF16)
        fr = (jnp.dot(p, cbd, preferred_element_type=_F32) - jnp.dot(q, sbd, preferred_element_type=_F32)) * norm
        y = jnp.dot(fr.astype(_BF16), wf, preferred_element_type=_F32) + bias
        cols = slice(j * B_WIDTH, (j + 1) * B_WIDTH)
        out_ref[0, :, cols] = (y * gb_ref[0, :, cols].astype(_F32)).astype(_BF16)


def _dft_tables(n1, n2):
    n = n1 * n2
    k1 = np.arange(n1)
    a1 = 2.0 * np.pi * ((k1[:, None] * k1[None, :]) % n1) / n1
    f1 = np.concatenate([np.cos(a1), np.sin(a1)], axis=0)
    k2 = np.arange(n2)
    a2 = 2.0 * np.pi * ((k2[:, None] * k2[None, :]) % n2) / n2
    c2, s2 = np.cos(a2), np.sin(a2)
    w2 = np.block([[c2, -s2], [s2, c2]]) if n1 > 1 else np.concatenate([c2, s2], axis=0)
    gch = np.arange(B_GROUP_DIM)
    ag = 2.0 * np.pi * ((gch[:, None] * gch[None, :]) % B_GROUP_DIM) / B_GROUP_DIM
    eye = np.eye(B_GROUPS)
    cbd = np.kron(eye, np.cos(ag))
    sbd = np.kron(eye, np.sin(ag))
    to_bf16 = lambda t: jnp.asarray(t, _F32).astype(_BF16)
    return to_bf16(f1), to_bf16(w2), to_bf16(cbd), to_bf16(sbd), 1.0 / math.sqrt(n * B_GROUP_DIM)


def _twiddles(n1, n2):
    k1 = jnp.arange(n1, dtype=jnp.int32)[:, None]
    m2 = jnp.arange(n2, dtype=jnp.int32)[None, :]
    ang = (k1 * m2).astype(_F32) * (2.0 * math.pi / (n1 * n2))
    shape = (n1, n2, V7X_LANES)
    return (jnp.broadcast_to(jnp.cos(ang)[:, :, None], shape), jnp.broadcast_to(jnp.sin(ang)[:, :, None], shape))


def _fourier(f, gb, wf_bd, bf_row):
    b, n, _ = f.shape
    two_stage = n > 2 * V7X_LANES
    n2 = V7X_LANES if two_stage else n
    n1 = n // n2
    g = min(n1, V7X_SUBLANES)
    f1, w2, cbd, sbd, norm = _dft_tables(n1, n2)
    full = lambda *shape: pl.BlockSpec(shape, lambda bi, j: (0,) * len(shape))
    gate_spec = pl.BlockSpec((1, n2, g * B_WIDTH), lambda bi, j: (bi, 0, j))
    consts = [w2, cbd, sbd, wf_bd, bf_row]
    const_specs = [full(*w2.shape), full(B_WIDTH, B_WIDTH), full(B_WIDTH, B_WIDTH), full(B_WIDTH, B_WIDTH),
                   full(1, B_WIDTH)]
    if two_stage:
        tc = min(n2 * B_WIDTH, 4096)
        a = _dft1_call(f1, f.reshape(b, n1, n2 * B_WIDTH), tc=tc)
        a5 = a.reshape(b, 2, n1, n2, B_WIDTH)
        twc, tws = _twiddles(n1, n2)
        args = [a5, a5, twc, tws] + consts
        in_specs = [pl.BlockSpec((1, 1, g, n2, B_WIDTH), lambda bi, j: (bi, 0, j, 0, 0)),
                    pl.BlockSpec((1, 1, g, n2, B_WIDTH), lambda bi, j: (bi, 1, j, 0, 0)),
                    pl.BlockSpec((g, n2, V7X_LANES), lambda bi, j: (j, 0, 0)),
                    pl.BlockSpec((g, n2, V7X_LANES), lambda bi, j: (j, 0, 0))] + const_specs
    else:
        args = [f.reshape(b, 1, 1, n2, B_WIDTH)] + consts
        in_specs = [pl.BlockSpec((1, 1, 1, n2, B_WIDTH), lambda bi, j: (bi, 0, 0, 0, 0))] + const_specs
    args.append(gb.reshape(b, n2, n1 * B_WIDTH))
    in_specs.append(gate_spec)
    out = pl.pallas_call(
        functools.partial(_dft2_kernel, g=g, n2=n2, two_stage=two_stage, norm=norm),
        grid=(b, n1 // g),
        in_specs=in_specs,
        out_specs=gate_spec,
        out_shape=jax.ShapeDtypeStruct((b, n2, n1 * B_WIDTH), _BF16),
        compiler_params=_params(2),
        name="dft_stage2",
    )(*args)
    return out.reshape(b, n, B_WIDTH)


def _merge_kernel(oa_ref, ob_ref, oc_ref, m_ref, x_ref, mod_ref, wa_ref, wb_ref, wc_ref, wo_ref, o_ref, *, d):
    ya = jnp.dot(oa_ref[0], wa_ref[...], preferred_element_type=_F32)
    yb = jnp.dot(ob_ref[0], wb_ref[...], preferred_element_type=_F32)
    yc = jnp.dot(oc_ref[0], wc_ref[...], preferred_element_type=_F32)
    m = m_ref[0]
    y = (m[:, :d].astype(_F32) * ya + m[:, d:2 * d].astype(_F32) * yb + m[:, 2 * d:].astype(_F32) * yc)
    out = jnp.dot(y.astype(_BF16), wo_ref[...], preferred_element_type=_F32)
    gate = mod_ref[0][:, 2 * d:]
    o_ref[0] = x_ref[0] + gate * out


def _merge_call(oa, ob, oc, m, x, mod, wa, wb, wc, wo, *, tm):
    b, n, d = x.shape
    row = lambda w: pl.BlockSpec((1, tm, w), lambda bi, i: (bi, i, 0))
    full = lambda *shape: pl.BlockSpec(shape, lambda bi, i: (0,) * len(shape))
    return pl.pallas_call(
        functools.partial(_merge_kernel, d=d),
        grid=(b, n // tm),
        in_specs=[row(A_WIDTH), row(B_WIDTH), row(C_WIDTH), row(3 * d), row(d),
                  pl.BlockSpec((1, 1, 3 * d), lambda bi, i: (bi, 0, 0)),
                  full(A_WIDTH, d), full(B_WIDTH, d), full(C_WIDTH, d), full(d, d)],
        out_specs=row(d),
        out_shape=jax.ShapeDtypeStruct((b, n, d), _F32),
        compiler_params=_params(2),
        name="merge",
    )(oa, ob, oc, m, x, mod, wa, wb, wc, wo)


def _rope_tables(n):
    pos = jnp.arange(n, dtype=jnp.int32)
    row = (pos // GRID_W).astype(_F32)
    col = (pos % GRID_W).astype(_F32)
    freqs = ROPE_BASE ** (-jnp.arange(0, AXIS_DIM, 2, dtype=_F32) / AXIS_DIM)
    ang_r = row[:, None] * freqs
    ang_c = col[:, None] * freqs
    ang = jnp.concatenate([ang_r, ang_r, ang_c, ang_c], axis=-1)
    cos, sin = jnp.cos(ang), jnp.sin(ang)
    first_half = (jnp.arange(A_QK_DIM) % AXIS_DIM) < AXIS_DIM // 2
    sa = jnp.where(first_half, -sin, 0.0)
    sb = jnp.where(first_half, 0.0, sin)
    tile2 = lambda t: jnp.concatenate([t, t], axis=-1)
    return tile2(cos), tile2(sa), tile2(sb)


def _no_rope_tables(n):
    return (jnp.ones((n, V7X_LANES), _F32), jnp.zeros((n, V7X_LANES), _F32), jnp.zeros((n, V7X_LANES), _F32))


def kernel(x, c, ctx, c_ctx, w_ada, b_ada, g_norm, w_in, g_q, g_k, lam_q1, lam_k1, lam_q2, lam_k2, g_sub,
           w_f, b_f, ln_g, ln_b, w_s, b_s, w_br_a, w_br_b, w_br_c, w_out):
    b, n, d = x.shape
    n_ctx = ctx.shape[1]
    depth = w_in.shape[0]
    tm_lat = min(n, 512)
    tm_ctx = min(n_ctx, 512)

    n_rows = V7X_SUBLANES * pl.cdiv(b + 1, V7X_SUBLANES)
    rows = jnp.concatenate([c, c_ctx[None, :], jnp.zeros((n_rows - b - 1, d), _F32)], axis=0)
    mod_all = _mod_call(rows, w_ada, b_ada)

    rope_lat = _rope_tables(n)
    rope_ctx = _no_rope_tables(n_ctx)
    group_mean = np.kron(np.eye(V7X_LANES // A_QK_DIM), np.full((A_QK_DIM, A_QK_DIM), 1.0 / A_QK_DIM))
    bmat = jnp.asarray(group_mean, _BF16)

    for l in range(depth):
        last = l == depth - 1
        lam_init = 0.8 - 0.6 * math.exp(-0.3 * l)
        w_bf = w_in[l].astype(_BF16)
        wvt_bf = w_in[l][:, OFF_V:OFF_F].T.astype(_BF16)
        lamv = jnp.stack([lam_q1[l], lam_k1[l], lam_q2[l], lam_k2[l]], axis=0)
        shared = dict(
            gn=g_norm[l][None, :], w_bf=w_bf, wvt_bf=wvt_bf,
            gq=jnp.tile(g_q[l], 2)[None, :], gk=jnp.tile(g_k[l], 2)[None, :], bmat=bmat,
            lng=ln_g[l][None, :], lnb=ln_b[l][None, :], ws_bf=w_s[l].astype(_BF16),
            bsm=jnp.repeat(b_s[l].T, C_GROUP_DIM, axis=1))
        gsub = g_sub[l][None, :]
        wf_bd = jax.scipy.linalg.block_diag(*[w_f[l, gi] for gi in range(B_GROUPS)]).astype(_BF16)
        bf_row = b_f[l].reshape(1, B_WIDTH)
        merge_w = (w_br_a[l].astype(_BF16), w_br_b[l].astype(_BF16), w_br_c[l].astype(_BF16),
                   w_out[l].astype(_BF16))

        mod_lat = mod_all[l, :b][:, None, :]
        mod_ctx = jnp.broadcast_to(mod_all[l, b][None, None, :], (b, 1, 3 * d))

        cq2, ck, cvt, cf, coc, cga, cgb, cmg = _inproj_call(
            ctx, mod_ctx, cos=rope_ctx[0], sa=rope_ctx[1], sb=rope_ctx[2], tm=tm_ctx, **shared)
        q2, k, vt, f, oc, ga, gb, mg = _inproj_call(
            x, mod_lat, cos=rope_lat[0], sa=rope_lat[1], sb=rope_lat[2], tm=tm_lat, **shared)

        oa = _attention(lamv, q2, ga, gsub, [(k, vt), (ck, cvt)], g_q[l], g_k[l], tq=tm_lat, lam_init=lam_init)
        ob = _fourier(f, gb, wf_bd, bf_row)
        x_new = _merge_call(oa, ob, oc, mg, x, mod_lat, *merge_w, tm=tm_lat)

        if not last:
            coa = _attention(lamv, cq2, cga, gsub, [(ck, cvt)], g_q[l], g_k[l], tq=tm_ctx, lam_init=lam_init)
            cob = _fourier(cf, cgb, wf_bd, bf_row)
            ctx = _merge_call(coa, cob, coc, cmg, ctx, mod_ctx, *merge_w, tm=tm_ctx)
        x = x_new
    return x
```

```python
import functools
import math

import numpy as np
import jax
import jax.numpy as jnp
from jax import lax
from jax.experimental import pallas as pl
from jax.experimental.pallas import tpu as pltpu

A_HEADS = 4
A_QK_DIM = 64
A_V_DIM = 2 * A_QK_DIM
A_WIDTH = A_HEADS * A_V_DIM
B_GROUPS = 4
B_GROUP_DIM = 64
B_WIDTH = B_GROUPS * B_GROUP_DIM
C_GROUPS = 4
C_GROUP_DIM = 64
C_WIDTH = C_GROUPS * C_GROUP_DIM
CHUNK = 128
GRID_W = 64
ROPE_BASE = 10000.0
AXIS_DIM = A_QK_DIM // 2
EPS = 1e-6

OFF_Q = 0
OFF_K = OFF_Q + A_WIDTH
OFF_V = OFF_K + A_WIDTH
OFF_F = OFF_V + A_WIDTH
OFF_U = OFF_F + B_WIDTH
OFF_VC = OFF_U + C_WIDTH
OFF_GATE = OFF_VC + C_WIDTH
GATE_WIDTH = A_WIDTH + B_WIDTH + C_WIDTH
OFF_MERGE = OFF_GATE + GATE_WIDTH

V7X_LANES = 128
V7X_SUBLANES = 8
V7X_MXU_DIM = 256
V7X_VMEM_LIMIT_BYTES = 56 * 1024 * 1024

LOG2E = math.log2(math.e)
Q_SCALE = (A_QK_DIM ** -0.5) * LOG2E

_F32 = jnp.float32
_BF16 = jnp.bfloat16
_NT = (((1,), (1,)), ((), ()))


def _sigmoid(x):
    return 1.0 / (1.0 + jnp.exp(-x))


def _params(n_axes):
    return pltpu.CompilerParams(dimension_semantics=("arbitrary",) * n_axes,
                                vmem_limit_bytes=V7X_VMEM_LIMIT_BYTES)


def _mod_kernel(c_ref, w_ref, b_ref, o_ref):
    cc = c_ref[...]
    s = cc * _sigmoid(cc)
    o_ref[0] = jnp.dot(s, w_ref[0], preferred_element_type=_F32) + b_ref[0]


def _mod_call(rows, w_ada, b_ada):
    depth, d, d3 = w_ada.shape
    nb = d3 // d
    return pl.pallas_call(
        _mod_kernel,
        grid=(depth, nb),
        in_specs=[pl.BlockSpec((rows.shape[0], d), lambda l, j: (0, 0)),
                  pl.BlockSpec((1, d, d), lambda l, j: (l, 0, j)),
                  pl.BlockSpec((1, 1, d), lambda l, j: (l, 0, j))],
        out_specs=pl.BlockSpec((1, rows.shape[0], d), lambda l, j: (l, 0, j)),
        out_shape=jax.ShapeDtypeStruct((depth, rows.shape[0], d3), _F32),
        compiler_params=_params(2),
        name="adaln_mod",
    )(rows, w_ada, b_ada.reshape(depth, 1, d3))


def _inproj_kernel(x_ref, mod_ref, gn_ref, w_ref, wvt_ref, gq_ref, gk_ref, cos_ref, sa_ref, sb_ref,
                   bmat_ref, lng_ref, lnb_ref, ws_ref, bsm_ref,
                   q2_ref, k_ref, vt_ref, f_ref, oc_ref, ga_ref, gb_ref, m_ref, *, tm, d):
    x = x_ref[0]
    mod = mod_ref[0]
    shift = mod[:, :d]
    scale = mod[:, d:2 * d]
    ms = jnp.mean(x * x, axis=-1, keepdims=True)
    h = (x * lax.rsqrt(ms + EPS) * gn_ref[...]) * (1.0 + scale) + shift
    hb = h.astype(_BF16)

    def proj(off, width):
        return jnp.dot(hb, w_ref[:, off:off + width], preferred_element_type=_F32)

    cos = cos_ref[...]
    sa = sa_ref[...]
    sb = sb_ref[...]
    bmat = bmat_ref[...]
    lane = lax.broadcasted_iota(jnp.int32, (tm, V7X_LANES), 1)
    lo = lane < A_QK_DIM

    def qk_head(p_h, g):
        msq = jnp.dot((p_h * p_h).astype(_BF16), bmat, preferred_element_type=_F32)
        t = p_h * lax.rsqrt(msq + EPS) * g
        return t * cos + pltpu.roll(t, V7X_LANES - AXIS_DIM // 2, 1) * sa + pltpu.roll(t, AXIS_DIM // 2, 1) * sb

    pq = proj(OFF_Q, A_WIDTH)
    gq = gq_ref[...]
    for hh in range(A_HEADS):
        t = qk_head(pq[:, hh * A_V_DIM:(hh + 1) * A_V_DIM], gq) * Q_SCALE
        q2_ref[0, hh, 0] = jnp.where(lo, t, 0.0).astype(_BF16)
        q2_ref[0, hh, 1] = jnp.where(lo, 0.0, t).astype(_BF16)

    pk = proj(OFF_K, A_WIDTH)
    gk = gk_ref[...]
    for hh in range(A_HEADS):
        k_ref[0, hh] = qk_head(pk[:, hh * A_V_DIM:(hh + 1) * A_V_DIM], gk).astype(_BF16)

    for hh in range(A_HEADS):
        vt = lax.dot_general(wvt_ref[hh * A_V_DIM:(hh + 1) * A_V_DIM, :], hb, _NT,
                             preferred_element_type=_F32)
        vt_ref[0, hh, 0] = vt.astype(_BF16)

    f_ref[0] = proj(OFF_F, B_WIDTH)

    u = proj(OFF_U, C_WIDTH)
    vc = proj(OFF_VC, C_WIDTH)
    mu = jnp.mean(vc, axis=-1, keepdims=True)
    dv = vc - mu
    var = jnp.mean(dv * dv, axis=-1, keepdims=True)
    vn = (dv * lax.rsqrt(var + EPS) * lng_ref[...] + lnb_ref[...]).astype(_BF16)

    pg = proj(OFF_GATE, GATE_WIDTH)
    gate = pg * _sigmoid(pg)
    ga_ref[0] = gate[:, :A_WIDTH].astype(_BF16)
    gb_ref[0] = gate[:, A_WIDTH:A_WIDTH + B_WIDTH].astype(_BF16)
    gate_c = gate[:, A_WIDTH + B_WIDTH:]

    lo_c = lax.broadcasted_iota(jnp.int32, (CHUNK, V7X_LANES), 1) < C_GROUP_DIM
    bsm = bsm_ref[...]
    for t in range(tm // CHUNK):
        rows = slice(t * CHUNK, (t + 1) * CHUNK)
        vl = vn[rows, :V7X_LANES]
        vr = vn[rows, V7X_LANES:]
        s_l = jnp.where(lo_c, jnp.dot(ws_ref[0], vl, preferred_element_type=_F32),
                        jnp.dot(ws_ref[1], vl, preferred_element_type=_F32))
        s_r = jnp.where(lo_c, jnp.dot(ws_ref[2], vr, preferred_element_type=_F32),
                        jnp.dot(ws_ref[3], vr, preferred_element_type=_F32))
        s = jnp.concatenate([s_l, s_r], axis=1) + bsm
        oc_ref[0, rows, :] = (u[rows] * s * gate_c[rows]).astype(_BF16)

    for j in range(3):
        pm = proj(OFF_MERGE + j * d, d)
        m_ref[0, :, j * d:(j + 1) * d] = _sigmoid(pm).astype(_BF16)


def _inproj_call(x, mod, gn, w_bf, wvt_bf, gq, gk, cos, sa, sb, bmat, lng, lnb, ws_bf, bsm, *, tm):
    b, n, d = x.shape
    nt = n // tm
    in_w = w_bf.shape[1]
    full = lambda *shape: pl.BlockSpec(shape, lambda bi, i: (0,) * len(shape))
    in_specs = [
        pl.BlockSpec((1, tm, d), lambda bi, i: (bi, i, 0)),
        pl.BlockSpec((1, 1, 3 * d), lambda bi, i: (bi, 0, 0)),
        full(1, d),
        full(d, in_w),
        full(A_WIDTH, d),
        full(1, V7X_LANES),
        full(1, V7X_LANES),
        pl.BlockSpec((tm, V7X_LANES), lambda bi, i: (i, 0)),
        pl.BlockSpec((tm, V7X_LANES), lambda bi, i: (i, 0)),
        pl.BlockSpec((tm, V7X_LANES), lambda bi, i: (i, 0)),
        full(V7X_LANES, V7X_LANES),
        full(1, C_WIDTH),
        full(1, C_WIDTH),
        full(C_GROUPS, CHUNK, CHUNK),
        full(CHUNK, C_WIDTH),
    ]
    out_shape = [
        jax.ShapeDtypeStruct((b, A_HEADS, 2, n, A_V_DIM), _BF16),
        jax.ShapeDtypeStruct((b, A_HEADS, n, A_V_DIM), _BF16),
        jax.ShapeDtypeStruct((b, A_HEADS, nt, A_V_DIM, tm), _BF16),
        jax.ShapeDtypeStruct((b, n, B_WIDTH), _F32),
        jax.ShapeDtypeStruct((b, n, C_WIDTH), _BF16),
        jax.ShapeDtypeStruct((b, n, A_WIDTH), _BF16),
        jax.ShapeDtypeStruct((b, n, B_WIDTH), _BF16),
        jax.ShapeDtypeStruct((b, n, 3 * d), _BF16),
    ]
    out_specs = [
        pl.BlockSpec((1, A_HEADS, 2, tm, A_V_DIM), lambda bi, i: (bi, 0, 0, i, 0)),
        pl.BlockSpec((1, A_HEADS, tm, A_V_DIM), lambda bi, i: (bi, 0, i, 0)),
        pl.BlockSpec((1, A_HEADS, 1, A_V_DIM, tm), lambda bi, i: (bi, 0, i, 0, 0)),
        pl.BlockSpec((1, tm, B_WIDTH), lambda bi, i: (bi, i, 0)),
        pl.BlockSpec((1, tm, C_WIDTH), lambda bi, i: (bi, i, 0)),
        pl.BlockSpec((1, tm, A_WIDTH), lambda bi, i: (bi, i, 0)),
        pl.BlockSpec((1, tm, B_WIDTH), lambda bi, i: (bi, i, 0)),
        pl.BlockSpec((1, tm, 3 * d), lambda bi, i: (bi, i, 0)),
    ]
    return pl.pallas_call(
        functools.partial(_inproj_kernel, tm=tm, d=d),
        grid=(b, nt),
        in_specs=in_specs,
        out_specs=out_specs,
        out_shape=out_shape,
        compiler_params=_params(2),
        name="inproj",
    )(x, mod, gn, w_bf, wvt_bf, gq, gk, cos, sa, sb, bmat, lng, lnb, ws_bf, bsm)


def _attn_kernel(*refs, tq, n_src, tiles, lam_init, bounded, unroll=True):
    lam_ref, q2_ref, ga_ref, gsub_ref = refs[:4]
    src_refs = refs[4:4 + 2 * n_src]
    out_ref = refs[4 + 2 * n_src]
    p_sc, m_sc, l_sc, acc_sc = refs[5 + 2 * n_src:]

    q2 = q2_ref[0, 0].reshape(2 * tq, A_V_DIM)
    l_sc[...] = jnp.zeros(l_sc.shape, _F32)
    acc_sc[...] = jnp.zeros(acc_sc.shape, _F32)

    def tile_len(si):
        return src_refs[2 * si + 1].shape[-1]

    def k_tile(si, t):
        tk = tile_len(si)
        if isinstance(t, int):
            return src_refs[2 * si][0, 0, t * tk:(t + 1) * tk, :]
        return src_refs[2 * si][0, 0, pl.ds(pl.multiple_of(t * tk, tk), tk), :]

    def vt_tile(si, t):
        return src_refs[2 * si + 1][0, 0, t]

    if bounded:
        def exponentials(slot, k_t):
            p = jnp.exp2(lax.dot_general(k_t, q2, _NT, preferred_element_type=_F32))
            l_sc[...] += jnp.sum(p, axis=0, keepdims=True)
            p_sc[slot, :k_t.shape[0], :] = p.astype(_BF16)

        def weighted_values(slot, vt_t):
            acc_sc[...] += jnp.dot(vt_t, p_sc[slot, :vt_t.shape[1], :], preferred_element_type=_F32)

        flat = [(si, t) for si in range(n_src) for t in range(tiles[si])]
        exponentials(0, k_tile(*flat[0]))
        n_pairs = (tiles[0] - 1) // 2
        if n_pairs > 1 and not unroll:
            def body(i, carry):
                exponentials(1, k_tile(0, 2 * i + 1))
                weighted_values(0, vt_tile(0, 2 * i))
                exponentials(0, k_tile(0, 2 * i + 2))
                weighted_values(1, vt_tile(0, 2 * i + 1))
                return carry
            lax.fori_loop(0, n_pairs, body, 0)
        else:
            n_pairs = 0
        for j in range(2 * n_pairs, len(flat)):
            if j + 1 < len(flat):
                exponentials((j + 1) % 2, k_tile(*flat[j + 1]))
            weighted_values(j % 2, vt_tile(*flat[j]))
    else:
        m_sc[...] = jnp.full(m_sc.shape, -jnp.inf, _F32)

        def step(k_t, vt_t):
            s = lax.dot_general(k_t, q2, _NT, preferred_element_type=_F32)
            m_prev = m_sc[...]
            m_new = jnp.maximum(m_prev, jnp.max(s, axis=0, keepdims=True))
            alpha = jnp.exp2(m_prev - m_new)
            p = jnp.exp2(s - m_new)
            l_sc[...] = alpha * l_sc[...] + jnp.sum(p, axis=0, keepdims=True)
            acc_sc[...] = alpha * acc_sc[...] + jnp.dot(vt_t, p.astype(_BF16), preferred_element_type=_F32)
            m_sc[...] = m_new

        for si in range(n_src):
            if tiles[si] == 1:
                step(k_tile(si, 0), vt_tile(si, 0))
            else:
                def body(t, carry, si=si):
                    step(k_tile(si, t), vt_tile(si, t))
                    return carry
                lax.fori_loop(0, tiles[si], body, 0)

    lv = lam_ref[...]
    lam = (jnp.exp(jnp.sum(lv[0:1] * lv[1:2], axis=-1, keepdims=True))
           - jnp.exp(jnp.sum(lv[2:3] * lv[3:4], axis=-1, keepdims=True)) + lam_init)
    inv = 1.0 / l_sc[...]
    acc = acc_sc[...]
    o = acc[:, :tq] * inv[:, :tq] - lam * (acc[:, tq:] * inv[:, tq:])
    ms = jnp.mean(o * o, axis=0, keepdims=True)
    on = o * lax.rsqrt(ms + EPS)
    out = on.T * (gsub_ref[...] * (1.0 - lam_init)) * ga_ref[0].astype(_F32)
    out_ref[0] = out.astype(_BF16)


def _attn_call(lamv, q2, ga, gsub, sources, *, tq, lam_init, bounded):
    b, _, _, n, _ = q2.shape
    in_specs = [
        pl.BlockSpec(lamv.shape, lambda bi, hi, i: (0, 0)),
        pl.BlockSpec((1, 1, 2, tq, A_V_DIM), lambda bi, hi, i: (bi, hi, 0, i, 0)),
        pl.BlockSpec((1, tq, A_V_DIM), lambda bi, hi, i: (bi, i, hi)),
        pl.BlockSpec((1, A_V_DIM), lambda bi, hi, i: (0, 0)),
    ]
    args = [lamv, q2, ga, gsub]
    tiles = []
    for k_s, vt_s in sources:
        ls = k_s.shape[2]
        nt, _, tk = vt_s.shape[2:]
        tiles.append(nt)
        in_specs.append(pl.BlockSpec((1, 1, ls, A_V_DIM), lambda bi, hi, i: (bi, hi, 0, 0)))
        in_specs.append(pl.BlockSpec((1, 1, nt, A_V_DIM, tk), lambda bi, hi, i: (bi, hi, 0, 0, 0)))
        args += [k_s, vt_s]
    tk_max = max(vt_s.shape[-1] for _, vt_s in sources)
    return pl.pallas_call(
        functools.partial(_attn_kernel, tq=tq, n_src=len(sources), tiles=tuple(tiles),
                          lam_init=lam_init, bounded=bounded),
        grid=(b, A_HEADS, n // tq),
        in_specs=in_specs,
        out_specs=pl.BlockSpec((1, tq, A_V_DIM), lambda bi, hi, i: (bi, i, hi)),
        out_shape=jax.ShapeDtypeStruct((b, n, A_WIDTH), _BF16),
        scratch_shapes=[pltpu.VMEM((2, tk_max, 2 * tq), _BF16),
                        pltpu.VMEM((1, 2 * tq), _F32),
                        pltpu.VMEM((1, 2 * tq), _F32),
                        pltpu.VMEM((A_V_DIM, 2 * tq), _F32)],
        compiler_params=_params(3),
        name="diff_attn" if bounded else "diff_attn_general",
    )(*args)


SAFE_EXP2_RANGE = 60.0


def _attention(lamv, q2, ga, gsub, sources, gq, gk, *, tq, lam_init):
    bound = A_QK_DIM * Q_SCALE * jnp.max(jnp.abs(gq)) * jnp.max(jnp.abs(gk))
    call = lambda bounded: functools.partial(_attn_call, tq=tq, lam_init=lam_init, bounded=bounded)
    return lax.cond(bound <= SAFE_EXP2_RANGE, call(True), call(False), lamv, q2, ga, gsub, sources)


def _dft1_kernel(f1_ref, x_ref, o_ref):
    o_ref[0] = jnp.dot(f1_ref[...], x_ref[0].astype(_BF16), preferred_element_type=_F32)


def _dft1_call(f1, x2d, *, tc):
    b, n1, w = x2d.shape
    return pl.pallas_call(
        _dft1_kernel,
        grid=(b, w // tc),
        in_specs=[pl.BlockSpec((2 * n1, n1), lambda bi, j: (0, 0)),
                  pl.BlockSpec((1, n1, tc), lambda bi, j: (bi, 0, j))],
        out_specs=pl.BlockSpec((1, 2 * n1, tc), lambda bi, j: (bi, 0, j)),
        out_shape=jax.ShapeDtypeStruct((b, 2 * n1, w), _F32),
        compiler_params=_params(2),
        name="dft_stage1",
    )(f1, x2d)


def _dft2_kernel(*refs, g, n2, two_stage, norm):
    if two_stage:
        ar_ref, ai_ref, twc_ref, tws_ref, w2_ref, cbd_ref, sbd_ref, wf_ref, bf_ref, gb_ref, out_ref = refs
    else:
        ar_ref, w2_ref, cbd_ref, sbd_ref, wf_ref, bf_ref, gb_ref, out_ref = refs
    w2 = w2_ref[...]
    cbd = cbd_ref[...]
    sbd = sbd_ref[...]
    wf = wf_ref[...]
    bias = bf_ref[...]
    for j in range(g):
        ar = ar_ref[0, 0, j]
        if two_stage:
            ai = ai_ref[0, 0, j]
            tc = twc_ref[j]
            ts = tws_ref[j]
            c2 = jnp.concatenate([tc, tc], axis=1)
            s2 = jnp.concatenate([ts, ts], axis=1)
            a = jnp.concatenate([ar * c2 - ai * s2, ar * s2 + ai * c2], axis=0).astype(_BF16)
        else:
            a = ar.astype(_BF16)
        z = jnp.dot(w2, a, preferred_element_type=_F32)
        p = z[:n2].astype(_BF16)
        q = z[n2:].astype(_BF16)
        fr = (jnp.dot(p, cbd, preferred_element_type=_F32) - jnp.dot(q, sbd, preferred_element_type=_F32)) * norm
        y = jnp.dot(fr.astype(_BF16), wf, preferred_element_type=_F32) + bias
        cols = slice(j * B_WIDTH, (j + 1) * B_WIDTH)
        out_ref[0, :, cols] = (y * gb_ref[0, :, cols].astype(_F32)).astype(_BF16)


def _dft_tables(n1, n2):
    n = n1 * n2
    k1 = np.arange(n1)
    a1 = 2.0 * np.pi * ((k1[:, None] * k1[None, :]) % n1) / n1
    f1 = np.concatenate([np.cos(a1), np.sin(a1)], axis=0)
    k2 = np.arange(n2)
    a2 = 2.0 * np.pi * ((k2[:, None] * k2[None, :]) % n2) / n2
    c2, s2 = np.cos(a2), np.sin(a2)
    w2 = np.block([[c2, -s2], [s2, c2]]) if n1 > 1 else np.concatenate([c2, s2], axis=0)
    gch = np.arange(B_GROUP_DIM)
    ag = 2.0 * np.pi * ((gch[:, None] * gch[None, :]) % B_GROUP_DIM) / B_GROUP_DIM
    eye = np.eye(B_GROUPS)
    cbd = np.kron(eye, np.cos(ag))
    sbd = np.kron(eye, np.sin(ag))
    to_bf16 = lambda t: jnp.asarray(t, _F32).astype(_BF16)
    return to_bf16(f1), to_bf16(w2), to_bf16(cbd), to_bf16(sbd), 1.0 / math.sqrt(n * B_GROUP_DIM)


def _twiddles(n1, n2):
    k1 = jnp.arange(n1, dtype=jnp.int32)[:, None]
    m2 = jnp.arange(n2, dtype=jnp.int32)[None, :]
    ang = (k1 * m2).astype(_F32) * (2.0 * math.pi / (n1 * n2))
    shape = (n1, n2, V7X_LANES)
    return (jnp.broadcast_to(jnp.cos(ang)[:, :, None], shape), jnp.broadcast_to(jnp.sin(ang)[:, :, None], shape))


def _fourier(f, gb, wf_bd, bf_row):
    b, n, _ = f.shape
    two_stage = n > 2 * V7X_LANES
    n2 = V7X_LANES if two_stage else n
    n1 = n // n2
    g = min(n1, V7X_SUBLANES)
    f1, w2, cbd, sbd, norm = _dft_tables(n1, n2)
    full = lambda *shape: pl.BlockSpec(shape, lambda bi, j: (0,) * len(shape))
    gate_spec = pl.BlockSpec((1, n2, g * B_WIDTH), lambda bi, j: (bi, 0, j))
    consts = [w2, cbd, sbd, wf_bd, bf_row]
    const_specs = [full(*w2.shape), full(B_WIDTH, B_WIDTH), full(B_WIDTH, B_WIDTH), full(B_WIDTH, B_WIDTH),
                   full(1, B_WIDTH)]
    if two_stage:
        tc = min(n2 * B_WIDTH, 4096)
        a = _dft1_call(f1, f.reshape(b, n1, n2 * B_WIDTH), tc=tc)
        a5 = a.reshape(b, 2, n1, n2, B_WIDTH)
        twc, tws = _twiddles(n1, n2)
        args = [a5, a5, twc, tws] + consts
        in_specs = [pl.BlockSpec((1, 1, g, n2, B_WIDTH), lambda bi, j: (bi, 0, j, 0, 0)),
                    pl.BlockSpec((1, 1, g, n2, B_WIDTH), lambda bi, j: (bi, 1, j, 0, 0)),
                    pl.BlockSpec((g, n2, V7X_LANES), lambda bi, j: (j, 0, 0)),
                    pl.BlockSpec((g, n2, V7X_LANES), lambda bi, j: (j, 0, 0))] + const_specs
    else:
        args = [f.reshape(b, 1, 1, n2, B_WIDTH)] + consts
        in_specs = [pl.BlockSpec((1, 1, 1, n2, B_WIDTH), lambda bi, j: (bi, 0, 0, 0, 0))] + const_specs
    args.append(gb.reshape(b, n2, n1 * B_WIDTH))
    in_specs.append(gate_spec)
    out = pl.pallas_call(
        functools.partial(_dft2_kernel, g=g, n2=n2, two_stage=two_stage, norm=norm),
        grid=(b, n1 // g),
        in_specs=in_specs,
        out_specs=gate_spec,
        out_shape=jax.ShapeDtypeStruct((b, n2, n1 * B_WIDTH), _BF16),
        compiler_params=_params(2),
        name="dft_stage2",
    )(*args)
    return out.reshape(b, n, B_WIDTH)


def _merge_kernel(oa_ref, ob_ref, oc_ref, m_ref, x_ref, mod_ref, wa_ref, wb_ref, wc_ref, wo_ref, o_ref, *, d):
    ya = jnp.dot(oa_ref[0], wa_ref[...], preferred_element_type=_F32)
    yb = jnp.dot(ob_ref[0], wb_ref[...], preferred_element_type=_F32)
    yc = jnp.dot(oc_ref[0], wc_ref[...], preferred_element_type=_F32)
    m = m_ref[0]
    y = (m[:, :d].astype(_F32) * ya + m[:, d:2 * d].astype(_F32) * yb + m[:, 2 * d:].astype(_F32) * yc)
    out = jnp.dot(y.astype(_BF16), wo_ref[...], preferred_element_type=_F32)
    gate = mod_ref[0][:, 2 * d:]
    o_ref[0] = x_ref[0] + gate * out


def _merge_call(oa, ob, oc, m, x, mod, wa, wb, wc, wo, *, tm):
    b, n, d = x.shape
    row = lambda w: pl.BlockSpec((1, tm, w), lambda bi, i: (bi, i, 0))
    full = lambda *shape: pl.BlockSpec(shape, lambda bi, i: (0,) * len(shape))
    return pl.pallas_call(
        functools.partial(_merge_kernel, d=d),
        grid=(b, n // tm),
        in_specs=[row(A_WIDTH), row(B_WIDTH), row(C_WIDTH), row(3 * d), row(d),
                  pl.BlockSpec((1, 1, 3 * d), lambda bi, i: (bi, 0, 0)),
                  full(A_WIDTH, d), full(B_WIDTH, d), full(C_WIDTH, d), full(d, d)],
        out_specs=row(d),
        out_shape=jax.ShapeDtypeStruct((b, n, d), _F32),
        compiler_params=_params(2),
        name="merge",
    )(oa, ob, oc, m, x, mod, wa, wb, wc, wo)


def _rope_tables(n):
    pos = jnp.arange(n, dtype=jnp.int32)
    row = (pos // GRID_W).astype(_F32)
    col = (pos % GRID_W).astype(_F32)
    freqs = ROPE_BASE ** (-jnp.arange(0, AXIS_DIM, 2, dtype=_F32) / AXIS_DIM)
    ang_r = row[:, None] * freqs
    ang_c = col[:, None] * freqs
    ang = jnp.concatenate([ang_r, ang_r, ang_c, ang_c], axis=-1)
    cos, sin = jnp.cos(ang), jnp.sin(ang)
    first_half = (jnp.arange(A_QK_DIM) % AXIS_DIM) < AXIS_DIM // 2
    sa = jnp.where(first_half, -sin, 0.0)
    sb = jnp.where(first_half, 0.0, sin)
    tile2 = lambda t: jnp.concatenate([t, t], axis=-1)
    return tile2(cos), tile2(sa), tile2(sb)


def _no_rope_tables(n):
    return (jnp.ones((n, V7X_LANES), _F32), jnp.zeros((n, V7X_LANES), _F32), jnp.zeros((n, V7X_LANES), _F32))


def kernel(x, c, ctx, c_ctx, w_ada, b_ada, g_norm, w_in, g_q, g_k, lam_q1, lam_k1, lam_q2, lam_k2, g_sub,
           w_f, b_f, ln_g, ln_b, w_s, b_s, w_br_a, w_br_b, w_br_c, w_out):
    b, n, d = x.shape
    n_ctx = ctx.shape[1]
    depth = w_in.shape[0]
    tm_lat = min(n, 512)
    tm_ctx = min(n_ctx, 512)

    n_rows = V7X_SUBLANES * pl.cdiv(b + 1, V7X_SUBLANES)
    rows = jnp.concatenate([c, c_ctx[None, :], jnp.zeros((n_rows - b - 1, d), _F32)], axis=0)
    mod_all = _mod_call(rows, w_ada, b_ada)

    rope_lat = _rope_tables(n)
    rope_ctx = _no_rope_tables(n_ctx)
    group_mean = np.kron(np.eye(V7X_LANES // A_QK_DIM), np.full((A_QK_DIM, A_QK_DIM), 1.0 / A_QK_DIM))
    bmat = jnp.asarray(group_mean, _BF16)

    for l in range(depth):
        last = l == depth - 1
        lam_init = 0.8 - 0.6 * math.exp(-0.3 * l)
        w_bf = w_in[l].astype(_BF16)
        wvt_bf = w_in[l][:, OFF_V:OFF_F].T.astype(_BF16)
        lamv = jnp.stack([lam_q1[l], lam_k1[l], lam_q2[l], lam_k2[l]], axis=0)
        shared = dict(
            gn=g_norm[l][None, :], w_bf=w_bf, wvt_bf=wvt_bf,
            gq=jnp.tile(g_q[l], 2)[None, :], gk=jnp.tile(g_k[l], 2)[None, :], bmat=bmat,
            lng=ln_g[l][None, :], lnb=ln_b[l][None, :], ws_bf=w_s[l].astype(_BF16),
            bsm=jnp.repeat(b_s[l].T, C_GROUP_DIM, axis=1))
        gsub = g_sub[l][None, :]
        wf_bd = jax.scipy.linalg.block_diag(*[w_f[l, gi] for gi in range(B_GROUPS)]).astype(_BF16)
        bf_row = b_f[l].reshape(1, B_WIDTH)
        merge_w = (w_br_a[l].astype(_BF16), w_br_b[l].astype(_BF16), w_br_c[l].astype(_BF16),
                   w_out[l].astype(_BF16))

        mod_lat = mod_all[l, :b][:, None, :]
        mod_ctx = jnp.broadcast_to(mod_all[l, b][None, None, :], (b, 1, 3 * d))

        cq2, ck, cvt, cf, coc, cga, cgb, cmg = _inproj_call(
            ctx, mod_ctx, cos=rope_ctx[0], sa=rope_ctx[1], sb=rope_ctx[2], tm=tm_ctx, **shared)
        q2, k, vt, f, oc, ga, gb, mg = _inproj_call(
            x, mod_lat, cos=rope_lat[0], sa=rope_lat[1], sb=rope_lat[2], tm=tm_lat, **shared)

        oa = _attention(lamv, q2, ga, gsub, [(k, vt), (ck, cvt)], g_q[l], g_k[l], tq=tm_lat, lam_init=lam_init)
        ob = _fourier(f, gb, wf_bd, bf_row)
        x_new = _merge_call(oa, ob, oc, mg, x, mod_lat, *merge_w, tm=tm_lat)

        if not last:
            coa = _attention(lamv, cq2, cga, gsub, [(ck, cvt)], g_q[l], g_k[l], tq=tm_ctx, lam_init=lam_init)
            cob = _fourier(cf, cgb, wf_bd, bf_row)
            ctx = _merge_call(coa, cob, coc, cmg, ctx, mod_ctx, *merge_w, tm=tm_ctx)
        x = x_new
    return x
```

```python
import functools
import math

import numpy as np
import jax
import jax.numpy as jnp
from jax import lax
from jax.experimental import pallas as pl
from jax.experimental.pallas import tpu as pltpu

A_HEADS = 4
A_QK_DIM = 64
A_V_DIM = 2 * A_QK_DIM
A_WIDTH = A_HEADS * A_V_DIM
B_GROUPS = 4
B_GROUP_DIM = 64
B_WIDTH = B_GROUPS * B_GROUP_DIM
C_GROUPS = 4
C_GROUP_DIM = 64
C_WIDTH = C_GROUPS * C_GROUP_DIM
CHUNK = 128
GRID_W = 64
ROPE_BASE = 10000.0
AXIS_DIM = A_QK_DIM // 2
EPS = 1e-6

OFF_Q = 0
OFF_K = OFF_Q + A_WIDTH
OFF_V = OFF_K + A_WIDTH
OFF_F = OFF_V + A_WIDTH
OFF_U = OFF_F + B_WIDTH
OFF_VC = OFF_U + C_WIDTH
OFF_GATE = OFF_VC + C_WIDTH
GATE_WIDTH = A_WIDTH + B_WIDTH + C_WIDTH
OFF_MERGE = OFF_GATE + GATE_WIDTH

V7X_LANES = 128
V7X_SUBLANES = 8
V7X_MXU_DIM = 256
V7X_VMEM_LIMIT_BYTES = 56 * 1024 * 1024

LOG2E = math.log2(math.e)
Q_SCALE = (A_QK_DIM ** -0.5) * LOG2E

_F32 = jnp.float32
_BF16 = jnp.bfloat16
_NT = (((1,), (1,)), ((), ()))


def _sigmoid(x):
    return 1.0 / (1.0 + jnp.exp(-x))


def _params(n_axes):
    return pltpu.CompilerParams(dimension_semantics=("arbitrary",) * n_axes,
                                vmem_limit_bytes=V7X_VMEM_LIMIT_BYTES)


def _mod_kernel(c_ref, w_ref, b_ref, o_ref):
    cc = c_ref[...]
    s = cc * _sigmoid(cc)
    o_ref[0] = jnp.dot(s, w_ref[0], preferred_element_type=_F32) + b_ref[0]


def _mod_call(rows, w_ada, b_ada):
    depth, d, d3 = w_ada.shape
    nb = d3 // d
    return pl.pallas_call(
        _mod_kernel,
        grid=(depth, nb),
        in_specs=[pl.BlockSpec((rows.shape[0], d), lambda l, j: (0, 0)),
                  pl.BlockSpec((1, d, d), lambda l, j: (l, 0, j)),
                  pl.BlockSpec((1, 1, d), lambda l, j: (l, 0, j))],
        out_specs=pl.BlockSpec((1, rows.shape[0], d), lambda l, j: (l, 0, j)),
        out_shape=jax.ShapeDtypeStruct((depth, rows.shape[0], d3), _F32),
        compiler_params=_params(2),
        name="adaln_mod",
    )(rows, w_ada, b_ada.reshape(depth, 1, d3))


def _inproj_kernel(x_ref, mod_ref, gn_ref, w_ref, gq_ref, gk_ref, cos_ref, sa_ref, sb_ref,
                   bmat_ref, lng_ref, lnb_ref, ws_ref, bsm_ref,
                   q2_ref, k_ref, vt_ref, f_ref, oc_ref, ga_ref, gb_ref, m_ref, *, tm, d):
    x = x_ref[0]
    mod = mod_ref[0]
    shift = mod[:, :d]
    scale = mod[:, d:2 * d]
    ms = jnp.mean(x * x, axis=-1, keepdims=True)
    h = (x * lax.rsqrt(ms + EPS) * gn_ref[...]) * (1.0 + scale) + shift
    hb = h.astype(_BF16)

    def proj(off, width):
        return jnp.dot(hb, w_ref[:, off:off + width], preferred_element_type=_F32)

    cos = cos_ref[...]
    sa = sa_ref[...]
    sb = sb_ref[...]
    bmat = bmat_ref[...]
    lane = lax.broadcasted_iota(jnp.int32, (tm, V7X_LANES), 1)
    lo = lane < A_QK_DIM

    def qk_head(p_h, g):
        msq = jnp.dot((p_h * p_h).astype(_BF16), bmat, preferred_element_type=_F32)
        t = p_h * lax.rsqrt(msq + EPS) * g
        return t * cos + pltpu.roll(t, V7X_LANES - AXIS_DIM // 2, 1) * sa + pltpu.roll(t, AXIS_DIM // 2, 1) * sb

    pq = proj(OFF_Q, A_WIDTH)
    gq = gq_ref[...]
    for hh in range(A_HEADS):
        t = qk_head(pq[:, hh * A_V_DIM:(hh + 1) * A_V_DIM], gq) * Q_SCALE
        q2_ref[0, hh, 0] = jnp.where(lo, t, 0.0).astype(_BF16)
        q2_ref[0, hh, 1] = jnp.where(lo, 0.0, t).astype(_BF16)

    pk = proj(OFF_K, A_WIDTH)
    gk = gk_ref[...]
    for hh in range(A_HEADS):
        k_ref[0, hh] = qk_head(pk[:, hh * A_V_DIM:(hh + 1) * A_V_DIM], gk).astype(_BF16)

    pv = proj(OFF_V, A_WIDTH)
    for hh in range(A_HEADS):
        vt_ref[0, hh, 0] = pv[:, hh * A_V_DIM:(hh + 1) * A_V_DIM].T.astype(_BF16)

    f_ref[0] = proj(OFF_F, B_WIDTH)

    u = proj(OFF_U, C_WIDTH)
    vc = proj(OFF_VC, C_WIDTH)
    mu = jnp.mean(vc, axis=-1, keepdims=True)
    dv = vc - mu
    var = jnp.mean(dv * dv, axis=-1, keepdims=True)
    vn = (dv * lax.rsqrt(var + EPS) * lng_ref[...] + lnb_ref[...]).astype(_BF16)

    pg = proj(OFF_GATE, GATE_WIDTH)
    gate = pg * _sigmoid(pg)
    ga_ref[0] = gate[:, :A_WIDTH].astype(_BF16)
    gb_ref[0] = gate[:, A_WIDTH:A_WIDTH + B_WIDTH].astype(_BF16)
    gate_c = gate[:, A_WIDTH + B_WIDTH:]

    lo_c = lax.broadcasted_iota(jnp.int32, (CHUNK, V7X_LANES), 1) < C_GROUP_DIM
    bsm = bsm_ref[...]
    for t in range(tm // CHUNK):
        rows = slice(t * CHUNK, (t + 1) * CHUNK)
        vl = vn[rows, :V7X_LANES]
        vr = vn[rows, V7X_LANES:]
        s_l = jnp.where(lo_c, jnp.dot(ws_ref[0], vl, preferred_element_type=_F32),
                        jnp.dot(ws_ref[1], vl, preferred_element_type=_F32))
        s_r = jnp.where(lo_c, jnp.dot(ws_ref[2], vr, preferred_element_type=_F32),
                        jnp.dot(ws_ref[3], vr, preferred_element_type=_F32))
        s = jnp.concatenate([s_l, s_r], axis=1) + bsm
        oc_ref[0, rows, :] = (u[rows] * s * gate_c[rows]).astype(_BF16)

    for j in range(3):
        pm = proj(OFF_MERGE + j * d, d)
        m_ref[0, :, j * d:(j + 1) * d] = _sigmoid(pm).astype(_BF16)


def _inproj_call(x, mod, gn, w_bf, gq, gk, cos, sa, sb, bmat, lng, lnb, ws_bf, bsm, *, tm, layer):
    b, n, d = x.shape
    nt = n // tm
    in_w = w_bf.shape[-1]
    full = lambda *shape: pl.BlockSpec(shape, lambda bi, i: (0,) * len(shape))
    of_layer = lambda *shape: pl.BlockSpec((None,) + shape, lambda bi, i: (layer,) + (0,) * len(shape))
    in_specs = [
        pl.BlockSpec((1, tm, d), lambda bi, i: (bi, i, 0)),
        pl.BlockSpec((1, 1, 3 * d), lambda bi, i: (bi, 0, 0)),
        full(1, d),
        of_layer(d, in_w),
        full(1, V7X_LANES),
        full(1, V7X_LANES),
        pl.BlockSpec((tm, V7X_LANES), lambda bi, i: (i, 0)),
        pl.BlockSpec((tm, V7X_LANES), lambda bi, i: (i, 0)),
        pl.BlockSpec((tm, V7X_LANES), lambda bi, i: (i, 0)),
        full(V7X_LANES, V7X_LANES),
        full(1, C_WIDTH),
        full(1, C_WIDTH),
        of_layer(C_GROUPS, CHUNK, CHUNK),
        full(CHUNK, C_WIDTH),
    ]
    out_shape = [
        jax.ShapeDtypeStruct((b, A_HEADS, 2, n, A_V_DIM), _BF16),
        jax.ShapeDtypeStruct((b, A_HEADS, n, A_V_DIM), _BF16),
        jax.ShapeDtypeStruct((b, A_HEADS, nt, A_V_DIM, tm), _BF16),
        jax.ShapeDtypeStruct((b, n, B_WIDTH), _F32),
        jax.ShapeDtypeStruct((b, n, C_WIDTH), _BF16),
        jax.ShapeDtypeStruct((b, n, A_WIDTH), _BF16),
        jax.ShapeDtypeStruct((b, n, B_WIDTH), _BF16),
        jax.ShapeDtypeStruct((b, n, 3 * d), _BF16),
    ]
    out_specs = [
        pl.BlockSpec((1, A_HEADS, 2, tm, A_V_DIM), lambda bi, i: (bi, 0, 0, i, 0)),
        pl.BlockSpec((1, A_HEADS, tm, A_V_DIM), lambda bi, i: (bi, 0, i, 0)),
        pl.BlockSpec((1, A_HEADS, 1, A_V_DIM, tm), lambda bi, i: (bi, 0, i, 0, 0)),
        pl.BlockSpec((1, tm, B_WIDTH), lambda bi, i: (bi, i, 0)),
        pl.BlockSpec((1, tm, C_WIDTH), lambda bi, i: (bi, i, 0)),
        pl.BlockSpec((1, tm, A_WIDTH), lambda bi, i: (bi, i, 0)),
        pl.BlockSpec((1, tm, B_WIDTH), lambda bi, i: (bi, i, 0)),
        pl.BlockSpec((1, tm, 3 * d), lambda bi, i: (bi, i, 0)),
    ]
    return pl.pallas_call(
        functools.partial(_inproj_kernel, tm=tm, d=d),
        grid=(b, nt),
        in_specs=in_specs,
        out_specs=out_specs,
        out_shape=out_shape,
        compiler_params=_params(2),
        name="inproj",
    )(x, mod, gn, w_bf, gq, gk, cos, sa, sb, bmat, lng, lnb, ws_bf, bsm)


def _attn_kernel(*refs, tq, n_src, tiles, lam_init, bounded):
    lam_ref, q2_ref, ga_ref, gsub_ref = refs[:4]
    src_refs = refs[4:4 + 2 * n_src]
    out_ref = refs[4 + 2 * n_src]
    p_sc, m_sc, l_sc, acc_sc = refs[5 + 2 * n_src:]

    q2 = q2_ref[0, 0].reshape(2 * tq, A_V_DIM)
    l_sc[...] = jnp.zeros(l_sc.shape, _F32)
    acc_sc[...] = jnp.zeros(acc_sc.shape, _F32)

    def tile_len(si):
        return src_refs[2 * si + 1].shape[-1]

    def k_tile(si, t):
        tk = tile_len(si)
        if isinstance(t, int):
            return src_refs[2 * si][0, 0, t * tk:(t + 1) * tk, :]
        return src_refs[2 * si][0, 0, pl.ds(pl.multiple_of(t * tk, tk), tk), :]

    def vt_tile(si, t):
        return src_refs[2 * si + 1][0, 0, t]

    if bounded:
        def exponentials(slot, k_t):
            p = jnp.exp2(lax.dot_general(k_t, q2, _NT, preferred_element_type=_F32))
            l_sc[...] += jnp.sum(p, axis=0, keepdims=True)
            p_sc[slot, :k_t.shape[0], :] = p.astype(_BF16)

        def weighted_values(slot, vt_t):
            acc_sc[...] += jnp.dot(vt_t, p_sc[slot, :vt_t.shape[1], :], preferred_element_type=_F32)

        flat = [(si, t) for si in range(n_src) for t in range(tiles[si])]
        exponentials(0, k_tile(*flat[0]))
        for j in range(len(flat)):
            if j + 1 < len(flat):
                exponentials((j + 1) % 2, k_tile(*flat[j + 1]))
            weighted_values(j % 2, vt_tile(*flat[j]))
    else:
        m_sc[...] = jnp.full(m_sc.shape, -jnp.inf, _F32)

        def step(k_t, vt_t):
            s = lax.dot_general(k_t, q2, _NT, preferred_element_type=_F32)
            m_prev = m_sc[...]
            m_new = jnp.maximum(m_prev, jnp.max(s, axis=0, keepdims=True))
            alpha = jnp.exp2(m_prev - m_new)
            p = jnp.exp2(s - m_new)
            l_sc[...] = alpha * l_sc[...] + jnp.sum(p, axis=0, keepdims=True)
            acc_sc[...] = alpha * acc_sc[...] + jnp.dot(vt_t, p.astype(_BF16), preferred_element_type=_F32)
            m_sc[...] = m_new

        for si in range(n_src):
            if tiles[si] == 1:
                step(k_tile(si, 0), vt_tile(si, 0))
            else:
                def body(t, carry, si=si):
                    step(k_tile(si, t), vt_tile(si, t))
                    return carry
                lax.fori_loop(0, tiles[si], body, 0)

    lv = lam_ref[...]
    lam = (jnp.exp(jnp.sum(lv[0:1] * lv[1:2], axis=-1, keepdims=True))
           - jnp.exp(jnp.sum(lv[2:3] * lv[3:4], axis=-1, keepdims=True)) + lam_init)
    inv = 1.0 / l_sc[...]
    acc = acc_sc[...]
    o = acc[:, :tq] * inv[:, :tq] - lam * (acc[:, tq:] * inv[:, tq:])
    ms = jnp.mean(o * o, axis=0, keepdims=True)
    on = o * lax.rsqrt(ms + EPS)
    out = on.T * (gsub_ref[...] * (1.0 - lam_init)) * ga_ref[0].astype(_F32)
    out_ref[0] = out.astype(_BF16)


def _attn_call(lamv, q2, ga, gsub, sources, *, tq, lam_init, bounded):
    b, _, _, n, _ = q2.shape
    in_specs = [
        pl.BlockSpec(lamv.shape, lambda bi, hi, i: (0, 0)),
        pl.BlockSpec((1, 1, 2, tq, A_V_DIM), lambda bi, hi, i: (bi, hi, 0, i, 0)),
        pl.BlockSpec((1, tq, A_V_DIM), lambda bi, hi, i: (bi, i, hi)),
        pl.BlockSpec((1, A_V_DIM), lambda bi, hi, i: (0, 0)),
    ]
    args = [lamv, q2, ga, gsub]
    tiles = []
    for k_s, vt_s in sources:
        ls = k_s.shape[2]
        nt, _, tk = vt_s.shape[2:]
        tiles.append(nt)
        in_specs.append(pl.BlockSpec((1, 1, ls, A_V_DIM), lambda bi, hi, i: (bi, hi, 0, 0)))
        in_specs.append(pl.BlockSpec((1, 1, nt, A_V_DIM, tk), lambda bi, hi, i: (bi, hi, 0, 0, 0)))
        args += [k_s, vt_s]
    tk_max = max(vt_s.shape[-1] for _, vt_s in sources)
    return pl.pallas_call(
        functools.partial(_attn_kernel, tq=tq, n_src=len(sources), tiles=tuple(tiles),
                          lam_init=lam_init, bounded=bounded),
        grid=(b, A_HEADS, n // tq),
        in_specs=in_specs,
        out_specs=pl.BlockSpec((1, tq, A_V_DIM), lambda bi, hi, i: (bi, i, hi)),
        out_shape=jax.ShapeDtypeStruct((b, n, A_WIDTH), _BF16),
        scratch_shapes=[pltpu.VMEM((2, tk_max, 2 * tq), _BF16),
                        pltpu.VMEM((1, 2 * tq), _F32),
                        pltpu.VMEM((1, 2 * tq), _F32),
                        pltpu.VMEM((A_V_DIM, 2 * tq), _F32)],
        compiler_params=_params(3),
        name="diff_attn" if bounded else "diff_attn_general",
    )(*args)


SAFE_EXP2_RANGE = 60.0


def _attention(lamv, q2, ga, gsub, sources, gq, gk, *, tq, lam_init):
    bound = A_QK_DIM * Q_SCALE * jnp.max(jnp.abs(gq)) * jnp.max(jnp.abs(gk))
    call = lambda bounded: functools.partial(_attn_call, tq=tq, lam_init=lam_init, bounded=bounded)
    return lax.cond(bound <= SAFE_EXP2_RANGE, call(True), call(False), lamv, q2, ga, gsub, sources)


def _dft1_kernel(f1_ref, x_ref, o_ref):
    o_ref[0] = jnp.dot(f1_ref[...], x_ref[0].astype(_BF16), preferred_element_type=_F32).astype(_BF16)


def _dft1_call(f1, x2d, *, tc):
    b, n1, w = x2d.shape
    return pl.pallas_call(
        _dft1_kernel,
        grid=(b, w // tc),
        in_specs=[pl.BlockSpec((2 * n1, n1), lambda bi, j: (0, 0)),
                  pl.BlockSpec((1, n1, tc), lambda bi, j: (bi, 0, j))],
        out_specs=pl.BlockSpec((1, 2 * n1, tc), lambda bi, j: (bi, 0, j)),
        out_shape=jax.ShapeDtypeStruct((b, 2 * n1, w), _BF16),
        compiler_params=_params(2),
        name="dft_stage1",
    )(f1, x2d)


def _dft2_kernel(*refs, g, n2, two_stage, norm):
    if two_stage:
        ar_ref, ai_ref, twc_ref, tws_ref, w2_ref, cbd_ref, sbd_ref, wf_ref, bf_ref, out_ref = refs
    else:
        ar_ref, w2_ref, cbd_ref, sbd_ref, wf_ref, bf_ref, out_ref = refs
    w2 = w2_ref[...]
    cbd = cbd_ref[...]
    sbd = sbd_ref[...]
    wf = wf_ref[...]
    bias = bf_ref[...]
    for j in range(g):
        ar = ar_ref[0, 0, j].astype(_F32)
        if two_stage:
            ai = ai_ref[0, 0, j].astype(_F32)
            tc = twc_ref[j]
            ts = tws_ref[j]
            c2 = jnp.concatenate([tc, tc], axis=1)
            s2 = jnp.concatenate([ts, ts], axis=1)
            a = jnp.concatenate([ar * c2 - ai * s2, ar * s2 + ai * c2], axis=0).astype(_BF16)
        else:
            a = ar.astype(_BF16)
        z = jnp.dot(w2, a, preferred_element_type=_F32)
        p = z[:n2].astype(_BF16)
        q = z[n2:].astype(_BF16)
        fr = (jnp.dot(p, cbd, preferred_element_type=_F32) - jnp.dot(q, sbd, preferred_element_type=_F32)) * norm
        y = jnp.dot(fr.astype(_BF16), wf, preferred_element_type=_F32) + bias
        out_ref[0, :, j * B_WIDTH:(j + 1) * B_WIDTH] = y.astype(_BF16)


def _dft_tables(n1, n2):
    n = n1 * n2
    k1 = np.arange(n1)
    a1 = 2.0 * np.pi * ((k1[:, None] * k1[None, :]) % n1) / n1
    f1 = np.concatenate([np.cos(a1), np.sin(a1)], axis=0)
    k2 = np.arange(n2)
    a2 = 2.0 * np.pi * ((k2[:, None] * k2[None, :]) % n2) / n2
    c2, s2 = np.cos(a2), np.sin(a2)
    w2 = np.block([[c2, -s2], [s2, c2]]) if n1 > 1 else np.concatenate([c2, s2], axis=0)
    gch = np.arange(B_GROUP_DIM)
    ag = 2.0 * np.pi * ((gch[:, None] * gch[None, :]) % B_GROUP_DIM) / B_GROUP_DIM
    eye = np.eye(B_GROUPS)
    cbd = np.kron(eye, np.cos(ag))
    sbd = np.kron(eye, np.sin(ag))
    to_bf16 = lambda t: jnp.asarray(t, _F32).astype(_BF16)
    return to_bf16(f1), to_bf16(w2), to_bf16(cbd), to_bf16(sbd), 1.0 / math.sqrt(n * B_GROUP_DIM)


def _twiddles(n1, n2):
    k1 = jnp.arange(n1, dtype=jnp.int32)[:, None]
    m2 = jnp.arange(n2, dtype=jnp.int32)[None, :]
    ang = (k1 * m2).astype(_F32) * (2.0 * math.pi / (n1 * n2))
    shape = (n1, n2, V7X_LANES)
    return (jnp.broadcast_to(jnp.cos(ang)[:, :, None], shape), jnp.broadcast_to(jnp.sin(ang)[:, :, None], shape))


def _fourier(f, wf_bd, bf_row):
    b, n, _ = f.shape
    two_stage = n > 2 * V7X_LANES
    n2 = V7X_LANES if two_stage else n
    n1 = n // n2
    g = min(n1, V7X_SUBLANES)
    f1, w2, cbd, sbd, norm = _dft_tables(n1, n2)
    full = lambda *shape: pl.BlockSpec(shape, lambda bi, j: (0,) * len(shape))
    out_spec = pl.BlockSpec((1, n2, g * B_WIDTH), lambda bi, j: (bi, 0, j))
    consts = [w2, cbd, sbd, wf_bd, bf_row]
    const_specs = [full(*w2.shape), full(B_WIDTH, B_WIDTH), full(B_WIDTH, B_WIDTH), full(B_WIDTH, B_WIDTH),
                   full(1, B_WIDTH)]
    if two_stage:
        tc = min(n2 * B_WIDTH, 4096)
        a = _dft1_call(f1, f.reshape(b, n1, n2 * B_WIDTH), tc=tc)
        a5 = a.reshape(b, 2, n1, n2, B_WIDTH)
        twc, tws = _twiddles(n1, n2)
        args = [a5, a5, twc, tws] + consts
        in_specs = [pl.BlockSpec((1, 1, g, n2, B_WIDTH), lambda bi, j: (bi, 0, j, 0, 0)),
                    pl.BlockSpec((1, 1, g, n2, B_WIDTH), lambda bi, j: (bi, 1, j, 0, 0)),
                    pl.BlockSpec((g, n2, V7X_LANES), lambda bi, j: (j, 0, 0)),
                    pl.BlockSpec((g, n2, V7X_LANES), lambda bi, j: (j, 0, 0))] + const_specs
    else:
        args = [f.reshape(b, 1, 1, n2, B_WIDTH)] + consts
        in_specs = [pl.BlockSpec((1, 1, 1, n2, B_WIDTH), lambda bi, j: (bi, 0, 0, 0, 0))] + const_specs
    out = pl.pallas_call(
        functools.partial(_dft2_kernel, g=g, n2=n2, two_stage=two_stage, norm=norm),
        grid=(b, n1 // g),
        in_specs=in_specs,
        out_specs=out_spec,
        out_shape=jax.ShapeDtypeStruct((b, n2, n1 * B_WIDTH), _BF16),
        compiler_params=_params(2),
        name="dft_stage2",
    )(*args)
    return out.reshape(b, n, B_WIDTH)


def _merge_kernel(oa_ref, ob_ref, gb_ref, oc_ref, m_ref, x_ref, mod_ref, wa_ref, wb_ref, wc_ref, wo_ref, o_ref, *, d):
    ya = jnp.dot(oa_ref[0], wa_ref[...], preferred_element_type=_F32)
    yb = jnp.dot(ob_ref[0] * gb_ref[0], wb_ref[...], preferred_element_type=_F32)
    yc = jnp.dot(oc_ref[0], wc_ref[...], preferred_element_type=_F32)
    m = m_ref[0]
    y = (m[:, :d].astype(_F32) * ya + m[:, d:2 * d].astype(_F32) * yb + m[:, 2 * d:].astype(_F32) * yc)
    out = jnp.dot(y.astype(_BF16), wo_ref[...], preferred_element_type=_F32)
    gate = mod_ref[0][:, 2 * d:]
    o_ref[0] = x_ref[0] + gate * out


def _merge_call(oa, ob, gb, oc, m, x, mod, wa, wb, wc, wo, *, tm, layer):
    b, n, d = x.shape
    row = lambda w: pl.BlockSpec((1, tm, w), lambda bi, i: (bi, i, 0))
    of_layer = lambda *shape: pl.BlockSpec((None,) + shape, lambda bi, i: (layer,) + (0,) * len(shape))
    return pl.pallas_call(
        functools.partial(_merge_kernel, d=d),
        grid=(b, n // tm),
        in_specs=[row(A_WIDTH), row(B_WIDTH), row(B_WIDTH), row(C_WIDTH), row(3 * d), row(d),
                  pl.BlockSpec((1, 1, 3 * d), lambda bi, i: (bi, 0, 0)),
                  of_layer(A_WIDTH, d), of_layer(B_WIDTH, d), of_layer(C_WIDTH, d), of_layer(d, d)],
        out_specs=row(d),
        out_shape=jax.ShapeDtypeStruct((b, n, d), _F32),
        compiler_params=_params(2),
        name="merge",
    )(oa, ob, gb, oc, m, x, mod, wa, wb, wc, wo)


def _rope_tables(n):
    pos = jnp.arange(n, dtype=jnp.int32)
    row = (pos // GRID_W).astype(_F32)
    col = (pos % GRID_W).astype(_F32)
    freqs = ROPE_BASE ** (-jnp.arange(0, AXIS_DIM, 2, dtype=_F32) / AXIS_DIM)
    ang_r = row[:, None] * freqs
    ang_c = col[:, None] * freqs
    ang = jnp.concatenate([ang_r, ang_r, ang_c, ang_c], axis=-1)
    cos, sin = jnp.cos(ang), jnp.sin(ang)
    first_half = (jnp.arange(A_QK_DIM) % AXIS_DIM) < AXIS_DIM // 2
    sa = jnp.where(first_half, -sin, 0.0)
    sb = jnp.where(first_half, 0.0, sin)
    tile2 = lambda t: jnp.concatenate([t, t], axis=-1)
    return tile2(cos), tile2(sa), tile2(sb)


def _no_rope_tables(n):
    return (jnp.ones((n, V7X_LANES), _F32), jnp.zeros((n, V7X_LANES), _F32), jnp.zeros((n, V7X_LANES), _F32))


def kernel(x, c, ctx, c_ctx, w_ada, b_ada, g_norm, w_in, g_q, g_k, lam_q1, lam_k1, lam_q2, lam_k2, g_sub,
           w_f, b_f, ln_g, ln_b, w_s, b_s, w_br_a, w_br_b, w_br_c, w_out):
    b, n, d = x.shape
    n_ctx = ctx.shape[1]
    depth = w_in.shape[0]
    tm_lat = min(n, 512)
    tm_ctx = min(n_ctx, 512)

    n_rows = V7X_SUBLANES * pl.cdiv(b + 1, V7X_SUBLANES)
    rows = jnp.concatenate([c, c_ctx[None, :], jnp.zeros((n_rows - b - 1, d), _F32)], axis=0)
    mod_all = _mod_call(rows, w_ada, b_ada)

    rope_lat = _rope_tables(n)
    rope_ctx = _no_rope_tables(n_ctx)
    group_mean = np.kron(np.eye(V7X_LANES // A_QK_DIM), np.full((A_QK_DIM, A_QK_DIM), 1.0 / A_QK_DIM))
    bmat = jnp.asarray(group_mean, _BF16)

    w_in_bf = w_in.astype(_BF16)
    w_s_bf = w_s.astype(_BF16)
    merge_w = (w_br_a.astype(_BF16), w_br_b.astype(_BF16), w_br_c.astype(_BF16), w_out.astype(_BF16))

    for l in range(depth):
        last = l == depth - 1
        lam_init = 0.8 - 0.6 * math.exp(-0.3 * l)
        lamv = jnp.stack([lam_q1[l], lam_k1[l], lam_q2[l], lam_k2[l]], axis=0)
        shared = dict(
            gn=g_norm[l][None, :], w_bf=w_in_bf,
            gq=jnp.tile(g_q[l], 2)[None, :], gk=jnp.tile(g_k[l], 2)[None, :], bmat=bmat,
            lng=ln_g[l][None, :], lnb=ln_b[l][None, :], ws_bf=w_s_bf,
            bsm=jnp.repeat(b_s[l].T, C_GROUP_DIM, axis=1), layer=l)
        gsub = g_sub[l][None, :]
        wf_bd = jax.scipy.linalg.block_diag(*[w_f[l, gi] for gi in range(B_GROUPS)]).astype(_BF16)
        bf_row = b_f[l].reshape(1, B_WIDTH)

        mod_lat = mod_all[l, :b][:, None, :]
        mod_ctx = jnp.broadcast_to(mod_all[l, b][None, None, :], (b, 1, 3 * d))

        cq2, ck, cvt, cf, coc, cga, cgb, cmg = _inproj_call(
            ctx, mod_ctx, cos=rope_ctx[0], sa=rope_ctx[1], sb=rope_ctx[2], tm=tm_ctx, **shared)
        q2, k, vt, f, oc, ga, gb, mg = _inproj_call(
            x, mod_lat, cos=rope_lat[0], sa=rope_lat[1], sb=rope_lat[2], tm=tm_lat, **shared)

        oa = _attention(lamv, q2, ga, gsub, [(ck, cvt), (k, vt)], g_q[l], g_k[l], tq=tm_lat, lam_init=lam_init)
        ob = _fourier(f, wf_bd, bf_row)
        x_new = _merge_call(oa, ob, gb, oc, mg, x, mod_lat, *merge_w, tm=tm_lat, layer=l)

        if not last:
            coa = _attention(lamv, cq2, cga, gsub, [(ck, cvt)], g_q[l], g_k[l], tq=tm_ctx, lam_init=lam_init)
            cob = _fourier(cf, wf_bd, bf_row)
            ctx = _merge_call(coa, cob, cgb, coc, cmg, ctx, mod_ctx, *merge_w, tm=tm_ctx, layer=l)
        x = x_new
    return x
```

```python
import functools
import math

import numpy as np
import jax
import jax.numpy as jnp
from jax import lax
from jax.experimental import pallas as pl
from jax.experimental.pallas import tpu as pltpu

A_HEADS = 4
A_QK_DIM = 64
A_V_DIM = 2 * A_QK_DIM
A_WIDTH = A_HEADS * A_V_DIM
B_GROUPS = 4
B_GROUP_DIM = 64
B_WIDTH = B_GROUPS * B_GROUP_DIM
C_GROUPS = 4
C_GROUP_DIM = 64
C_WIDTH = C_GROUPS * C_GROUP_DIM
CHUNK = 128
GRID_W = 64
ROPE_BASE = 10000.0
AXIS_DIM = A_QK_DIM // 2
EPS = 1e-6

OFF_Q = 0
OFF_K = OFF_Q + A_WIDTH
OFF_V = OFF_K + A_WIDTH
OFF_F = OFF_V + A_WIDTH
OFF_U = OFF_F + B_WIDTH
OFF_VC = OFF_U + C_WIDTH
OFF_GATE = OFF_VC + C_WIDTH
GATE_WIDTH = A_WIDTH + B_WIDTH + C_WIDTH
OFF_MERGE = OFF_GATE + GATE_WIDTH

V7X_LANES = 128
V7X_SUBLANES = 8
V7X_MXU_DIM = 256
V7X_VMEM_LIMIT_BYTES = 56 * 1024 * 1024

LOG2E = math.log2(math.e)
Q_SCALE = (A_QK_DIM ** -0.5) * LOG2E

_F32 = jnp.float32
_BF16 = jnp.bfloat16
_NT = (((1,), (1,)), ((), ()))


def _sigmoid(x):
    return 1.0 / (1.0 + jnp.exp(-x))


def _params(n_axes):
    return pltpu.CompilerParams(dimension_semantics=("arbitrary",) * n_axes,
                                vmem_limit_bytes=V7X_VMEM_LIMIT_BYTES)


def _mod_kernel(c_ref, w_ref, b_ref, o_ref):
    cc = c_ref[...]
    s = cc * _sigmoid(cc)
    o_ref[0] = jnp.dot(s, w_ref[0], preferred_element_type=_F32) + b_ref[0]


def _mod_call(rows, w_ada, b_ada):
    depth, d, d3 = w_ada.shape
    nb = d3 // d
    return pl.pallas_call(
        _mod_kernel,
        grid=(depth, nb),
        in_specs=[pl.BlockSpec((rows.shape[0], d), lambda l, j: (0, 0)),
                  pl.BlockSpec((1, d, d), lambda l, j: (l, 0, j)),
                  pl.BlockSpec((1, 1, d), lambda l, j: (l, 0, j))],
        out_specs=pl.BlockSpec((1, rows.shape[0], d), lambda l, j: (l, 0, j)),
        out_shape=jax.ShapeDtypeStruct((depth, rows.shape[0], d3), _F32),
        compiler_params=_params(2),
        name="adaln_mod",
    )(rows, w_ada, b_ada.reshape(depth, 1, d3))


def _modulated_norm(x, mod, gn, d):
    ms = jnp.mean(x * x, axis=-1, keepdims=True)
    h = (x * lax.rsqrt(ms + EPS) * gn) * (1.0 + mod[:, d:2 * d]) + mod[:, :d]
    return h.astype(_BF16)


def _inproj_kernel(x_ref, mod_ref, gn_ref, w_ref, gq_ref, gk_ref, cos_ref, sa_ref, sb_ref,
                   bmat_ref, lng_ref, lnb_ref, ws_ref, bsm_ref,
                   q2_ref, k_ref, vt_ref, f_ref, oc_ref, ga_ref, gb_ref, *, tm, d):
    hb = _modulated_norm(x_ref[0], mod_ref[0], gn_ref[...], d)

    def proj(off, width):
        return jnp.dot(hb, w_ref[:, off:off + width], preferred_element_type=_F32)

    cos = cos_ref[...]
    sa = sa_ref[...]
    sb = sb_ref[...]
    bmat = bmat_ref[...]
    lane = lax.broadcasted_iota(jnp.int32, (tm, V7X_LANES), 1)
    lo = lane < A_QK_DIM

    def qk_head(p_h, g):
        msq = jnp.dot((p_h * p_h).astype(_BF16), bmat, preferred_element_type=_F32)
        t = p_h * lax.rsqrt(msq + EPS) * g
        return t * cos + pltpu.roll(t, V7X_LANES - AXIS_DIM // 2, 1) * sa + pltpu.roll(t, AXIS_DIM // 2, 1) * sb

    pq = proj(OFF_Q, A_WIDTH)
    gq = gq_ref[...]
    for hh in range(A_HEADS):
        t = qk_head(pq[:, hh * A_V_DIM:(hh + 1) * A_V_DIM], gq) * Q_SCALE
        q2_ref[0, hh, 0] = jnp.where(lo, t, 0.0).astype(_BF16)
        q2_ref[0, hh, 1] = jnp.where(lo, 0.0, t).astype(_BF16)

    pk = proj(OFF_K, A_WIDTH)
    gk = gk_ref[...]
    for hh in range(A_HEADS):
        k_ref[0, hh] = qk_head(pk[:, hh * A_V_DIM:(hh + 1) * A_V_DIM], gk).astype(_BF16)

    pv = proj(OFF_V, A_WIDTH)
    for hh in range(A_HEADS):
        vt_ref[0, hh, 0] = pv[:, hh * A_V_DIM:(hh + 1) * A_V_DIM].T.astype(_BF16)

    f_ref[0] = proj(OFF_F, B_WIDTH)

    u = proj(OFF_U, C_WIDTH)
    vc = proj(OFF_VC, C_WIDTH)
    mu = jnp.mean(vc, axis=-1, keepdims=True)
    dv = vc - mu
    var = jnp.mean(dv * dv, axis=-1, keepdims=True)
    vn = (dv * lax.rsqrt(var + EPS) * lng_ref[...] + lnb_ref[...]).astype(_BF16)

    pg = proj(OFF_GATE, GATE_WIDTH)
    gate = pg * _sigmoid(pg)
    ga_ref[0] = gate[:, :A_WIDTH].astype(_BF16)
    gb_ref[0] = gate[:, A_WIDTH:A_WIDTH + B_WIDTH].astype(_BF16)
    gate_c = gate[:, A_WIDTH + B_WIDTH:]

    lo_c = lax.broadcasted_iota(jnp.int32, (CHUNK, V7X_LANES), 1) < C_GROUP_DIM
    bsm = bsm_ref[...]
    for t in range(tm // CHUNK):
        rows = slice(t * CHUNK, (t + 1) * CHUNK)
        vl = vn[rows, :V7X_LANES]
        vr = vn[rows, V7X_LANES:]
        s_l = jnp.where(lo_c, jnp.dot(ws_ref[0], vl, preferred_element_type=_F32),
                        jnp.dot(ws_ref[1], vl, preferred_element_type=_F32))
        s_r = jnp.where(lo_c, jnp.dot(ws_ref[2], vr, preferred_element_type=_F32),
                        jnp.dot(ws_ref[3], vr, preferred_element_type=_F32))
        s = jnp.concatenate([s_l, s_r], axis=1) + bsm
        oc_ref[0, rows, :] = (u[rows] * s * gate_c[rows]).astype(_BF16)


def _inproj_call(x, mod, gn, w_bf, gq, gk, cos, sa, sb, bmat, lng, lnb, ws_bf, bsm, *, tm, layer):
    b, n, d = x.shape
    nt = n // tm
    in_w = w_bf.shape[-1]
    full = lambda *shape: pl.BlockSpec(shape, lambda bi, i: (0,) * len(shape))
    of_layer = lambda *shape: pl.BlockSpec((None,) + shape, lambda bi, i: (layer,) + (0,) * len(shape))
    in_specs = [
        pl.BlockSpec((1, tm, d), lambda bi, i: (bi, i, 0)),
        pl.BlockSpec((1, 1, 3 * d), lambda bi, i: (bi, 0, 0)),
        full(1, d),
        of_layer(d, in_w),
        full(1, V7X_LANES),
        full(1, V7X_LANES),
        pl.BlockSpec((tm, V7X_LANES), lambda bi, i: (i, 0)),
        pl.BlockSpec((tm, V7X_LANES), lambda bi, i: (i, 0)),
        pl.BlockSpec((tm, V7X_LANES), lambda bi, i: (i, 0)),
        full(V7X_LANES, V7X_LANES),
        full(1, C_WIDTH),
        full(1, C_WIDTH),
        of_layer(C_GROUPS, CHUNK, CHUNK),
        full(CHUNK, C_WIDTH),
    ]
    out_shape = [
        jax.ShapeDtypeStruct((b, A_HEADS, 2, n, A_V_DIM), _BF16),
        jax.ShapeDtypeStruct((b, A_HEADS, n, A_V_DIM), _BF16),
        jax.ShapeDtypeStruct((b, A_HEADS, nt, A_V_DIM, tm), _BF16),
        jax.ShapeDtypeStruct((b, n, B_WIDTH), _F32),
        jax.ShapeDtypeStruct((b, n, C_WIDTH), _BF16),
        jax.ShapeDtypeStruct((b, n, A_WIDTH), _BF16),
        jax.ShapeDtypeStruct((b, n, B_WIDTH), _BF16),
    ]
    out_specs = [
        pl.BlockSpec((1, A_HEADS, 2, tm, A_V_DIM), lambda bi, i: (bi, 0, 0, i, 0)),
        pl.BlockSpec((1, A_HEADS, tm, A_V_DIM), lambda bi, i: (bi, 0, i, 0)),
        pl.BlockSpec((1, A_HEADS, 1, A_V_DIM, tm), lambda bi, i: (bi, 0, i, 0, 0)),
        pl.BlockSpec((1, tm, B_WIDTH), lambda bi, i: (bi, i, 0)),
        pl.BlockSpec((1, tm, C_WIDTH), lambda bi, i: (bi, i, 0)),
        pl.BlockSpec((1, tm, A_WIDTH), lambda bi, i: (bi, i, 0)),
        pl.BlockSpec((1, tm, B_WIDTH), lambda bi, i: (bi, i, 0)),
    ]
    return pl.pallas_call(
        functools.partial(_inproj_kernel, tm=tm, d=d),
        grid=(b, nt),
        in_specs=in_specs,
        out_specs=out_specs,
        out_shape=out_shape,
        compiler_params=_params(2),
        name="inproj",
    )(x, mod, gn, w_bf, gq, gk, cos, sa, sb, bmat, lng, lnb, ws_bf, bsm)


def _attn_kernel(*refs, tq, n_src, tiles, lam_init, bounded):
    lam_ref, q2_ref, ga_ref, gsub_ref = refs[:4]
    src_refs = refs[4:4 + 2 * n_src]
    out_ref = refs[4 + 2 * n_src]
    p_sc, m_sc, l_sc, acc_sc = refs[5 + 2 * n_src:]

    q2 = q2_ref[0, 0].reshape(2 * tq, A_V_DIM)
    l_sc[...] = jnp.zeros(l_sc.shape, _F32)
    acc_sc[...] = jnp.zeros(acc_sc.shape, _F32)

    def tile_len(si):
        return src_refs[2 * si + 1].shape[-1]

    def k_tile(si, t):
        tk = tile_len(si)
        if isinstance(t, int):
            return src_refs[2 * si][0, 0, t * tk:(t + 1) * tk, :]
        return src_refs[2 * si][0, 0, pl.ds(pl.multiple_of(t * tk, tk), tk), :]

    def vt_tile(si, t):
        return src_refs[2 * si + 1][0, 0, t]

    if bounded:
        def exponentials(slot, k_t):
            p = jnp.exp2(lax.dot_general(k_t, q2, _NT, preferred_element_type=_F32))
            l_sc[...] += jnp.sum(p, axis=0, keepdims=True)
            p_sc[slot, :k_t.shape[0], :] = p.astype(_BF16)

        def weighted_values(slot, vt_t):
            acc_sc[...] += jnp.dot(vt_t, p_sc[slot, :vt_t.shape[1], :], preferred_element_type=_F32)

        flat = [(si, t) for si in range(n_src) for t in range(tiles[si])]
        exponentials(0, k_tile(*flat[0]))
        for j in range(len(flat)):
            if j + 1 < len(flat):
                exponentials((j + 1) % 2, k_tile(*flat[j + 1]))
            weighted_values(j % 2, vt_tile(*flat[j]))
    else:
        m_sc[...] = jnp.full(m_sc.shape, -jnp.inf, _F32)

        def step(k_t, vt_t):
            s = lax.dot_general(k_t, q2, _NT, preferred_element_type=_F32)
            m_prev = m_sc[...]
            m_new = jnp.maximum(m_prev, jnp.max(s, axis=0, keepdims=True))
            alpha = jnp.exp2(m_prev - m_new)
            p = jnp.exp2(s - m_new)
            l_sc[...] = alpha * l_sc[...] + jnp.sum(p, axis=0, keepdims=True)
            acc_sc[...] = alpha * acc_sc[...] + jnp.dot(vt_t, p.astype(_BF16), preferred_element_type=_F32)
            m_sc[...] = m_new

        for si in range(n_src):
            if tiles[si] == 1:
                step(k_tile(si, 0), vt_tile(si, 0))
            else:
                def body(t, carry, si=si):
                    step(k_tile(si, t), vt_tile(si, t))
                    return carry
                lax.fori_loop(0, tiles[si], body, 0)

    lv = lam_ref[...]
    lam = (jnp.exp(jnp.sum(lv[0:1] * lv[1:2], axis=-1, keepdims=True))
           - jnp.exp(jnp.sum(lv[2:3] * lv[3:4], axis=-1, keepdims=True)) + lam_init)
    inv = 1.0 / l_sc[...]
    acc = acc_sc[...]
    o = acc[:, :tq] * inv[:, :tq] - lam * (acc[:, tq:] * inv[:, tq:])
    ms = jnp.mean(o * o, axis=0, keepdims=True)
    on = o * lax.rsqrt(ms + EPS)
    out = on.T * (gsub_ref[...] * (1.0 - lam_init)) * ga_ref[0].astype(_F32)
    out_ref[0] = out.astype(_BF16)


def _attn_call(lamv, q2, ga, gsub, sources, *, tq, lam_init, bounded):
    b, _, _, n, _ = q2.shape
    in_specs = [
        pl.BlockSpec(lamv.shape, lambda bi, hi, i: (0, 0)),
        pl.BlockSpec((1, 1, 2, tq, A_V_DIM), lambda bi, hi, i: (bi, hi, 0, i, 0)),
        pl.BlockSpec((1, tq, A_V_DIM), lambda bi, hi, i: (bi, i, hi)),
        pl.BlockSpec((1, A_V_DIM), lambda bi, hi, i: (0, 0)),
    ]
    args = [lamv, q2, ga, gsub]
    tiles = []
    for k_s, vt_s in sources:
        ls = k_s.shape[2]
        nt, _, tk = vt_s.shape[2:]
        tiles.append(nt)
        in_specs.append(pl.BlockSpec((1, 1, ls, A_V_DIM), lambda bi, hi, i: (bi, hi, 0, 0)))
        in_specs.append(pl.BlockSpec((1, 1, nt, A_V_DIM, tk), lambda bi, hi, i: (bi, hi, 0, 0, 0)))
        args += [k_s, vt_s]
    tk_max = max(vt_s.shape[-1] for _, vt_s in sources)
    return pl.pallas_call(
        functools.partial(_attn_kernel, tq=tq, n_src=len(sources), tiles=tuple(tiles),
                          lam_init=lam_init, bounded=bounded),
        grid=(b, A_HEADS, n // tq),
        in_specs=in_specs,
        out_specs=pl.BlockSpec((1, tq, A_V_DIM), lambda bi, hi, i: (bi, i, hi)),
        out_shape=jax.ShapeDtypeStruct((b, n, A_WIDTH), _BF16),
        scratch_shapes=[pltpu.VMEM((2, tk_max, 2 * tq), _BF16),
                        pltpu.VMEM((1, 2 * tq), _F32),
                        pltpu.VMEM((1, 2 * tq), _F32),
                        pltpu.VMEM((A_V_DIM, 2 * tq), _F32)],
        compiler_params=_params(3),
        name="diff_attn" if bounded else "diff_attn_general",
    )(*args)


SAFE_EXP2_RANGE = 60.0


def _attention(lamv, q2, ga, gsub, sources, gq, gk, *, tq, lam_init):
    bound = A_QK_DIM * Q_SCALE * jnp.max(jnp.abs(gq)) * jnp.max(jnp.abs(gk))
    call = lambda bounded: functools.partial(_attn_call, tq=tq, lam_init=lam_init, bounded=bounded)
    return lax.cond(bound <= SAFE_EXP2_RANGE, call(True), call(False), lamv, q2, ga, gsub, sources)


def _dft1_kernel(f1_ref, x_ref, o_ref):
    o_ref[0] = jnp.dot(f1_ref[...], x_ref[0].astype(_BF16), preferred_element_type=_F32).astype(_BF16)


def _dft1_call(f1, x2d, *, tc):
    b, n1, w = x2d.shape
    return pl.pallas_call(
        _dft1_kernel,
        grid=(b, w // tc),
        in_specs=[pl.BlockSpec((2 * n1, n1), lambda bi, j: (0, 0)),
                  pl.BlockSpec((1, n1, tc), lambda bi, j: (bi, 0, j))],
        out_specs=pl.BlockSpec((1, 2 * n1, tc), lambda bi, j: (bi, 0, j)),
        out_shape=jax.ShapeDtypeStruct((b, 2 * n1, w), _BF16),
        compiler_params=_params(2),
        name="dft_stage1",
    )(f1, x2d)


def _dft2_kernel(*refs, g, n2, two_stage, norm):
    if two_stage:
        ar_ref, ai_ref, twc_ref, tws_ref, w2_ref, cbd_ref, sbd_ref, wf_ref, bf_ref, out_ref = refs
    else:
        ar_ref, w2_ref, cbd_ref, sbd_ref, wf_ref, bf_ref, out_ref = refs
    w2 = w2_ref[...]
    cbd = cbd_ref[...]
    sbd = sbd_ref[...]
    wf = wf_ref[...]
    bias = bf_ref[...]
    for j in range(g):
        ar = ar_ref[0, 0, j].astype(_F32)
        if two_stage:
            ai = ai_ref[0, 0, j].astype(_F32)
            tc = twc_ref[j]
            ts = tws_ref[j]
            c2 = jnp.concatenate([tc, tc], axis=1)
            s2 = jnp.concatenate([ts, ts], axis=1)
            a = jnp.concatenate([ar * c2 - ai * s2, ar * s2 + ai * c2], axis=0).astype(_BF16)
        else:
            a = ar.astype(_BF16)
        z = jnp.dot(w2, a, preferred_element_type=_F32)
        p = z[:n2].astype(_BF16)
        q = z[n2:].astype(_BF16)
        fr = (jnp.dot(p, cbd, preferred_element_type=_F32) - jnp.dot(q, sbd, preferred_element_type=_F32)) * norm
        y = jnp.dot(fr.astype(_BF16), wf, preferred_element_type=_F32) + bias
        out_ref[0, :, j * B_WIDTH:(j + 1) * B_WIDTH] = y.astype(_BF16)


def _dft_tables(n1, n2):
    n = n1 * n2
    k1 = np.arange(n1)
    a1 = 2.0 * np.pi * ((k1[:, None] * k1[None, :]) % n1) / n1
    f1 = np.concatenate([np.cos(a1), np.sin(a1)], axis=0)
    k2 = np.arange(n2)
    a2 = 2.0 * np.pi * ((k2[:, None] * k2[None, :]) % n2) / n2
    c2, s2 = np.cos(a2), np.sin(a2)
    w2 = np.block([[c2, -s2], [s2, c2]]) if n1 > 1 else np.concatenate([c2, s2], axis=0)
    gch = np.arange(B_GROUP_DIM)
    ag = 2.0 * np.pi * ((gch[:, None] * gch[None, :]) % B_GROUP_DIM) / B_GROUP_DIM
    eye = np.eye(B_GROUPS)
    cbd = np.kron(eye, np.cos(ag))
    sbd = np.kron(eye, np.sin(ag))
    to_bf16 = lambda t: jnp.asarray(t, _F32).astype(_BF16)
    return to_bf16(f1), to_bf16(w2), to_bf16(cbd), to_bf16(sbd), 1.0 / math.sqrt(n * B_GROUP_DIM)


def _twiddles(n1, n2):
    k1 = jnp.arange(n1, dtype=jnp.int32)[:, None]
    m2 = jnp.arange(n2, dtype=jnp.int32)[None, :]
    ang = (k1 * m2).astype(_F32) * (2.0 * math.pi / (n1 * n2))
    shape = (n1, n2, V7X_LANES)
    return (jnp.broadcast_to(jnp.cos(ang)[:, :, None], shape), jnp.broadcast_to(jnp.sin(ang)[:, :, None], shape))


def _fourier(f, wf_bd, bf_row):
    b, n, _ = f.shape
    two_stage = n > 2 * V7X_LANES
    n2 = V7X_LANES if two_stage else n
    n1 = n // n2
    g = min(n1, V7X_SUBLANES)
    f1, w2, cbd, sbd, norm = _dft_tables(n1, n2)
    full = lambda *shape: pl.BlockSpec(shape, lambda bi, j: (0,) * len(shape))
    out_spec = pl.BlockSpec((1, n2, g * B_WIDTH), lambda bi, j: (bi, 0, j))
    consts = [w2, cbd, sbd, wf_bd, bf_row]
    const_specs = [full(*w2.shape), full(B_WIDTH, B_WIDTH), full(B_WIDTH, B_WIDTH), full(B_WIDTH, B_WIDTH),
                   full(1, B_WIDTH)]
    if two_stage:
        tc = min(n2 * B_WIDTH, 4096)
        a = _dft1_call(f1, f.reshape(b, n1, n2 * B_WIDTH), tc=tc)
        a5 = a.reshape(b, 2, n1, n2, B_WIDTH)
        twc, tws = _twiddles(n1, n2)
        args = [a5, a5, twc, tws] + consts
        in_specs = [pl.BlockSpec((1, 1, g, n2, B_WIDTH), lambda bi, j: (bi, 0, j, 0, 0)),
                    pl.BlockSpec((1, 1, g, n2, B_WIDTH), lambda bi, j: (bi, 1, j, 0, 0)),
                    pl.BlockSpec((g, n2, V7X_LANES), lambda bi, j: (j, 0, 0)),
                    pl.BlockSpec((g, n2, V7X_LANES), lambda bi, j: (j, 0, 0))] + const_specs
    else:
        args = [f.reshape(b, 1, 1, n2, B_WIDTH)] + consts
        in_specs = [pl.BlockSpec((1, 1, 1, n2, B_WIDTH), lambda bi, j: (bi, 0, 0, 0, 0))] + const_specs
    out = pl.pallas_call(
        functools.partial(_dft2_kernel, g=g, n2=n2, two_stage=two_stage, norm=norm),
        grid=(b, n1 // g),
        in_specs=in_specs,
        out_specs=out_spec,
        out_shape=jax.ShapeDtypeStruct((b, n2, n1 * B_WIDTH), _BF16),
        compiler_params=_params(2),
        name="dft_stage2",
    )(*args)
    return out.reshape(b, n, B_WIDTH)


def _merge_kernel(oa_ref, ob_ref, gb_ref, oc_ref, x_ref, mod_ref, gn_ref, wm_ref, wa_ref, wb_ref, wc_ref, wo_ref,
                  o_ref, *, d):
    x = x_ref[0]
    mod = mod_ref[0]
    hb = _modulated_norm(x, mod, gn_ref[...], d)
    branches = (jnp.dot(oa_ref[0], wa_ref[...], preferred_element_type=_F32),
                jnp.dot(ob_ref[0] * gb_ref[0], wb_ref[...], preferred_element_type=_F32),
                jnp.dot(oc_ref[0], wc_ref[...], preferred_element_type=_F32))
    y = None
    for j, yj in enumerate(branches):
        mj = _sigmoid(jnp.dot(hb, wm_ref[:, j * d:(j + 1) * d], preferred_element_type=_F32))
        y = mj * yj if y is None else y + mj * yj
    out = jnp.dot(y.astype(_BF16), wo_ref[...], preferred_element_type=_F32)
    o_ref[0] = x + mod[:, 2 * d:] * out


def _merge_call(oa, ob, gb, oc, x, mod, gn, wm, wa, wb, wc, wo, *, tm, layer):
    b, n, d = x.shape
    row = lambda w: pl.BlockSpec((1, tm, w), lambda bi, i: (bi, i, 0))
    of_layer = lambda *shape: pl.BlockSpec((None,) + shape, lambda bi, i: (layer,) + (0,) * len(shape))
    return pl.pallas_call(
        functools.partial(_merge_kernel, d=d),
        grid=(b, n // tm),
        in_specs=[row(A_WIDTH), row(B_WIDTH), row(B_WIDTH), row(C_WIDTH), row(d),
                  pl.BlockSpec((1, 1, 3 * d), lambda bi, i: (bi, 0, 0)),
                  pl.BlockSpec((1, d), lambda bi, i: (0, 0)),
                  of_layer(d, 3 * d), of_layer(A_WIDTH, d), of_layer(B_WIDTH, d), of_layer(C_WIDTH, d),
                  of_layer(d, d)],
        out_specs=row(d),
        out_shape=jax.ShapeDtypeStruct((b, n, d), _F32),
        compiler_params=_params(2),
        name="merge",
    )(oa, ob, gb, oc, x, mod, gn, wm, wa, wb, wc, wo)


def _rope_tables(n):
    pos = jnp.arange(n, dtype=jnp.int32)
    row = (pos // GRID_W).astype(_F32)
    col = (pos % GRID_W).astype(_F32)
    freqs = ROPE_BASE ** (-jnp.arange(0, AXIS_DIM, 2, dtype=_F32) / AXIS_DIM)
    ang_r = row[:, None] * freqs
    ang_c = col[:, None] * freqs
    ang = jnp.concatenate([ang_r, ang_r, ang_c, ang_c], axis=-1)
    cos, sin = jnp.cos(ang), jnp.sin(ang)
    first_half = (jnp.arange(A_QK_DIM) % AXIS_DIM) < AXIS_DIM // 2
    sa = jnp.where(first_half, -sin, 0.0)
    sb = jnp.where(first_half, 0.0, sin)
    tile2 = lambda t: jnp.concatenate([t, t], axis=-1)
    return tile2(cos), tile2(sa), tile2(sb)


def _no_rope_tables(n):
    return (jnp.ones((n, V7X_LANES), _F32), jnp.zeros((n, V7X_LANES), _F32), jnp.zeros((n, V7X_LANES), _F32))


def kernel(x, c, ctx, c_ctx, w_ada, b_ada, g_norm, w_in, g_q, g_k, lam_q1, lam_k1, lam_q2, lam_k2, g_sub,
           w_f, b_f, ln_g, ln_b, w_s, b_s, w_br_a, w_br_b, w_br_c, w_out):
    b, n, d = x.shape
    n_ctx = ctx.shape[1]
    depth = w_in.shape[0]
    tm_lat = min(n, 512)
    tm_ctx = min(n_ctx, 512)

    n_rows = V7X_SUBLANES * pl.cdiv(b + 1, V7X_SUBLANES)
    rows = jnp.concatenate([c, c_ctx[None, :], jnp.zeros((n_rows - b - 1, d), _F32)], axis=0)
    mod_all = _mod_call(rows, w_ada, b_ada)

    rope_lat = _rope_tables(n)
    rope_ctx = _no_rope_tables(n_ctx)
    group_mean = np.kron(np.eye(V7X_LANES // A_QK_DIM), np.full((A_QK_DIM, A_QK_DIM), 1.0 / A_QK_DIM))
    bmat = jnp.asarray(group_mean, _BF16)

    w_in_bf = w_in[:, :, :OFF_MERGE].astype(_BF16)
    w_s_bf = w_s.astype(_BF16)
    merge_w = (w_in[:, :, OFF_MERGE:].astype(_BF16), w_br_a.astype(_BF16), w_br_b.astype(_BF16),
               w_br_c.astype(_BF16), w_out.astype(_BF16))

    for l in range(depth):
        last = l == depth - 1
        lam_init = 0.8 - 0.6 * math.exp(-0.3 * l)
        lamv = jnp.stack([lam_q1[l], lam_k1[l], lam_q2[l], lam_k2[l]], axis=0)
        shared = dict(
            gn=g_norm[l][None, :], w_bf=w_in_bf,
            gq=jnp.tile(g_q[l], 2)[None, :], gk=jnp.tile(g_k[l], 2)[None, :], bmat=bmat,
            lng=ln_g[l][None, :], lnb=ln_b[l][None, :], ws_bf=w_s_bf,
            bsm=jnp.repeat(b_s[l].T, C_GROUP_DIM, axis=1), layer=l)
        gsub = g_sub[l][None, :]
        wf_bd = jax.scipy.linalg.block_diag(*[w_f[l, gi] for gi in range(B_GROUPS)]).astype(_BF16)
        bf_row = b_f[l].reshape(1, B_WIDTH)

        mod_lat = mod_all[l, :b][:, None, :]
        mod_ctx = jnp.broadcast_to(mod_all[l, b][None, None, :], (b, 1, 3 * d))

        cq2, ck, cvt, cf, coc, cga, cgb = _inproj_call(
            ctx, mod_ctx, cos=rope_ctx[0], sa=rope_ctx[1], sb=rope_ctx[2], tm=tm_ctx, **shared)
        q2, k, vt, f, oc, ga, gb = _inproj_call(
            x, mod_lat, cos=rope_lat[0], sa=rope_lat[1], sb=rope_lat[2], tm=tm_lat, **shared)

        oa = _attention(lamv, q2, ga, gsub, [(ck, cvt), (k, vt)], g_q[l], g_k[l], tq=tm_lat, lam_init=lam_init)
        ob = _fourier(f, wf_bd, bf_row)
        x_new = _merge_call(oa, ob, gb, oc, x, mod_lat, shared["gn"], *merge_w, tm=tm_lat, layer=l)

        if not last:
            coa = _attention(lamv, cq2, cga, gsub, [(ck, cvt)], g_q[l], g_k[l], tq=tm_ctx, lam_init=lam_init)
            cob = _fourier(cf, wf_bd, bf_row)
            ctx = _merge_call(coa, cob, cgb, coc, ctx, mod_ctx, shared["gn"], *merge_w, tm=tm_ctx, layer=l)
        x = x_new
    return x
```

```python
import functools
import math

import numpy as np
import jax
import jax.numpy as jnp
from jax import lax
from jax.experimental import pallas as pl
from jax.experimental.pallas import tpu as pltpu

A_HEADS = 4
A_QK_DIM = 64
A_V_DIM = 2 * A_QK_DIM
A_WIDTH = A_HEADS * A_V_DIM
B_GROUPS = 4
B_GROUP_DIM = 64
B_WIDTH = B_GROUPS * B_GROUP_DIM
C_GROUPS = 4
C_GROUP_DIM = 64
C_WIDTH = C_GROUPS * C_GROUP_DIM
CHUNK = 128
GRID_W = 64
ROPE_BASE = 10000.0
AXIS_DIM = A_QK_DIM // 2
EPS = 1e-6

OFF_Q = 0
OFF_K = OFF_Q + A_WIDTH
OFF_V = OFF_K + A_WIDTH
OFF_F = OFF_V + A_WIDTH
OFF_U = OFF_F + B_WIDTH
OFF_VC = OFF_U + C_WIDTH
OFF_GATE = OFF_VC + C_WIDTH
GATE_WIDTH = A_WIDTH + B_WIDTH + C_WIDTH
OFF_MERGE = OFF_GATE + GATE_WIDTH

V7X_LANES = 128
V7X_SUBLANES = 8
V7X_MXU_DIM = 256
V7X_VMEM_LIMIT_BYTES = 56 * 1024 * 1024

LOG2E = math.log2(math.e)
Q_SCALE = (A_QK_DIM ** -0.5) * LOG2E

_F32 = jnp.float32
_BF16 = jnp.bfloat16
_NT = (((1,), (1,)), ((), ()))


def _sigmoid(x):
    return 1.0 / (1.0 + jnp.exp(-x))


def _params(n_axes):
    return pltpu.CompilerParams(dimension_semantics=("arbitrary",) * n_axes,
                                vmem_limit_bytes=V7X_VMEM_LIMIT_BYTES)


def _mod_kernel(c_ref, w_ref, b_ref, o_ref):
    cc = c_ref[...]
    s = cc * _sigmoid(cc)
    o_ref[0] = jnp.dot(s, w_ref[0], preferred_element_type=_F32) + b_ref[0]


def _mod_call(rows, w_ada, b_ada):
    depth, d, d3 = w_ada.shape
    nb = d3 // d
    return pl.pallas_call(
        _mod_kernel,
        grid=(depth, nb),
        in_specs=[pl.BlockSpec((rows.shape[0], d), lambda l, j: (0, 0)),
                  pl.BlockSpec((1, d, d), lambda l, j: (l, 0, j)),
                  pl.BlockSpec((1, 1, d), lambda l, j: (l, 0, j))],
        out_specs=pl.BlockSpec((1, rows.shape[0], d), lambda l, j: (l, 0, j)),
        out_shape=jax.ShapeDtypeStruct((depth, rows.shape[0], d3), _F32),
        compiler_params=_params(2),
        name="adaln_mod",
    )(rows, w_ada, b_ada.reshape(depth, 1, d3))


def _modulated_norm(x, mod, gn, d):
    ms = jnp.mean(x * x, axis=-1, keepdims=True)
    h = (x * lax.rsqrt(ms + EPS) * gn) * (1.0 + mod[:, d:2 * d]) + mod[:, :d]
    return h.astype(_BF16)


def _inproj_kernel(x_ref, mod_ref, gn_ref, w_ref, gq_ref, gk_ref, cos_ref, sa_ref, sb_ref,
                   bmat_ref, lng_ref, lnb_ref, ws_ref, bsm_ref,
                   q2_ref, k_ref, vt_ref, f_ref, oc_ref, ga_ref, gb_ref, *, sub, d):
    bmat = bmat_ref[...]
    lo = lax.broadcasted_iota(jnp.int32, (sub, V7X_LANES), 1) < A_QK_DIM
    lo_c = lax.broadcasted_iota(jnp.int32, (CHUNK, V7X_LANES), 1) < C_GROUP_DIM
    bsm = bsm_ref[...]
    gq = gq_ref[...]
    gk = gk_ref[...]

    for si in range(x_ref.shape[1] // sub):
        rows = slice(si * sub, (si + 1) * sub)
        hb = _modulated_norm(x_ref[0, rows, :], mod_ref[0], gn_ref[...], d)

        def proj(off, width, hb=hb):
            return jnp.dot(hb, w_ref[:, off:off + width], preferred_element_type=_F32)

        cos = cos_ref[rows, :]
        sa = sa_ref[rows, :]
        sb = sb_ref[rows, :]

        def qk_head(p_h, g, cos=cos, sa=sa, sb=sb):
            msq = jnp.dot((p_h * p_h).astype(_BF16), bmat, preferred_element_type=_F32)
            t = p_h * lax.rsqrt(msq + EPS) * g
            return (t * cos + pltpu.roll(t, V7X_LANES - AXIS_DIM // 2, 1) * sa
                    + pltpu.roll(t, AXIS_DIM // 2, 1) * sb)

        pq = proj(OFF_Q, A_WIDTH)
        for hh in range(A_HEADS):
            t = qk_head(pq[:, hh * A_V_DIM:(hh + 1) * A_V_DIM], gq) * Q_SCALE
            q2_ref[0, hh, 0, rows, :] = jnp.where(lo, t, 0.0).astype(_BF16)
            q2_ref[0, hh, 1, rows, :] = jnp.where(lo, 0.0, t).astype(_BF16)

        pk = proj(OFF_K, A_WIDTH)
        for hh in range(A_HEADS):
            k_ref[0, hh, rows, :] = qk_head(pk[:, hh * A_V_DIM:(hh + 1) * A_V_DIM], gk).astype(_BF16)

        pv = proj(OFF_V, A_WIDTH)
        for hh in range(A_HEADS):
            vt_ref[0, hh, si] = pv[:, hh * A_V_DIM:(hh + 1) * A_V_DIM].T.astype(_BF16)

        f_ref[0, rows, :] = proj(OFF_F, B_WIDTH)

        u = proj(OFF_U, C_WIDTH)
        vc = proj(OFF_VC, C_WIDTH)
        mu = jnp.mean(vc, axis=-1, keepdims=True)
        dv = vc - mu
        var = jnp.mean(dv * dv, axis=-1, keepdims=True)
        vn = (dv * lax.rsqrt(var + EPS) * lng_ref[...] + lnb_ref[...]).astype(_BF16)

        pg = proj(OFF_GATE, GATE_WIDTH)
        gate = pg * _sigmoid(pg)
        ga_ref[0, rows, :] = gate[:, :A_WIDTH].astype(_BF16)
        gb_ref[0, rows, :] = gate[:, A_WIDTH:A_WIDTH + B_WIDTH].astype(_BF16)
        gate_c = gate[:, A_WIDTH + B_WIDTH:]

        for t in range(sub // CHUNK):
            crows = slice(t * CHUNK, (t + 1) * CHUNK)
            vl = vn[crows, :V7X_LANES]
            vr = vn[crows, V7X_LANES:]
            s_l = jnp.where(lo_c, jnp.dot(ws_ref[0], vl, preferred_element_type=_F32),
                            jnp.dot(ws_ref[1], vl, preferred_element_type=_F32))
            s_r = jnp.where(lo_c, jnp.dot(ws_ref[2], vr, preferred_element_type=_F32),
                            jnp.dot(ws_ref[3], vr, preferred_element_type=_F32))
            s = jnp.concatenate([s_l, s_r], axis=1) + bsm
            oc_ref[0, si * sub + t * CHUNK:si * sub + (t + 1) * CHUNK, :] = (
                u[crows] * s * gate_c[crows]).astype(_BF16)


def _inproj_call(x, mod, gn, w_bf, gq, gk, cos, sa, sb, bmat, lng, lnb, ws_bf, bsm, *, tm, sub, layer):
    b, n, d = x.shape
    nt = n // tm
    in_w = w_bf.shape[-1]
    full = lambda *shape: pl.BlockSpec(shape, lambda bi, i: (0,) * len(shape))
    of_layer = lambda *shape: pl.BlockSpec((None,) + shape, lambda bi, i: (layer,) + (0,) * len(shape))
    in_specs = [
        pl.BlockSpec((1, tm, d), lambda bi, i: (bi, i, 0)),
        pl.BlockSpec((1, 1, 3 * d), lambda bi, i: (bi, 0, 0)),
        full(1, d),
        of_layer(d, in_w),
        full(1, V7X_LANES),
        full(1, V7X_LANES),
        pl.BlockSpec((tm, V7X_LANES), lambda bi, i: (i, 0)),
        pl.BlockSpec((tm, V7X_LANES), lambda bi, i: (i, 0)),
        pl.BlockSpec((tm, V7X_LANES), lambda bi, i: (i, 0)),
        full(V7X_LANES, V7X_LANES),
        full(1, C_WIDTH),
        full(1, C_WIDTH),
        of_layer(C_GROUPS, CHUNK, CHUNK),
        full(CHUNK, C_WIDTH),
    ]
    out_shape = [
        jax.ShapeDtypeStruct((b, A_HEADS, 2, n, A_V_DIM), _BF16),
        jax.ShapeDtypeStruct((b, A_HEADS, n, A_V_DIM), _BF16),
        jax.ShapeDtypeStruct((b, A_HEADS, n // sub, A_V_DIM, sub), _BF16),
        jax.ShapeDtypeStruct((b, n, B_WIDTH), _F32),
        jax.ShapeDtypeStruct((b, n, C_WIDTH), _BF16),
        jax.ShapeDtypeStruct((b, n, A_WIDTH), _BF16),
        jax.ShapeDtypeStruct((b, n, B_WIDTH), _BF16),
    ]
    out_specs = [
        pl.BlockSpec((1, A_HEADS, 2, tm, A_V_DIM), lambda bi, i: (bi, 0, 0, i, 0)),
        pl.BlockSpec((1, A_HEADS, tm, A_V_DIM), lambda bi, i: (bi, 0, i, 0)),
        pl.BlockSpec((1, A_HEADS, tm // sub, A_V_DIM, sub), lambda bi, i: (bi, 0, i, 0, 0)),
        pl.BlockSpec((1, tm, B_WIDTH), lambda bi, i: (bi, i, 0)),
        pl.BlockSpec((1, tm, C_WIDTH), lambda bi, i: (bi, i, 0)),
        pl.BlockSpec((1, tm, A_WIDTH), lambda bi, i: (bi, i, 0)),
        pl.BlockSpec((1, tm, B_WIDTH), lambda bi, i: (bi, i, 0)),
    ]
    return pl.pallas_call(
        functools.partial(_inproj_kernel, sub=sub, d=d),
        grid=(b, nt),
        in_specs=in_specs,
        out_specs=out_specs,
        out_shape=out_shape,
        compiler_params=_params(2),
        name="inproj",
    )(x, mod, gn, w_bf, gq, gk, cos, sa, sb, bmat, lng, lnb, ws_bf, bsm)


def _attn_kernel(*refs, tq, n_src, tiles, lam_init, bounded):
    lam_ref, q2_ref, ga_ref, gsub_ref = refs[:4]
    src_refs = refs[4:4 + 2 * n_src]
    out_ref = refs[4 + 2 * n_src]
    p_sc, m_sc, l_sc, acc_sc = refs[5 + 2 * n_src:]

    q2 = q2_ref[0, 0].reshape(2 * tq, A_V_DIM)
    l_sc[...] = jnp.zeros(l_sc.shape, _F32)
    acc_sc[...] = jnp.zeros(acc_sc.shape, _F32)

    def tile_len(si):
        return src_refs[2 * si + 1].shape[-1]

    def k_tile(si, t):
        tk = tile_len(si)
        if isinstance(t, int):
            return src_refs[2 * si][0, 0, t * tk:(t + 1) * tk, :]
        return src_refs[2 * si][0, 0, pl.ds(pl.multiple_of(t * tk, tk), tk), :]

    def vt_tile(si, t):
        return src_refs[2 * si + 1][0, 0, t]

    if bounded:
        def exponentials(slot, k_t):
            p = jnp.exp2(lax.dot_general(k_t, q2, _NT, preferred_element_type=_F32))
            l_sc[...] += jnp.sum(p, axis=0, keepdims=True)
            p_sc[slot, :k_t.shape[0], :] = p.astype(_BF16)

        def weighted_values(slot, vt_t):
            acc_sc[...] += jnp.dot(vt_t, p_sc[slot, :vt_t.shape[1], :], preferred_element_type=_F32)

        flat = [(si, t) for si in range(n_src) for t in range(tiles[si])]
        exponentials(0, k_tile(*flat[0]))
        for j in range(len(flat)):
            if j + 1 < len(flat):
                exponentials((j + 1) % 2, k_tile(*flat[j + 1]))
            weighted_values(j % 2, vt_tile(*flat[j]))
    else:
        m_sc[...] = jnp.full(m_sc.shape, -jnp.inf, _F32)

        def step(k_t, vt_t):
            s = lax.dot_general(k_t, q2, _NT, preferred_element_type=_F32)
            m_prev = m_sc[...]
            m_new = jnp.maximum(m_prev, jnp.max(s, axis=0, keepdims=True))
            alpha = jnp.exp2(m_prev - m_new)
            p = jnp.exp2(s - m_new)
            l_sc[...] = alpha * l_sc[...] + jnp.sum(p, axis=0, keepdims=True)
            acc_sc[...] = alpha * acc_sc[...] + jnp.dot(vt_t, p.astype(_BF16), preferred_element_type=_F32)
            m_sc[...] = m_new

        for si in range(n_src):
            if tiles[si] == 1:
                step(k_tile(si, 0), vt_tile(si, 0))
            else:
                def body(t, carry, si=si):
                    step(k_tile(si, t), vt_tile(si, t))
                    return carry
                lax.fori_loop(0, tiles[si], body, 0)

    lv = lam_ref[...]
    lam = (jnp.exp(jnp.sum(lv[0:1] * lv[1:2], axis=-1, keepdims=True))
           - jnp.exp(jnp.sum(lv[2:3] * lv[3:4], axis=-1, keepdims=True)) + lam_init)
    inv = 1.0 / l_sc[...]
    acc = acc_sc[...]
    o = acc[:, :tq] * inv[:, :tq] - lam * (acc[:, tq:] * inv[:, tq:])
    ms = jnp.mean(o * o, axis=0, keepdims=True)
    on = o * lax.rsqrt(ms + EPS)
    out = on.T * (gsub_ref[...] * (1.0 - lam_init)) * ga_ref[0].astype(_F32)
    out_ref[0] = out.astype(_BF16)


def _attn_call(lamv, q2, ga, gsub, sources, *, tq, lam_init, bounded):
    b, _, _, n, _ = q2.shape
    in_specs = [
        pl.BlockSpec(lamv.shape, lambda bi, hi, i: (0, 0)),
        pl.BlockSpec((1, 1, 2, tq, A_V_DIM), lambda bi, hi, i: (bi, hi, 0, i, 0)),
        pl.BlockSpec((1, tq, A_V_DIM), lambda bi, hi, i: (bi, i, hi)),
        pl.BlockSpec((1, A_V_DIM), lambda bi, hi, i: (0, 0)),
    ]
    args = [lamv, q2, ga, gsub]
    tiles = []
    for k_s, vt_s in sources:
        ls = k_s.shape[2]
        nt, _, tk = vt_s.shape[2:]
        tiles.append(nt)
        in_specs.append(pl.BlockSpec((1, 1, ls, A_V_DIM), lambda bi, hi, i: (bi, hi, 0, 0)))
        in_specs.append(pl.BlockSpec((1, 1, nt, A_V_DIM, tk), lambda bi, hi, i: (bi, hi, 0, 0, 0)))
        args += [k_s, vt_s]
    tk_max = max(vt_s.shape[-1] for _, vt_s in sources)
    return pl.pallas_call(
        functools.partial(_attn_kernel, tq=tq, n_src=len(sources), tiles=tuple(tiles),
                          lam_init=lam_init, bounded=bounded),
        grid=(b, A_HEADS, n // tq),
        in_specs=in_specs,
        out_specs=pl.BlockSpec((1, tq, A_V_DIM), lambda bi, hi, i: (bi, i, hi)),
        out_shape=jax.ShapeDtypeStruct((b, n, A_WIDTH), _BF16),
        scratch_shapes=[pltpu.VMEM((2, tk_max, 2 * tq), _BF16),
                        pltpu.VMEM((1, 2 * tq), _F32),
                        pltpu.VMEM((1, 2 * tq), _F32),
                        pltpu.VMEM((A_V_DIM, 2 * tq), _F32)],
        compiler_params=_params(3),
        name="diff_attn" if bounded else "diff_attn_general",
    )(*args)


SAFE_EXP2_RANGE = 60.0


def _attention(lamv, q2, ga, gsub, sources, gq, gk, *, tq, lam_init):
    bound = A_QK_DIM * Q_SCALE * jnp.max(jnp.abs(gq)) * jnp.max(jnp.abs(gk))
    call = lambda bounded: functools.partial(_attn_call, tq=tq, lam_init=lam_init, bounded=bounded)
    return lax.cond(bound <= SAFE_EXP2_RANGE, call(True), call(False), lamv, q2, ga, gsub, sources)


def _dft1_kernel(f1_ref, x_ref, o_ref):
    o_ref[0] = jnp.dot(f1_ref[...], x_ref[0].astype(_BF16), preferred_element_type=_F32).astype(_BF16)


def _dft1_call(f1, x2d, *, tc):
    b, n1, w = x2d.shape
    return pl.pallas_call(
        _dft1_kernel,
        grid=(b, w // tc),
        in_specs=[pl.BlockSpec((2 * n1, n1), lambda bi, j: (0, 0)),
                  pl.BlockSpec((1, n1, tc), lambda bi, j: (bi, 0, j))],
        out_specs=pl.BlockSpec((1, 2 * n1, tc), lambda bi, j: (bi, 0, j)),
        out_shape=jax.ShapeDtypeStruct((b, 2 * n1, w), _BF16),
        compiler_params=_params(2),
        name="dft_stage1",
    )(f1, x2d)


def _dft2_kernel(*refs, g, n2, two_stage, norm):
    if two_stage:
        ar_ref, ai_ref, twc_ref, tws_ref, w2_ref, cbd_ref, sbd_ref, wf_ref, bf_ref, out_ref = refs
    else:
        ar_ref, w2_ref, cbd_ref, sbd_ref, wf_ref, bf_ref, out_ref = refs
    w2 = w2_ref[...]
    cbd = cbd_ref[...]
    sbd = sbd_ref[...]
    wf = wf_ref[...]
    bias = bf_ref[...]
    for j in range(g):
        ar = ar_ref[0, 0, j].astype(_F32)
        if two_stage:
            ai = ai_ref[0, 0, j].astype(_F32)
            tc = twc_ref[j]
            ts = tws_ref[j]
            c2 = jnp.concatenate([tc, tc], axis=1)
            s2 = jnp.concatenate([ts, ts], axis=1)
            a = jnp.concatenate([ar * c2 - ai * s2, ar * s2 + ai * c2], axis=0).astype(_BF16)
        else:
            a = ar.astype(_BF16)
        z = jnp.dot(w2, a, preferred_element_type=_F32)
        p = z[:n2].astype(_BF16)
        q = z[n2:].astype(_BF16)
        fr = (jnp.dot(p, cbd, preferred_element_type=_F32) - jnp.dot(q, sbd, preferred_element_type=_F32)) * norm
        y = jnp.dot(fr.astype(_BF16), wf, preferred_element_type=_F32) + bias
        out_ref[0, :, j * B_WIDTH:(j + 1) * B_WIDTH] = y.astype(_BF16)


def _dft_tables(n1, n2):
    n = n1 * n2
    k1 = np.arange(n1)
    a1 = 2.0 * np.pi * ((k1[:, None] * k1[None, :]) % n1) / n1
    f1 = np.concatenate([np.cos(a1), np.sin(a1)], axis=0)
    k2 = np.arange(n2)
    a2 = 2.0 * np.pi * ((k2[:, None] * k2[None, :]) % n2) / n2
    c2, s2 = np.cos(a2), np.sin(a2)
    w2 = np.block([[c2, -s2], [s2, c2]]) if n1 > 1 else np.concatenate([c2, s2], axis=0)
    gch = np.arange(B_GROUP_DIM)
    ag = 2.0 * np.pi * ((gch[:, None] * gch[None, :]) % B_GROUP_DIM) / B_GROUP_DIM
    eye = np.eye(B_GROUPS)
    cbd = np.kron(eye, np.cos(ag))
    sbd = np.kron(eye, np.sin(ag))
    to_bf16 = lambda t: jnp.asarray(t, _F32).astype(_BF16)
    return to_bf16(f1), to_bf16(w2), to_bf16(cbd), to_bf16(sbd), 1.0 / math.sqrt(n * B_GROUP_DIM)


def _twiddles(n1, n2):
    k1 = jnp.arange(n1, dtype=jnp.int32)[:, None]
    m2 = jnp.arange(n2, dtype=jnp.int32)[None, :]
    ang = (k1 * m2).astype(_F32) * (2.0 * math.pi / (n1 * n2))
    shape = (n1, n2, V7X_LANES)
    return (jnp.broadcast_to(jnp.cos(ang)[:, :, None], shape), jnp.broadcast_to(jnp.sin(ang)[:, :, None], shape))


def _fourier(f, wf_bd, bf_row):
    b, n, _ = f.shape
    two_stage = n > 2 * V7X_LANES
    n2 = V7X_LANES if two_stage else n
    n1 = n // n2
    g = min(n1, V7X_SUBLANES)
    f1, w2, cbd, sbd, norm = _dft_tables(n1, n2)
    full = lambda *shape: pl.BlockSpec(shape, lambda bi, j: (0,) * len(shape))
    out_spec = pl.BlockSpec((1, n2, g * B_WIDTH), lambda bi, j: (bi, 0, j))
    consts = [w2, cbd, sbd, wf_bd, bf_row]
    const_specs = [full(*w2.shape), full(B_WIDTH, B_WIDTH), full(B_WIDTH, B_WIDTH), full(B_WIDTH, B_WIDTH),
                   full(1, B_WIDTH)]
    if two_stage:
        tc = min(n2 * B_WIDTH, 4096)
        a = _dft1_call(f1, f.reshape(b, n1, n2 * B_WIDTH), tc=tc)
        a5 = a.reshape(b, 2, n1, n2, B_WIDTH)
        twc, tws = _twiddles(n1, n2)
        args = [a5, a5, twc, tws] + consts
        in_specs = [pl.BlockSpec((1, 1, g, n2, B_WIDTH), lambda bi, j: (bi, 0, j, 0, 0)),
                    pl.BlockSpec((1, 1, g, n2, B_WIDTH), lambda bi, j: (bi, 1, j, 0, 0)),
                    pl.BlockSpec((g, n2, V7X_LANES), lambda bi, j: (j, 0, 0)),
                    pl.BlockSpec((g, n2, V7X_LANES), lambda bi, j: (j, 0, 0))] + const_specs
    else:
        args = [f.reshape(b, 1, 1, n2, B_WIDTH)] + consts
        in_specs = [pl.BlockSpec((1, 1, 1, n2, B_WIDTH), lambda bi, j: (bi, 0, 0, 0, 0))] + const_specs
    out = pl.pallas_call(
        functools.partial(_dft2_kernel, g=g, n2=n2, two_stage=two_stage, norm=norm),
        grid=(b, n1 // g),
        in_specs=in_specs,
        out_specs=out_spec,
        out_shape=jax.ShapeDtypeStruct((b, n2, n1 * B_WIDTH), _BF16),
        compiler_params=_params(2),
        name="dft_stage2",
    )(*args)
    return out.reshape(b, n, B_WIDTH)


def _merge_kernel(oa_ref, ob_ref, gb_ref, oc_ref, x_ref, mod_ref, gn_ref, wm_ref, wa_ref, wb_ref, wc_ref, wo_ref,
                  o_ref, *, d, sub):
    mod = mod_ref[0]
    for r0 in range(0, x_ref.shape[1], sub):
        rows = slice(r0, r0 + sub)
        x = x_ref[0, rows, :]
        hb = _modulated_norm(x, mod, gn_ref[...], d)
        branches = (jnp.dot(oa_ref[0, rows, :], wa_ref[...], preferred_element_type=_F32),
                    jnp.dot(ob_ref[0, rows, :] * gb_ref[0, rows, :], wb_ref[...], preferred_element_type=_F32),
                    jnp.dot(oc_ref[0, rows, :], wc_ref[...], preferred_element_type=_F32))
        y = None
        for j, yj in enumerate(branches):
            mj = _sigmoid(jnp.dot(hb, wm_ref[:, OFF_MERGE + j * d:OFF_MERGE + (j + 1) * d],
                                  preferred_element_type=_F32))
            y = mj * yj if y is None else y + mj * yj
        out = jnp.dot(y.astype(_BF16), wo_ref[...], preferred_element_type=_F32)
        o_ref[0, rows, :] = x + mod[:, 2 * d:] * out


def _merge_call(oa, ob, gb, oc, x, mod, gn, w_in_bf, wa, wb, wc, wo, *, tm, sub, layer):
    b, n, d = x.shape
    row = lambda w: pl.BlockSpec((1, tm, w), lambda bi, i: (bi, i, 0))
    of_layer = lambda *shape: pl.BlockSpec((None,) + shape, lambda bi, i: (layer,) + (0,) * len(shape))
    return pl.pallas_call(
        functools.partial(_merge_kernel, d=d, sub=sub),
        grid=(b, n // tm),
        in_specs=[row(A_WIDTH), row(B_WIDTH), row(B_WIDTH), row(C_WIDTH), row(d),
                  pl.BlockSpec((1, 1, 3 * d), lambda bi, i: (bi, 0, 0)),
                  pl.BlockSpec((1, d), lambda bi, i: (0, 0)),
                  of_layer(d, w_in_bf.shape[-1]), of_layer(A_WIDTH, d), of_layer(B_WIDTH, d),
                  of_layer(C_WIDTH, d), of_layer(d, d)],
        out_specs=row(d),
        out_shape=jax.ShapeDtypeStruct((b, n, d), _F32),
        compiler_params=_params(2),
        name="merge",
    )(oa, ob, gb, oc, x, mod, gn, w_in_bf, wa, wb, wc, wo)


def _rope_tables(n):
    pos = jnp.arange(n, dtype=jnp.int32)
    row = (pos // GRID_W).astype(_F32)
    col = (pos % GRID_W).astype(_F32)
    freqs = ROPE_BASE ** (-jnp.arange(0, AXIS_DIM, 2, dtype=_F32) / AXIS_DIM)
    ang_r = row[:, None] * freqs
    ang_c = col[:, None] * freqs
    ang = jnp.concatenate([ang_r, ang_r, ang_c, ang_c], axis=-1)
    cos, sin = jnp.cos(ang), jnp.sin(ang)
    first_half = (jnp.arange(A_QK_DIM) % AXIS_DIM) < AXIS_DIM // 2
    sa = jnp.where(first_half, -sin, 0.0)
    sb = jnp.where(first_half, 0.0, sin)
    tile2 = lambda t: jnp.concatenate([t, t], axis=-1)
    return tile2(cos), tile2(sa), tile2(sb)


def _no_rope_tables(n):
    return (jnp.ones((n, V7X_LANES), _F32), jnp.zeros((n, V7X_LANES), _F32), jnp.zeros((n, V7X_LANES), _F32))


def kernel(x, c, ctx, c_ctx, w_ada, b_ada, g_norm, w_in, g_q, g_k, lam_q1, lam_k1, lam_q2, lam_k2, g_sub,
           w_f, b_f, ln_g, ln_b, w_s, b_s, w_br_a, w_br_b, w_br_c, w_out):
    b, n, d = x.shape
    n_ctx = ctx.shape[1]
    depth = w_in.shape[0]
    tm_lat = min(n, 512)
    tm_ctx = min(n_ctx, 512)

    n_rows = V7X_SUBLANES * pl.cdiv(b + 1, V7X_SUBLANES)
    rows = jnp.concatenate([c, c_ctx[None, :], jnp.zeros((n_rows - b - 1, d), _F32)], axis=0)
    mod_all = _mod_call(rows, w_ada, b_ada)

    rope_lat = _rope_tables(n)
    rope_ctx = _no_rope_tables(n_ctx)
    group_mean = np.kron(np.eye(V7X_LANES // A_QK_DIM), np.full((A_QK_DIM, A_QK_DIM), 1.0 / A_QK_DIM))
    bmat = jnp.asarray(group_mean, _BF16)

    w_in_bf = w_in.astype(_BF16)
    w_s_bf = w_s.astype(_BF16)
    merge_w = (w_in_bf, w_br_a.astype(_BF16), w_br_b.astype(_BF16), w_br_c.astype(_BF16), w_out.astype(_BF16))

    for l in range(depth):
        last = l == depth - 1
        lam_init = 0.8 - 0.6 * math.exp(-0.3 * l)
        lamv = jnp.stack([lam_q1[l], lam_k1[l], lam_q2[l], lam_k2[l]], axis=0)
        shared = dict(
            gn=g_norm[l][None, :], w_bf=w_in_bf,
            gq=jnp.tile(g_q[l], 2)[None, :], gk=jnp.tile(g_k[l], 2)[None, :], bmat=bmat,
            lng=ln_g[l][None, :], lnb=ln_b[l][None, :], ws_bf=w_s_bf,
            bsm=jnp.repeat(b_s[l].T, C_GROUP_DIM, axis=1), layer=l)
        gsub = g_sub[l][None, :]
        wf_bd = jax.scipy.linalg.block_diag(*[w_f[l, gi] for gi in range(B_GROUPS)]).astype(_BF16)
        bf_row = b_f[l].reshape(1, B_WIDTH)

        mod_lat = mod_all[l, :b][:, None, :]
        mod_ctx = jnp.broadcast_to(mod_all[l, b][None, None, :], (b, 1, 3 * d))

        cq2, ck, cvt, cf, coc, cga, cgb = _inproj_call(
            ctx, mod_ctx, cos=rope_ctx[0], sa=rope_ctx[1], sb=rope_ctx[2], tm=tm_ctx, sub=tm_ctx, **shared)
        q2, k, vt, f, oc, ga, gb = _inproj_call(
            x, mod_lat, cos=rope_lat[0], sa=rope_lat[1], sb=rope_lat[2], tm=min(n, 2 * tm_lat), sub=tm_lat,
            **shared)

        oa = _attention(lamv, q2, ga, gsub, [(ck, cvt), (k, vt)], g_q[l], g_k[l], tq=tm_lat, lam_init=lam_init)
        ob = _fourier(f, wf_bd, bf_row)
        x_new = _merge_call(oa, ob, gb, oc, x, mod_lat, shared["gn"], *merge_w, tm=min(n, 2 * tm_lat), sub=tm_lat,
                            layer=l)

        if not last:
            coa = _attention(lamv, cq2, cga, gsub, [(ck, cvt)], g_q[l], g_k[l], tq=tm_ctx, lam_init=lam_init)
            cob = _fourier(cf, wf_bd, bf_row)
            ctx = _merge_call(coa, cob, cgb, coc, ctx, mod_ctx, shared["gn"], *merge_w, tm=tm_ctx, sub=tm_ctx,
                              layer=l)
        x = x_new
    return x
```

```python
import functools
import math

import numpy as np
import jax
import jax.numpy as jnp
from jax import lax
from jax.experimental import pallas as pl
from jax.experimental.pallas import tpu as pltpu

A_HEADS = 4
A_QK_DIM = 64
A_V_DIM = 2 * A_QK_DIM
A_WIDTH = A_HEADS * A_V_DIM
B_GROUPS = 4
B_GROUP_DIM = 64
B_WIDTH = B_GROUPS * B_GROUP_DIM
C_GROUPS = 4
C_GROUP_DIM = 64
C_WIDTH = C_GROUPS * C_GROUP_DIM
CHUNK = 128
GRID_W = 64
ROPE_BASE = 10000.0
AXIS_DIM = A_QK_DIM // 2
EPS = 1e-6

OFF_Q = 0
OFF_K = OFF_Q + A_WIDTH
OFF_V = OFF_K + A_WIDTH
OFF_F = OFF_V + A_WIDTH
OFF_U = OFF_F + B_WIDTH
OFF_VC = OFF_U + C_WIDTH
OFF_GATE = OFF_VC + C_WIDTH
GATE_WIDTH = A_WIDTH + B_WIDTH + C_WIDTH
OFF_MERGE = OFF_GATE + GATE_WIDTH

V7X_LANES = 128
V7X_SUBLANES = 8
V7X_BF16_SUBLANES = 16
V7X_MXU_DIM = 256
V7X_VMEM_LIMIT_BYTES = 56 * 1024 * 1024

LOG2E = math.log2(math.e)
Q_SCALE = (A_QK_DIM ** -0.5) * LOG2E

_F32 = jnp.float32
_BF16 = jnp.bfloat16
_NT = (((1,), (1,)), ((), ()))


def _sigmoid(x):
    return 1.0 / (1.0 + jnp.exp(-x))


def _params(n_axes):
    return pltpu.CompilerParams(dimension_semantics=("arbitrary",) * n_axes,
                                vmem_limit_bytes=V7X_VMEM_LIMIT_BYTES)


def _mod_kernel(c_ref, w_ref, b_ref, o_ref):
    cc = c_ref[...]
    s = cc * _sigmoid(cc)
    o_ref[0] = jnp.dot(s, w_ref[0], preferred_element_type=_F32) + b_ref[0]


def _mod_call(rows, w_ada, b_ada):
    depth, d, d3 = w_ada.shape
    nb = d3 // d
    return pl.pallas_call(
        _mod_kernel,
        grid=(depth, nb),
        in_specs=[pl.BlockSpec((rows.shape[0], d), lambda l, j: (0, 0)),
                  pl.BlockSpec((1, d, d), lambda l, j: (l, 0, j)),
                  pl.BlockSpec((1, 1, d), lambda l, j: (l, 0, j))],
        out_specs=pl.BlockSpec((1, rows.shape[0], d), lambda l, j: (l, 0, j)),
        out_shape=jax.ShapeDtypeStruct((depth, rows.shape[0], d3), _F32),
        compiler_params=_params(2),
        name="adaln_mod",
    )(rows, w_ada, b_ada.reshape(depth, 1, d3))


def _modulated_norm(x, mod, gn, d):
    ms = jnp.mean(x * x, axis=-1, keepdims=True)
    h = (x * lax.rsqrt(ms + EPS) * gn) * (1.0 + mod[:, d:2 * d]) + mod[:, :d]
    return h.astype(_BF16)


def _inproj_kernel(x_ref, mod_ref, gn_ref, w_ref, gq_ref, gk_ref, cos_ref, sa_ref, sb_ref,
                   bmat_ref, lng_ref, lnb_ref, ws_ref, bsm_ref,
                   q2_ref, k_ref, vt_ref, f_ref, oc_ref, ga_ref, gb_ref, *, sub, d):
    bmat = bmat_ref[...]
    lo = lax.broadcasted_iota(jnp.int32, (sub, V7X_LANES), 1) < A_QK_DIM
    lo_c = lax.broadcasted_iota(jnp.int32, (CHUNK, V7X_LANES), 1) < C_GROUP_DIM
    bsm = bsm_ref[...]
    gq = gq_ref[...]
    gk = gk_ref[...]

    for si in range(x_ref.shape[1] // sub):
        rows = slice(si * sub, (si + 1) * sub)
        hb = _modulated_norm(x_ref[0, rows, :], mod_ref[0], gn_ref[...], d)

        def proj(off, width, hb=hb):
            return jnp.dot(hb, w_ref[:, off:off + width], preferred_element_type=_F32)

        cos = cos_ref[rows, :]
        sa = sa_ref[rows, :]
        sb = sb_ref[rows, :]

        def qk_head(p_h, g, cos=cos, sa=sa, sb=sb):
            msq = jnp.dot((p_h * p_h).astype(_BF16), bmat, preferred_element_type=_F32)
            t = p_h * lax.rsqrt(msq + EPS) * g
            return (t * cos + pltpu.roll(t, V7X_LANES - AXIS_DIM // 2, 1) * sa
                    + pltpu.roll(t, AXIS_DIM // 2, 1) * sb)

        pq = proj(OFF_Q, A_WIDTH)
        for hh in range(A_HEADS):
            t = qk_head(pq[:, hh * A_V_DIM:(hh + 1) * A_V_DIM], gq) * Q_SCALE
            q2_ref[0, hh, 0, rows, :] = jnp.where(lo, t, 0.0).astype(_BF16)
            q2_ref[0, hh, 1, rows, :] = jnp.where(lo, 0.0, t).astype(_BF16)

        pk = proj(OFF_K, A_WIDTH)
        for hh in range(A_HEADS):
            k_ref[0, hh, rows, :] = qk_head(pk[:, hh * A_V_DIM:(hh + 1) * A_V_DIM], gk).astype(_BF16)

        pv = proj(OFF_V, A_WIDTH)
        for hh in range(A_HEADS):
            vt_ref[0, hh, si] = pv[:, hh * A_V_DIM:(hh + 1) * A_V_DIM].T.astype(_BF16)

        f_ref[0, rows, :] = proj(OFF_F, B_WIDTH)

        u = proj(OFF_U, C_WIDTH)
        vc = proj(OFF_VC, C_WIDTH)
        mu = jnp.mean(vc, axis=-1, keepdims=True)
        dv = vc - mu
        var = jnp.mean(dv * dv, axis=-1, keepdims=True)
        vn = (dv * lax.rsqrt(var + EPS) * lng_ref[...] + lnb_ref[...]).astype(_BF16)

        pg = proj(OFF_GATE, GATE_WIDTH)
        gate = pg * _sigmoid(pg)
        ga_ref[0, rows, :] = gate[:, :A_WIDTH].astype(_BF16)
        gb_ref[0, rows, :] = gate[:, A_WIDTH:A_WIDTH + B_WIDTH].astype(_BF16)
        gate_c = gate[:, A_WIDTH + B_WIDTH:]

        for t in range(sub // CHUNK):
            crows = slice(t * CHUNK, (t + 1) * CHUNK)
            vl = vn[crows, :V7X_LANES]
            vr = vn[crows, V7X_LANES:]
            s_l = jnp.where(lo_c, jnp.dot(ws_ref[0], vl, preferred_element_type=_F32),
                            jnp.dot(ws_ref[1], vl, preferred_element_type=_F32))
            s_r = jnp.where(lo_c, jnp.dot(ws_ref[2], vr, preferred_element_type=_F32),
                            jnp.dot(ws_ref[3], vr, preferred_element_type=_F32))
            s = jnp.concatenate([s_l, s_r], axis=1) + bsm
            oc_ref[0, si * sub + t * CHUNK:si * sub + (t + 1) * CHUNK, :] = (
                u[crows] * s * gate_c[crows]).astype(_BF16)


def _inproj_call(x, mod, gn, w_bf, gq, gk, cos, sa, sb, bmat, lng, lnb, ws_bf, bsm, *, tm, sub, layer):
    b, n, d = x.shape
    nt = n // tm
    in_w = w_bf.shape[-1]
    full = lambda *shape: pl.BlockSpec(shape, lambda bi, i: (0,) * len(shape))
    of_layer = lambda *shape: pl.BlockSpec((None,) + shape, lambda bi, i: (layer,) + (0,) * len(shape))
    in_specs = [
        pl.BlockSpec((1, tm, d), lambda bi, i: (bi, i, 0)),
        pl.BlockSpec((1, 1, 3 * d), lambda bi, i: (bi, 0, 0)),
        full(1, d),
        of_layer(d, in_w),
        full(1, V7X_LANES),
        full(1, V7X_LANES),
        pl.BlockSpec((tm, V7X_LANES), lambda bi, i: (i, 0)),
        pl.BlockSpec((tm, V7X_LANES), lambda bi, i: (i, 0)),
        pl.BlockSpec((tm, V7X_LANES), lambda bi, i: (i, 0)),
        full(V7X_LANES, V7X_LANES),
        full(1, C_WIDTH),
        full(1, C_WIDTH),
        of_layer(C_GROUPS, CHUNK, CHUNK),
        full(CHUNK, C_WIDTH),
    ]
    out_shape = [
        jax.ShapeDtypeStruct((b, A_HEADS, 2, n, A_V_DIM), _BF16),
        jax.ShapeDtypeStruct((b, A_HEADS, n, A_V_DIM), _BF16),
        jax.ShapeDtypeStruct((b, A_HEADS, n // sub, A_V_DIM, sub), _BF16),
        jax.ShapeDtypeStruct((b, n, B_WIDTH), _F32),
        jax.ShapeDtypeStruct((b, n, C_WIDTH), _BF16),
        jax.ShapeDtypeStruct((b, n, A_WIDTH), _BF16),
        jax.ShapeDtypeStruct((b, n, B_WIDTH), _BF16),
    ]
    out_specs = [
        pl.BlockSpec((1, A_HEADS, 2, tm, A_V_DIM), lambda bi, i: (bi, 0, 0, i, 0)),
        pl.BlockSpec((1, A_HEADS, tm, A_V_DIM), lambda bi, i: (bi, 0, i, 0)),
        pl.BlockSpec((1, A_HEADS, tm // sub, A_V_DIM, sub), lambda bi, i: (bi, 0, i, 0, 0)),
        pl.BlockSpec((1, tm, B_WIDTH), lambda bi, i: (bi, i, 0)),
        pl.BlockSpec((1, tm, C_WIDTH), lambda bi, i: (bi, i, 0)),
        pl.BlockSpec((1, tm, A_WIDTH), lambda bi, i: (bi, i, 0)),
        pl.BlockSpec((1, tm, B_WIDTH), lambda bi, i: (bi, i, 0)),
    ]
    return pl.pallas_call(
        functools.partial(_inproj_kernel, sub=sub, d=d),
        grid=(b, nt),
        in_specs=in_specs,
        out_specs=out_specs,
        out_shape=out_shape,
        compiler_params=_params(2),
        name="inproj",
    )(x, mod, gn, w_bf, gq, gk, cos, sa, sb, bmat, lng, lnb, ws_bf, bsm)


def _attn_kernel(*refs, tq, n_src, tiles, lam_init, bounded):
    lam_ref, q2_ref, ga_ref, gsub_ref = refs[:4]
    src_refs = refs[4:4 + 2 * n_src]
    out_ref = refs[4 + 2 * n_src]
    p_sc, m_sc, l_sc, acc_sc = refs[5 + 2 * n_src:]

    q2 = q2_ref[0, 0].reshape(2 * tq, A_V_DIM)
    l_sc[...] = jnp.zeros(l_sc.shape, _F32)
    acc_sc[...] = jnp.zeros(acc_sc.shape, _F32)

    def tile_len(si):
        return src_refs[2 * si + 1].shape[-1]

    def k_tile(si, t):
        tk = tile_len(si)
        if isinstance(t, int):
            return src_refs[2 * si][0, 0, t * tk:(t + 1) * tk, :]
        return src_refs[2 * si][0, 0, pl.ds(pl.multiple_of(t * tk, tk), tk), :]

    def vt_tile(si, t):
        return src_refs[2 * si + 1][0, 0, t]

    if bounded:
        def exponentials(slot, k_t):
            p = jnp.exp2(lax.dot_general(k_t, q2, _NT, preferred_element_type=_F32))
            l_sc[...] += jnp.sum(p, axis=0, keepdims=True)
            p_sc[slot, :k_t.shape[0], :] = p.astype(_BF16)

        def weighted_values(slot, vt_t):
            acc_sc[...] += jnp.dot(vt_t, p_sc[slot, :vt_t.shape[1], :], preferred_element_type=_F32)

        flat = [(si, t) for si in range(n_src) for t in range(tiles[si])]
        exponentials(0, k_tile(*flat[0]))
        for j in range(len(flat)):
            if j + 1 < len(flat):
                exponentials((j + 1) % 2, k_tile(*flat[j + 1]))
            weighted_values(j % 2, vt_tile(*flat[j]))
    else:
        m_sc[...] = jnp.full(m_sc.shape, -jnp.inf, _F32)

        def step(k_t, vt_t):
            s = lax.dot_general(k_t, q2, _NT, preferred_element_type=_F32)
            m_prev = m_sc[...]
            m_new = jnp.maximum(m_prev, jnp.max(s, axis=0, keepdims=True))
            alpha = jnp.exp2(m_prev - m_new)
            p = jnp.exp2(s - m_new)
            l_sc[...] = alpha * l_sc[...] + jnp.sum(p, axis=0, keepdims=True)
            acc_sc[...] = alpha * acc_sc[...] + jnp.dot(vt_t, p.astype(_BF16), preferred_element_type=_F32)
            m_sc[...] = m_new

        for si in range(n_src):
            if tiles[si] == 1:
                step(k_tile(si, 0), vt_tile(si, 0))
            else:
                def body(t, carry, si=si):
                    step(k_tile(si, t), vt_tile(si, t))
                    return carry
                lax.fori_loop(0, tiles[si], body, 0)

    lv = lam_ref[...]
    lam = (jnp.exp(jnp.sum(lv[0:1] * lv[1:2], axis=-1, keepdims=True))
           - jnp.exp(jnp.sum(lv[2:3] * lv[3:4], axis=-1, keepdims=True)) + lam_init)
    inv = 1.0 / l_sc[...]
    acc = acc_sc[...]
    o = acc[:, :tq] * inv[:, :tq] - lam * (acc[:, tq:] * inv[:, tq:])
    ms = jnp.mean(o * o, axis=0, keepdims=True)
    on = o * lax.rsqrt(ms + EPS)
    out = on.T * (gsub_ref[...] * (1.0 - lam_init)) * ga_ref[0].astype(_F32)
    out_ref[0] = out.astype(_BF16)


def _attn_call(lamv, q2, ga, gsub, sources, *, tq, lam_init, bounded):
    b, _, _, n, _ = q2.shape
    in_specs = [
        pl.BlockSpec(lamv.shape, lambda bi, hi, i: (0, 0)),
        pl.BlockSpec((1, 1, 2, tq, A_V_DIM), lambda bi, hi, i: (bi, hi, 0, i, 0)),
        pl.BlockSpec((1, tq, A_V_DIM), lambda bi, hi, i: (bi, i, hi)),
        pl.BlockSpec((1, A_V_DIM), lambda bi, hi, i: (0, 0)),
    ]
    args = [lamv, q2, ga, gsub]
    tiles = []
    for k_s, vt_s in sources:
        ls = k_s.shape[2]
        nt, _, tk = vt_s.shape[2:]
        tiles.append(nt)
        in_specs.append(pl.BlockSpec((1, 1, ls, A_V_DIM), lambda bi, hi, i: (bi, hi, 0, 0)))
        in_specs.append(pl.BlockSpec((1, 1, nt, A_V_DIM, tk), lambda bi, hi, i: (bi, hi, 0, 0, 0)))
        args += [k_s, vt_s]
    tk_max = max(vt_s.shape[-1] for _, vt_s in sources)
    return pl.pallas_call(
        functools.partial(_attn_kernel, tq=tq, n_src=len(sources), tiles=tuple(tiles),
                          lam_init=lam_init, bounded=bounded),
        grid=(b, A_HEADS, n // tq),
        in_specs=in_specs,
        out_specs=pl.BlockSpec((1, tq, A_V_DIM), lambda bi, hi, i: (bi, i, hi)),
        out_shape=jax.ShapeDtypeStruct((b, n, A_WIDTH), _BF16),
        scratch_shapes=[pltpu.VMEM((2, tk_max, 2 * tq), _BF16),
                        pltpu.VMEM((1, 2 * tq), _F32),
                        pltpu.VMEM((1, 2 * tq), _F32),
                        pltpu.VMEM((A_V_DIM, 2 * tq), _F32)],
        compiler_params=_params(3),
        name="diff_attn" if bounded else "diff_attn_general",
    )(*args)


SAFE_EXP2_RANGE = 60.0


def _attention(lamv, q2, ga, gsub, sources, gq, gk, *, tq, lam_init):
    bound = A_QK_DIM * Q_SCALE * jnp.max(jnp.abs(gq)) * jnp.max(jnp.abs(gk))
    call = lambda bounded: functools.partial(_attn_call, tq=tq, lam_init=lam_init, bounded=bounded)
    return lax.cond(bound <= SAFE_EXP2_RANGE, call(True), call(False), lamv, q2, ga, gsub, sources)


def _dft1_kernel(f1_ref, x_ref, o_ref, *, rows_per_step):
    z = jnp.dot(f1_ref[...], x_ref[0].astype(_BF16), preferred_element_type=_F32)
    o_ref[0] = z.reshape(z.shape[0], rows_per_step, B_WIDTH).astype(_BF16)


def _dft1_call(f1, x2d, *, n2, rows_per_step):
    b, n1, _ = x2d.shape
    tc = rows_per_step * B_WIDTH
    return pl.pallas_call(
        functools.partial(_dft1_kernel, rows_per_step=rows_per_step),
        grid=(b, n2 // rows_per_step),
        in_specs=[pl.BlockSpec((2 * n1, n1), lambda bi, j: (0, 0)),
                  pl.BlockSpec((1, n1, tc), lambda bi, j: (bi, 0, j))],
        out_specs=pl.BlockSpec((1, 2 * n1, rows_per_step, B_WIDTH), lambda bi, j: (bi, 0, j, 0)),
        out_shape=jax.ShapeDtypeStruct((b, 2 * n1, n2, B_WIDTH), _BF16),
        compiler_params=_params(2),
        name="dft_stage1",
    )(f1, x2d)


def _dft2_kernel(*refs, g, n2, two_stage, norm):
    if two_stage:
        ar_ref, ai_ref, twc_ref, tws_ref, w2_ref, cbd_ref, sbd_ref, wf_ref, bf_ref, out_ref = refs
    else:
        ar_ref, w2_ref, cbd_ref, sbd_ref, wf_ref, bf_ref, out_ref = refs
    w2 = w2_ref[...]
    cbd = cbd_ref[...]
    sbd = sbd_ref[...]
    wf = wf_ref[...]
    bias = bf_ref[...]
    ys = []
    for j in range(g):
        ar = ar_ref[0, 0, j].astype(_F32)
        if two_stage:
            ai = ai_ref[0, 0, j].astype(_F32)
            tc = twc_ref[j]
            ts = tws_ref[j]
            c2 = jnp.concatenate([tc, tc], axis=1)
            s2 = jnp.concatenate([ts, ts], axis=1)
            a = jnp.concatenate([ar * c2 - ai * s2, ar * s2 + ai * c2], axis=0).astype(_BF16)
        else:
            a = ar.astype(_BF16)
        z = jnp.dot(w2, a, preferred_element_type=_F32)
        p = z[:n2].astype(_BF16)
        q = z[n2:].astype(_BF16)
        fr = (jnp.dot(p, cbd, preferred_element_type=_F32) - jnp.dot(q, sbd, preferred_element_type=_F32)) * norm
        ys.append(jnp.dot(fr.astype(_BF16), wf, preferred_element_type=_F32) + bias)
    if two_stage:
        out_ref[0] = jnp.swapaxes(jnp.stack(ys, axis=0), 0, 1).astype(_BF16)
    else:
        out_ref[0] = ys[0].astype(_BF16)


def _dft_tables(n1, n2):
    n = n1 * n2
    k1 = np.arange(n1)
    a1 = 2.0 * np.pi * ((k1[:, None] * k1[None, :]) % n1) / n1
    f1 = np.concatenate([np.cos(a1), np.sin(a1)], axis=0)
    k2 = np.arange(n2)
    a2 = 2.0 * np.pi * ((k2[:, None] * k2[None, :]) % n2) / n2
    c2, s2 = np.cos(a2), np.sin(a2)
    w2 = np.block([[c2, -s2], [s2, c2]]) if n1 > 1 else np.concatenate([c2, s2], axis=0)
    gch = np.arange(B_GROUP_DIM)
    ag = 2.0 * np.pi * ((gch[:, None] * gch[None, :]) % B_GROUP_DIM) / B_GROUP_DIM
    eye = np.eye(B_GROUPS)
    cbd = np.kron(eye, np.cos(ag))
    sbd = np.kron(eye, np.sin(ag))
    to_bf16 = lambda t: jnp.asarray(t, _F32).astype(_BF16)
    return to_bf16(f1), to_bf16(w2), to_bf16(cbd), to_bf16(sbd), 1.0 / math.sqrt(n * B_GROUP_DIM)


def _twiddles(n1, n2):
    k1 = jnp.arange(n1, dtype=jnp.int32)[:, None]
    m2 = jnp.arange(n2, dtype=jnp.int32)[None, :]
    ang = (k1 * m2).astype(_F32) * (2.0 * math.pi / (n1 * n2))
    shape = (n1, n2, V7X_LANES)
    return (jnp.broadcast_to(jnp.cos(ang)[:, :, None], shape), jnp.broadcast_to(jnp.sin(ang)[:, :, None], shape))


def _fourier(f, wf_bd, bf_row):
    b, n, _ = f.shape
    two_stage = n > 2 * V7X_LANES
    n2 = V7X_LANES if two_stage else n
    n1 = n // n2
    g = min(n1, V7X_BF16_SUBLANES)
    f1, w2, cbd, sbd, norm = _dft_tables(n1, n2)
    full = lambda *shape: pl.BlockSpec(shape, lambda bi, j: (0,) * len(shape))
    consts = [w2, cbd, sbd, wf_bd, bf_row]
    const_specs = [full(*w2.shape), full(B_WIDTH, B_WIDTH), full(B_WIDTH, B_WIDTH), full(B_WIDTH, B_WIDTH),
                   full(1, B_WIDTH)]
    if two_stage:
        a = _dft1_call(f1, f.reshape(b, n1, n2 * B_WIDTH), n2=n2, rows_per_step=V7X_BF16_SUBLANES)
        a5 = a.reshape(b, 2, n1, n2, B_WIDTH)
        out_shape = jax.ShapeDtypeStruct((b, n2, n1, B_WIDTH), _BF16)
        out_spec = pl.BlockSpec((1, n2, g, B_WIDTH), lambda bi, j: (bi, 0, j, 0))
        twc, tws = _twiddles(n1, n2)
        args = [a5, a5, twc, tws] + consts
        in_specs = [pl.BlockSpec((1, 1, g, n2, B_WIDTH), lambda bi, j: (bi, 0, j, 0, 0)),
                    pl.BlockSpec((1, 1, g, n2, B_WIDTH), lambda bi, j: (bi, 1, j, 0, 0)),
                    pl.BlockSpec((g, n2, V7X_LANES), lambda bi, j: (j, 0, 0)),
                    pl.BlockSpec((g, n2, V7X_LANES), lambda bi, j: (j, 0, 0))] + const_specs
    else:
        args = [f.reshape(b, 1, 1, n2, B_WIDTH)] + consts
        in_specs = [pl.BlockSpec((1, 1, 1, n2, B_WIDTH), lambda bi, j: (bi, 0, 0, 0, 0))] + const_specs
        out_shape = jax.ShapeDtypeStruct((b, n2, B_WIDTH), _BF16)
        out_spec = pl.BlockSpec((1, n2, B_WIDTH), lambda bi, j: (bi, 0, 0))
    out = pl.pallas_call(
        functools.partial(_dft2_kernel, g=g, n2=n2, two_stage=two_stage, norm=norm),
        grid=(b, n1 // g),
        in_specs=in_specs,
        out_specs=out_spec,
        out_shape=out_shape,
        compiler_params=_params(2),
        name="dft_stage2",
    )(*args)
    return out.reshape(b, n, B_WIDTH)


def _merge_kernel(oa_ref, ob_ref, gb_ref, oc_ref, x_ref, mod_ref, gn_ref, wm_ref, wa_ref, wb_ref, wc_ref, wo_ref,
                  o_ref, *, d, sub):
    mod = mod_ref[0]
    for r0 in range(0, x_ref.shape[1], sub):
        rows = slice(r0, r0 + sub)
        x = x_ref[0, rows, :]
        hb = _modulated_norm(x, mod, gn_ref[...], d)
        branches = (jnp.dot(oa_ref[0, rows, :], wa_ref[...], preferred_element_type=_F32),
                    jnp.dot(ob_ref[0, rows, :] * gb_ref[0, rows, :], wb_ref[...], preferred_element_type=_F32),
                    jnp.dot(oc_ref[0, rows, :], wc_ref[...], preferred_element_type=_F32))
        y = None
        for j, yj in enumerate(branches):
            mj = _sigmoid(jnp.dot(hb, wm_ref[:, OFF_MERGE + j * d:OFF_MERGE + (j + 1) * d],
                                  preferred_element_type=_F32))
            y = mj * yj if y is None else y + mj * yj
        out = jnp.dot(y.astype(_BF16), wo_ref[...], preferred_element_type=_F32)
        o_ref[0, rows, :] = x + mod[:, 2 * d:] * out


def _merge_call(oa, ob, gb, oc, x, mod, gn, w_in_bf, wa, wb, wc, wo, *, tm, sub, layer):
    b, n, d = x.shape
    row = lambda w: pl.BlockSpec((1, tm, w), lambda bi, i: (bi, i, 0))
    of_layer = lambda *shape: pl.BlockSpec((None,) + shape, lambda bi, i: (layer,) + (0,) * len(shape))
    return pl.pallas_call(
        functools.partial(_merge_kernel, d=d, sub=sub),
        grid=(b, n // tm),
        in_specs=[row(A_WIDTH), row(B_WIDTH), row(B_WIDTH), row(C_WIDTH), row(d),
                  pl.BlockSpec((1, 1, 3 * d), lambda bi, i: (bi, 0, 0)),
                  pl.BlockSpec((1, d), lambda bi, i: (0, 0)),
                  of_layer(d, w_in_bf.shape[-1]), of_layer(A_WIDTH, d), of_layer(B_WIDTH, d),
                  of_layer(C_WIDTH, d), of_layer(d, d)],
        out_specs=row(d),
        out_shape=jax.ShapeDtypeStruct((b, n, d), _F32),
        compiler_params=_params(2),
        name="merge",
    )(oa, ob, gb, oc, x, mod, gn, w_in_bf, wa, wb, wc, wo)


def _rope_tables(n):
    pos = jnp.arange(n, dtype=jnp.int32)
    row = (pos // GRID_W).astype(_F32)
    col = (pos % GRID_W).astype(_F32)
    freqs = ROPE_BASE ** (-jnp.arange(0, AXIS_DIM, 2, dtype=_F32) / AXIS_DIM)
    ang_r = row[:, None] * freqs
    ang_c = col[:, None] * freqs
    ang = jnp.concatenate([ang_r, ang_r, ang_c, ang_c], axis=-1)
    cos, sin = jnp.cos(ang), jnp.sin(ang)
    first_half = (jnp.arange(A_QK_DIM) % AXIS_DIM) < AXIS_DIM // 2
    sa = jnp.where(first_half, -sin, 0.0)
    sb = jnp.where(first_half, 0.0, sin)
    tile2 = lambda t: jnp.concatenate([t, t], axis=-1)
    return tile2(cos), tile2(sa), tile2(sb)


def _no_rope_tables(n):
    return (jnp.ones((n, V7X_LANES), _F32), jnp.zeros((n, V7X_LANES), _F32), jnp.zeros((n, V7X_LANES), _F32))


def kernel(x, c, ctx, c_ctx, w_ada, b_ada, g_norm, w_in, g_q, g_k, lam_q1, lam_k1, lam_q2, lam_k2, g_sub,
           w_f, b_f, ln_g, ln_b, w_s, b_s, w_br_a, w_br_b, w_br_c, w_out):
    b, n, d = x.shape
    n_ctx = ctx.shape[1]
    depth = w_in.shape[0]
    tm_lat = min(n, 512)
    tm_ctx = min(n_ctx, 512)

    n_rows = V7X_SUBLANES * pl.cdiv(b + 1, V7X_SUBLANES)
    rows = jnp.concatenate([c, c_ctx[None, :], jnp.zeros((n_rows - b - 1, d), _F32)], axis=0)
    mod_all = _mod_call(rows, w_ada, b_ada)

    rope_lat = _rope_tables(n)
    rope_ctx = _no_rope_tables(n_ctx)
    group_mean = np.kron(np.eye(V7X_LANES // A_QK_DIM), np.full((A_QK_DIM, A_QK_DIM), 1.0 / A_QK_DIM))
    bmat = jnp.asarray(group_mean, _BF16)

    w_in_bf = w_in.astype(_BF16)
    w_s_bf = w_s.astype(_BF16)
    merge_w = (w_in_bf, w_br_a.astype(_BF16), w_br_b.astype(_BF16), w_br_c.astype(_BF16), w_out.astype(_BF16))

    for l in range(depth):
        last = l == depth - 1
        lam_init = 0.8 - 0.6 * math.exp(-0.3 * l)
        lamv = jnp.stack([lam_q1[l], lam_k1[l], lam_q2[l], lam_k2[l]], axis=0)
        shared = dict(
            gn=g_norm[l][None, :], w_bf=w_in_bf,
            gq=jnp.tile(g_q[l], 2)[None, :], gk=jnp.tile(g_k[l], 2)[None, :], bmat=bmat,
            lng=ln_g[l][None, :], lnb=ln_b[l][None, :], ws_bf=w_s_bf,
            bsm=jnp.repeat(b_s[l].T, C_GROUP_DIM, axis=1), layer=l)
        gsub = g_sub[l][None, :]
        wf_bd = jax.scipy.linalg.block_diag(*[w_f[l, gi] for gi in range(B_GROUPS)]).astype(_BF16)
        bf_row = b_f[l].reshape(1, B_WIDTH)

        mod_lat = mod_all[l, :b][:, None, :]
        mod_ctx = jnp.broadcast_to(mod_all[l, b][None, None, :], (b, 1, 3 * d))

        cq2, ck, cvt, cf, coc, cga, cgb = _inproj_call(
            ctx, mod_ctx, cos=rope_ctx[0], sa=rope_ctx[1], sb=rope_ctx[2], tm=tm_ctx, sub=tm_ctx, **shared)
        q2, k, vt, f, oc, ga, gb = _inproj_call(
            x, mod_lat, cos=rope_lat[0], sa=rope_lat[1], sb=rope_lat[2], tm=min(n, 2 * tm_lat), sub=tm_lat,
            **shared)

        oa = _attention(lamv, q2, ga, gsub, [(ck, cvt), (k, vt)], g_q[l], g_k[l], tq=tm_lat, lam_init=lam_init)
        ob = _fourier(f, wf_bd, bf_row)
        x_new = _merge_call(oa, ob, gb, oc, x, mod_lat, shared["gn"], *merge_w, tm=min(n, 2 * tm_lat), sub=tm_lat,
                            layer=l)

        if not last:
            coa = _attention(lamv, cq2, cga, gsub, [(ck, cvt)], g_q[l], g_k[l], tq=tm_ctx, lam_init=lam_init)
            cob = _fourier(cf, wf_bd, bf_row)
            ctx = _merge_call(coa, cob, cgb, coc, ctx, mod_ctx, shared["gn"], *merge_w, tm=tm_ctx, sub=tm_ctx,
                              layer=l)
        x = x_new
    return x
```

```python
import functools
import math

import numpy as np
import jax
import jax.numpy as jnp
from jax import lax
from jax.experimental import pallas as pl
from jax.experimental.pallas import tpu as pltpu

A_HEADS = 4
A_QK_DIM = 64
A_V_DIM = 2 * A_QK_DIM
A_WIDTH = A_HEADS * A_V_DIM
B_GROUPS = 4
B_GROUP_DIM = 64
B_WIDTH = B_GROUPS * B_GROUP_DIM
C_GROUPS = 4
C_GROUP_DIM = 64
C_WIDTH = C_GROUPS * C_GROUP_DIM
CHUNK = 128
GRID_W = 64
ROPE_BASE = 10000.0
AXIS_DIM = A_QK_DIM // 2
EPS = 1e-6

OFF_Q = 0
OFF_K = OFF_Q + A_WIDTH
OFF_V = OFF_K + A_WIDTH
OFF_F = OFF_V + A_WIDTH
OFF_U = OFF_F + B_WIDTH
OFF_VC = OFF_U + C_WIDTH
OFF_GATE = OFF_VC + C_WIDTH
GATE_WIDTH = A_WIDTH + B_WIDTH + C_WIDTH
OFF_MERGE = OFF_GATE + GATE_WIDTH

V7X_LANES = 128
V7X_SUBLANES = 8
V7X_BF16_SUBLANES = 16
V7X_MXU_DIM = 256
V7X_VMEM_LIMIT_BYTES = 56 * 1024 * 1024

LOG2E = math.log2(math.e)
Q_SCALE = (A_QK_DIM ** -0.5) * LOG2E

_F32 = jnp.float32
_BF16 = jnp.bfloat16
_NT = (((1,), (1,)), ((), ()))


def _sigmoid(x):
    return 1.0 / (1.0 + jnp.exp(-x))


def _params(n_axes):
    return pltpu.CompilerParams(dimension_semantics=("arbitrary",) * n_axes,
                                vmem_limit_bytes=V7X_VMEM_LIMIT_BYTES)


def _mod_kernel(c_ref, w_ref, b_ref, o_ref):
    cc = c_ref[...]
    s = cc * _sigmoid(cc)
    o_ref[0] = jnp.dot(s, w_ref[0], preferred_element_type=_F32) + b_ref[0]


def _mod_call(rows, w_ada, b_ada):
    depth, d, d3 = w_ada.shape
    nb = d3 // d
    return pl.pallas_call(
        _mod_kernel,
        grid=(depth, nb),
        in_specs=[pl.BlockSpec((rows.shape[0], d), lambda l, j: (0, 0)),
                  pl.BlockSpec((1, d, d), lambda l, j: (l, 0, j)),
                  pl.BlockSpec((1, 1, d), lambda l, j: (l, 0, j))],
        out_specs=pl.BlockSpec((1, rows.shape[0], d), lambda l, j: (l, 0, j)),
        out_shape=jax.ShapeDtypeStruct((depth, rows.shape[0], d3), _F32),
        compiler_params=_params(2),
        name="adaln_mod",
    )(rows, w_ada, b_ada.reshape(depth, 1, d3))


def _modulated_norm(x, mod, gn, d):
    ms = jnp.mean(x * x, axis=-1, keepdims=True)
    h = (x * lax.rsqrt(ms + EPS) * gn) * (1.0 + mod[:, d:2 * d]) + mod[:, :d]
    return h.astype(_BF16)


def _inproj_kernel(x_ref, mod_ref, gn_ref, w_ref, gq_ref, gk_ref, cos_ref, sa_ref, sb_ref,
                   bmat_ref, lng_ref, lnb_ref, ws_ref, bsm_ref,
                   q2_ref, k_ref, vt_ref, f_ref, oc_ref, ga_ref, gb_ref, *, sub, d):
    bmat = bmat_ref[...]
    lo = lax.broadcasted_iota(jnp.int32, (sub, V7X_LANES), 1) < A_QK_DIM
    lo_c = lax.broadcasted_iota(jnp.int32, (CHUNK, V7X_LANES), 1) < C_GROUP_DIM
    bsm = bsm_ref[...]
    gq = gq_ref[...]
    gk = gk_ref[...]

    for si in range(x_ref.shape[1] // sub):
        rows = slice(si * sub, (si + 1) * sub)
        hb = _modulated_norm(x_ref[0, rows, :], mod_ref[0], gn_ref[...], d)

        def proj(off, width, hb=hb):
            return jnp.dot(hb, w_ref[:, off:off + width], preferred_element_type=_F32)

        cos = cos_ref[rows, :]
        sa = sa_ref[rows, :]
        sb = sb_ref[rows, :]

        def qk_head(p_h, g, cos=cos, sa=sa, sb=sb):
            msq = jnp.dot((p_h * p_h).astype(_BF16), bmat, preferred_element_type=_F32)
            t = p_h * lax.rsqrt(msq + EPS) * g
            return (t * cos + pltpu.roll(t, V7X_LANES - AXIS_DIM // 2, 1) * sa
                    + pltpu.roll(t, AXIS_DIM // 2, 1) * sb)

        pq = proj(OFF_Q, A_WIDTH)
        for hh in range(A_HEADS):
            t = qk_head(pq[:, hh * A_V_DIM:(hh + 1) * A_V_DIM], gq) * Q_SCALE
            q2_ref[0, hh, 0, rows, :] = jnp.where(lo, t, 0.0).astype(_BF16)
            q2_ref[0, hh, 1, rows, :] = jnp.where(lo, 0.0, t).astype(_BF16)

        pk = proj(OFF_K, A_WIDTH)
        for hh in range(A_HEADS):
            k_ref[0, hh, rows, :] = qk_head(pk[:, hh * A_V_DIM:(hh + 1) * A_V_DIM], gk).astype(_BF16)

        pv = proj(OFF_V, A_WIDTH)
        for hh in range(A_HEADS):
            vt_ref[0, hh, si] = pv[:, hh * A_V_DIM:(hh + 1) * A_V_DIM].T.astype(_BF16)

        f_ref[0, rows, :] = proj(OFF_F, B_WIDTH)

        u = proj(OFF_U, C_WIDTH)
        vc = proj(OFF_VC, C_WIDTH)
        mu = jnp.mean(vc, axis=-1, keepdims=True)
        dv = vc - mu
        var = jnp.mean(dv * dv, axis=-1, keepdims=True)
        vn = (dv * lax.rsqrt(var + EPS) * lng_ref[...] + lnb_ref[...]).astype(_BF16)

        pg = proj(OFF_GATE, GATE_WIDTH)
        gate = pg * _sigmoid(pg)
        ga_ref[0, rows, :] = gate[:, :A_WIDTH].astype(_BF16)
        gb_ref[0, rows, :] = gate[:, A_WIDTH:A_WIDTH + B_WIDTH].astype(_BF16)
        gate_c = gate[:, A_WIDTH + B_WIDTH:]

        for t in range(sub // CHUNK):
            crows = slice(t * CHUNK, (t + 1) * CHUNK)
            vl = vn[crows, :V7X_LANES]
            vr = vn[crows, V7X_LANES:]
            s_l = jnp.where(lo_c, jnp.dot(ws_ref[0], vl, preferred_element_type=_F32),
                            jnp.dot(ws_ref[1], vl, preferred_element_type=_F32))
            s_r = jnp.where(lo_c, jnp.dot(ws_ref[2], vr, preferred_element_type=_F32),
                            jnp.dot(ws_ref[3], vr, preferred_element_type=_F32))
            s = jnp.concatenate([s_l, s_r], axis=1) + bsm
            oc_ref[0, si * sub + t * CHUNK:si * sub + (t + 1) * CHUNK, :] = (
                u[crows] * s * gate_c[crows]).astype(_BF16)


def _inproj_call(x, mod, gn, w_bf, gq, gk, cos, sa, sb, bmat, lng, lnb, ws_bf, bsm, *, tm, sub, layer):
    b, n, d = x.shape
    nt = n // tm
    in_w = w_bf.shape[-1]
    full = lambda *shape: pl.BlockSpec(shape, lambda bi, i: (0,) * len(shape))
    of_layer = lambda *shape: pl.BlockSpec((None,) + shape, lambda bi, i: (layer,) + (0,) * len(shape))
    in_specs = [
        pl.BlockSpec((1, tm, d), lambda bi, i: (bi, i, 0)),
        pl.BlockSpec((1, 1, 3 * d), lambda bi, i: (bi, 0, 0)),
        full(1, d),
        of_layer(d, in_w),
        full(1, V7X_LANES),
        full(1, V7X_LANES),
        pl.BlockSpec((tm, V7X_LANES), lambda bi, i: (i, 0)),
        pl.BlockSpec((tm, V7X_LANES), lambda bi, i: (i, 0)),
        pl.BlockSpec((tm, V7X_LANES), lambda bi, i: (i, 0)),
        full(V7X_LANES, V7X_LANES),
        full(1, C_WIDTH),
        full(1, C_WIDTH),
        of_layer(C_GROUPS, CHUNK, CHUNK),
        full(CHUNK, C_WIDTH),
    ]
    out_shape = [
        jax.ShapeDtypeStruct((b, A_HEADS, 2, n, A_V_DIM), _BF16),
        jax.ShapeDtypeStruct((b, A_HEADS, n, A_V_DIM), _BF16),
        jax.ShapeDtypeStruct((b, A_HEADS, n // sub, A_V_DIM, sub), _BF16),
        jax.ShapeDtypeStruct((b, n, B_WIDTH), _F32),
        jax.ShapeDtypeStruct((b, n, C_WIDTH), _BF16),
        jax.ShapeDtypeStruct((b, n, A_WIDTH), _BF16),
        jax.ShapeDtypeStruct((b, n, B_WIDTH), _BF16),
    ]
    out_specs = [
        pl.BlockSpec((1, A_HEADS, 2, tm, A_V_DIM), lambda bi, i: (bi, 0, 0, i, 0)),
        pl.BlockSpec((1, A_HEADS, tm, A_V_DIM), lambda bi, i: (bi, 0, i, 0)),
        pl.BlockSpec((1, A_HEADS, tm // sub, A_V_DIM, sub), lambda bi, i: (bi, 0, i, 0, 0)),
        pl.BlockSpec((1, tm, B_WIDTH), lambda bi, i: (bi, i, 0)),
        pl.BlockSpec((1, tm, C_WIDTH), lambda bi, i: (bi, i, 0)),
        pl.BlockSpec((1, tm, A_WIDTH), lambda bi, i: (bi, i, 0)),
        pl.BlockSpec((1, tm, B_WIDTH), lambda bi, i: (bi, i, 0)),
    ]
    return pl.pallas_call(
        functools.partial(_inproj_kernel, sub=sub, d=d),
        grid=(b, nt),
        in_specs=in_specs,
        out_specs=out_specs,
        out_shape=out_shape,
        compiler_params=_params(2),
        name="inproj",
    )(x, mod, gn, w_bf, gq, gk, cos, sa, sb, bmat, lng, lnb, ws_bf, bsm)


def _attn_kernel(*refs, tq, nq, n_src, tiles, lam_init, bounded):
    lam_ref, q2_ref, ga_ref, gsub_ref = refs[:4]
    src_refs = refs[4:4 + 2 * n_src]
    out_ref = refs[4 + 2 * n_src]
    p_sc, m_sc, l_sc, acc_sc = refs[5 + 2 * n_src:]

    def queries(u):
        rows = slice(u * tq, (u + 1) * tq)
        return jnp.concatenate([q2_ref[0, 0, 0, rows, :], q2_ref[0, 0, 1, rows, :]], axis=0)

    l_sc[...] = jnp.zeros(l_sc.shape, _F32)
    acc_sc[...] = jnp.zeros(acc_sc.shape, _F32)
    lv = lam_ref[...]
    lam = (jnp.exp(jnp.sum(lv[0:1] * lv[1:2], axis=-1, keepdims=True))
           - jnp.exp(jnp.sum(lv[2:3] * lv[3:4], axis=-1, keepdims=True)) + lam_init)

    def finalize(u):
        inv = 1.0 / l_sc[u]
        acc = acc_sc[u]
        o = acc[:, :tq] * inv[:, :tq] - lam * (acc[:, tq:] * inv[:, tq:])
        ms = jnp.mean(o * o, axis=0, keepdims=True)
        on = o * lax.rsqrt(ms + EPS)
        rows = slice(u * tq, (u + 1) * tq)
        out = on.T * (gsub_ref[...] * (1.0 - lam_init)) * ga_ref[0, rows, :].astype(_F32)
        out_ref[0, rows, :] = out.astype(_BF16)

    def tile_len(si):
        return src_refs[2 * si + 1].shape[-1]

    def k_tile(si, t):
        tk = tile_len(si)
        if isinstance(t, int):
            return src_refs[2 * si][0, 0, t * tk:(t + 1) * tk, :]
        return src_refs[2 * si][0, 0, pl.ds(pl.multiple_of(t * tk, tk), tk), :]

    def vt_tile(si, t):
        return src_refs[2 * si + 1][0, 0, t]

    if bounded:
        def exponentials(slot, u, k_t):
            p = jnp.exp2(lax.dot_general(k_t, queries(u), _NT, preferred_element_type=_F32))
            l_sc[u] += jnp.sum(p, axis=0, keepdims=True)
            p_sc[slot, :k_t.shape[0], :] = p.astype(_BF16)

        def weighted_values(slot, u, vt_t):
            acc_sc[u] += jnp.dot(vt_t, p_sc[slot, :vt_t.shape[1], :], preferred_element_type=_F32)

        stages = [(u, si, t) for u in range(nq) for si in range(n_src) for t in range(tiles[si])]
        exponentials(0, stages[0][0], k_tile(*stages[0][1:]))
        for j, (u, si, t) in enumerate(stages):
            if j + 1 < len(stages):
                exponentials((j + 1) % 2, stages[j + 1][0], k_tile(*stages[j + 1][1:]))
            weighted_values(j % 2, u, vt_tile(si, t))
            if j + 1 == len(stages) or stages[j + 1][0] != u:
                finalize(u)
    else:
        for u in range(nq):
            m_sc[...] = jnp.full(m_sc.shape, -jnp.inf, _F32)

            def step(k_t, vt_t, u=u):
                s = lax.dot_general(k_t, queries(u), _NT, preferred_element_type=_F32)
                m_prev = m_sc[...]
                m_new = jnp.maximum(m_prev, jnp.max(s, axis=0, keepdims=True))
                alpha = jnp.exp2(m_prev - m_new)
                p = jnp.exp2(s - m_new)
                l_sc[u] = alpha * l_sc[u] + jnp.sum(p, axis=0, keepdims=True)
                acc_sc[u] = alpha * acc_sc[u] + jnp.dot(vt_t, p.astype(_BF16), preferred_element_type=_F32)
                m_sc[...] = m_new

            for si in range(n_src):
                if tiles[si] == 1:
                    step(k_tile(si, 0), vt_tile(si, 0))
                else:
                    def body(t, carry, si=si, step=step):
                        step(k_tile(si, t), vt_tile(si, t))
                        return carry
                    lax.fori_loop(0, tiles[si], body, 0)
            finalize(u)


def _attn_call(lamv, q2, ga, gsub, sources, *, tq, nq, lam_init, bounded):
    b, _, _, n, _ = q2.shape
    rows = nq * tq
    in_specs = [
        pl.BlockSpec(lamv.shape, lambda bi, hi, i: (0, 0)),
        pl.BlockSpec((1, 1, 2, rows, A_V_DIM), lambda bi, hi, i: (bi, hi, 0, i, 0)),
        pl.BlockSpec((1, rows, A_V_DIM), lambda bi, hi, i: (bi, i, hi)),
        pl.BlockSpec((1, A_V_DIM), lambda bi, hi, i: (0, 0)),
    ]
    args = [lamv, q2, ga, gsub]
    tiles = []
    for k_s, vt_s in sources:
        ls = k_s.shape[2]
        nt, _, tk = vt_s.shape[2:]
        tiles.append(nt)
        in_specs.append(pl.BlockSpec((1, 1, ls, A_V_DIM), lambda bi, hi, i: (bi, hi, 0, 0)))
        in_specs.append(pl.BlockSpec((1, 1, nt, A_V_DIM, tk), lambda bi, hi, i: (bi, hi, 0, 0, 0)))
        args += [k_s, vt_s]
    tk_max = max(vt_s.shape[-1] for _, vt_s in sources)
    return pl.pallas_call(
        functools.partial(_attn_kernel, tq=tq, nq=nq, n_src=len(sources), tiles=tuple(tiles),
                          lam_init=lam_init, bounded=bounded),
        grid=(b, A_HEADS, n // rows),
        in_specs=in_specs,
        out_specs=pl.BlockSpec((1, rows, A_V_DIM), lambda bi, hi, i: (bi, i, hi)),
        out_shape=jax.ShapeDtypeStruct((b, n, A_WIDTH), _BF16),
        scratch_shapes=[pltpu.VMEM((2, tk_max, 2 * tq), _BF16),
                        pltpu.VMEM((1, 2 * tq), _F32),
                        pltpu.VMEM((nq, 1, 2 * tq), _F32),
                        pltpu.VMEM((nq, A_V_DIM, 2 * tq), _F32)],
        compiler_params=_params(3),
        name="diff_attn" if bounded else "diff_attn_general",
    )(*args)


SAFE_EXP2_RANGE = 60.0


def _attention(lamv, q2, ga, gsub, sources, gq, gk, *, tq, nq, lam_init):
    bound = A_QK_DIM * Q_SCALE * jnp.max(jnp.abs(gq)) * jnp.max(jnp.abs(gk))
    call = lambda bounded: functools.partial(_attn_call, tq=tq, nq=nq, lam_init=lam_init, bounded=bounded)
    return lax.cond(bound <= SAFE_EXP2_RANGE, call(True), call(False), lamv, q2, ga, gsub, sources)


def _dft1_kernel(f1_ref, x_ref, o_ref, *, rows_per_step):
    n1 = x_ref.shape[1]
    x = x_ref[0].reshape(n1, rows_per_step * B_WIDTH).astype(_BF16)
    z = jnp.dot(f1_ref[...], x, preferred_element_type=_F32)
    o_ref[0] = z.reshape(2 * n1, rows_per_step, B_WIDTH).astype(_BF16)


def _dft1_call(f1, x4d, *, rows_per_step):
    b, n1, n2, _ = x4d.shape
    return pl.pallas_call(
        functools.partial(_dft1_kernel, rows_per_step=rows_per_step),
        grid=(b, n2 // rows_per_step),
        in_specs=[pl.BlockSpec((2 * n1, n1), lambda bi, j: (0, 0)),
                  pl.BlockSpec((1, n1, rows_per_step, B_WIDTH), lambda bi, j: (bi, 0, j, 0))],
        out_specs=pl.BlockSpec((1, 2 * n1, rows_per_step, B_WIDTH), lambda bi, j: (bi, 0, j, 0)),
        out_shape=jax.ShapeDtypeStruct((b, 2 * n1, n2, B_WIDTH), _BF16),
        compiler_params=_params(2),
        name="dft_stage1",
    )(f1, x4d)


def _dft2_kernel(*refs, g, n2, two_stage, norm):
    if two_stage:
        ar_ref, ai_ref, twc_ref, tws_ref, w2_ref, cbd_ref, sbd_ref, wf_ref, bf_ref, out_ref = refs
    else:
        ar_ref, w2_ref, cbd_ref, sbd_ref, wf_ref, bf_ref, out_ref = refs
    w2 = w2_ref[...]
    cbd = cbd_ref[...]
    sbd = sbd_ref[...]
    wf = wf_ref[...]
    bias = bf_ref[...]
    ys = []
    for j in range(g):
        ar = ar_ref[0, 0, j].astype(_F32)
        if two_stage:
            ai = ai_ref[0, 0, j].astype(_F32)
            tc = twc_ref[j]
            ts = tws_ref[j]
            c2 = jnp.concatenate([tc, tc], axis=1)
            s2 = jnp.concatenate([ts, ts], axis=1)
            a = jnp.concatenate([ar * c2 - ai * s2, ar * s2 + ai * c2], axis=0).astype(_BF16)
        else:
            a = ar.astype(_BF16)
        z = jnp.dot(w2, a, preferred_element_type=_F32)
        p = z[:n2].astype(_BF16)
        q = z[n2:].astype(_BF16)
        fr = (jnp.dot(p, cbd, preferred_element_type=_F32) - jnp.dot(q, sbd, preferred_element_type=_F32)) * norm
        ys.append(jnp.dot(fr.astype(_BF16), wf, preferred_element_type=_F32) + bias)
    if two_stage:
        out_ref[0] = jnp.swapaxes(jnp.stack(ys, axis=0), 0, 1).astype(_BF16)
    else:
        out_ref[0] = ys[0].astype(_BF16)


def _dft_tables(n1, n2):
    n = n1 * n2
    k1 = np.arange(n1)
    a1 = 2.0 * np.pi * ((k1[:, None] * k1[None, :]) % n1) / n1
    f1 = np.concatenate([np.cos(a1), np.sin(a1)], axis=0)
    k2 = np.arange(n2)
    a2 = 2.0 * np.pi * ((k2[:, None] * k2[None, :]) % n2) / n2
    c2, s2 = np.cos(a2), np.sin(a2)
    w2 = np.block([[c2, -s2], [s2, c2]]) if n1 > 1 else np.concatenate([c2, s2], axis=0)
    gch = np.arange(B_GROUP_DIM)
    ag = 2.0 * np.pi * ((gch[:, None] * gch[None, :]) % B_GROUP_DIM) / B_GROUP_DIM
    eye = np.eye(B_GROUPS)
    cbd = np.kron(eye, np.cos(ag))
    sbd = np.kron(eye, np.sin(ag))
    to_bf16 = lambda t: jnp.asarray(t, _F32).astype(_BF16)
    return to_bf16(f1), to_bf16(w2), to_bf16(cbd), to_bf16(sbd), 1.0 / math.sqrt(n * B_GROUP_DIM)


def _twiddles(n1, n2):
    k1 = jnp.arange(n1, dtype=jnp.int32)[:, None]
    m2 = jnp.arange(n2, dtype=jnp.int32)[None, :]
    ang = (k1 * m2).astype(_F32) * (2.0 * math.pi / (n1 * n2))
    shape = (n1, n2, V7X_LANES)
    return (jnp.broadcast_to(jnp.cos(ang)[:, :, None], shape), jnp.broadcast_to(jnp.sin(ang)[:, :, None], shape))


def _fourier(f, wf_bd, bf_row):
    b, n, _ = f.shape
    two_stage = n > 2 * V7X_LANES
    n2 = V7X_LANES if two_stage else n
    n1 = n // n2
    g = min(n1, V7X_BF16_SUBLANES)
    f1, w2, cbd, sbd, norm = _dft_tables(n1, n2)
    full = lambda *shape: pl.BlockSpec(shape, lambda bi, j: (0,) * len(shape))
    consts = [w2, cbd, sbd, wf_bd, bf_row]
    const_specs = [full(*w2.shape), full(B_WIDTH, B_WIDTH), full(B_WIDTH, B_WIDTH), full(B_WIDTH, B_WIDTH),
                   full(1, B_WIDTH)]
    if two_stage:
        a = _dft1_call(f1, f.reshape(b, n1, n2, B_WIDTH), rows_per_step=V7X_BF16_SUBLANES)
        a5 = a.reshape(b, 2, n1, n2, B_WIDTH)
        out_shape = jax.ShapeDtypeStruct((b, n2, n1, B_WIDTH), _BF16)
        out_spec = pl.BlockSpec((1, n2, g, B_WIDTH), lambda bi, j: (bi, 0, j, 0))
        twc, tws = _twiddles(n1, n2)
        args = [a5, a5, twc, tws] + consts
        in_specs = [pl.BlockSpec((1, 1, g, n2, B_WIDTH), lambda bi, j: (bi, 0, j, 0, 0)),
                    pl.BlockSpec((1, 1, g, n2, B_WIDTH), lambda bi, j: (bi, 1, j, 0, 0)),
                    pl.BlockSpec((g, n2, V7X_LANES), lambda bi, j: (j, 0, 0)),
                    pl.BlockSpec((g, n2, V7X_LANES), lambda bi, j: (j, 0, 0))] + const_specs
    else:
        args = [f.reshape(b, 1, 1, n2, B_WIDTH)] + consts
        in_specs = [pl.BlockSpec((1, 1, 1, n2, B_WIDTH), lambda bi, j: (bi, 0, 0, 0, 0))] + const_specs
        out_shape = jax.ShapeDtypeStruct((b, n2, B_WIDTH), _BF16)
        out_spec = pl.BlockSpec((1, n2, B_WIDTH), lambda bi, j: (bi, 0, 0))
    out = pl.pallas_call(
        functools.partial(_dft2_kernel, g=g, n2=n2, two_stage=two_stage, norm=norm),
        grid=(b, n1 // g),
        in_specs=in_specs,
        out_specs=out_spec,
        out_shape=out_shape,
        compiler_params=_params(2),
        name="dft_stage2",
    )(*args)
    return out.reshape(b, n, B_WIDTH)


def _merge_kernel(oa_ref, ob_ref, gb_ref, oc_ref, x_ref, mod_ref, gn_ref, wm_ref, wa_ref, wb_ref, wc_ref, wo_ref,
                  o_ref, *, d, sub):
    mod = mod_ref[0]
    for r0 in range(0, x_ref.shape[1], sub):
        rows = slice(r0, r0 + sub)
        x = x_ref[0, rows, :]
        hb = _modulated_norm(x, mod, gn_ref[...], d)
        branches = (jnp.dot(oa_ref[0, rows, :], wa_ref[...], preferred_element_type=_F32),
                    jnp.dot(ob_ref[0, rows, :] * gb_ref[0, rows, :], wb_ref[...], preferred_element_type=_F32),
                    jnp.dot(oc_ref[0, rows, :], wc_ref[...], preferred_element_type=_F32))
        y = None
        for j, yj in enumerate(branches):
            mj = _sigmoid(jnp.dot(hb, wm_ref[:, OFF_MERGE + j * d:OFF_MERGE + (j + 1) * d],
                                  preferred_element_type=_F32))
            y = mj * yj if y is None else y + mj * yj
        out = jnp.dot(y.astype(_BF16), wo_ref[...], preferred_element_type=_F32)
        o_ref[0, rows, :] = x + mod[:, 2 * d:] * out


def _merge_call(oa, ob, gb, oc, x, mod, gn, w_in_bf, wa, wb, wc, wo, *, tm, sub, layer):
    b, n, d = x.shape
    row = lambda w: pl.BlockSpec((1, tm, w), lambda bi, i: (bi, i, 0))
    of_layer = lambda *shape: pl.BlockSpec((None,) + shape, lambda bi, i: (layer,) + (0,) * len(shape))
    return pl.pallas_call(
        functools.partial(_merge_kernel, d=d, sub=sub),
        grid=(b, n // tm),
        in_specs=[row(A_WIDTH), row(B_WIDTH), row(B_WIDTH), row(C_WIDTH), row(d),
                  pl.BlockSpec((1, 1, 3 * d), lambda bi, i: (bi, 0, 0)),
                  pl.BlockSpec((1, d), lambda bi, i: (0, 0)),
                  of_layer(d, w_in_bf.shape[-1]), of_layer(A_WIDTH, d), of_layer(B_WIDTH, d),
                  of_layer(C_WIDTH, d), of_layer(d, d)],
        out_specs=row(d),
        out_shape=jax.ShapeDtypeStruct((b, n, d), _F32),
        compiler_params=_params(2),
        name="merge",
    )(oa, ob, gb, oc, x, mod, gn, w_in_bf, wa, wb, wc, wo)


def _rope_tables(n):
    n_rows = n // GRID_W
    freqs = ROPE_BASE ** (-jnp.arange(0, AXIS_DIM, 2, dtype=_F32) / AXIS_DIM)
    ang_r = jnp.arange(n_rows, dtype=_F32)[:, None] * freqs
    ang_c = jnp.arange(GRID_W, dtype=_F32)[:, None] * freqs

    def on_grid(fn):
        shape = (n_rows, GRID_W, AXIS_DIM // 2)
        r = jnp.broadcast_to(fn(ang_r)[:, None, :], shape)
        c = jnp.broadcast_to(fn(ang_c)[None, :, :], shape)
        return jnp.concatenate([r, r, c, c], axis=-1).reshape(n, A_QK_DIM)

    cos, sin = on_grid(jnp.cos), on_grid(jnp.sin)
    first_half = (jnp.arange(A_QK_DIM) % AXIS_DIM) < AXIS_DIM // 2
    sa = jnp.where(first_half, -sin, 0.0)
    sb = jnp.where(first_half, 0.0, sin)
    tile2 = lambda t: jnp.concatenate([t, t], axis=-1)
    return tile2(cos), tile2(sa), tile2(sb)


def _no_rope_tables(n):
    return (jnp.ones((n, V7X_LANES), _F32), jnp.zeros((n, V7X_LANES), _F32), jnp.zeros((n, V7X_LANES), _F32))


def kernel(x, c, ctx, c_ctx, w_ada, b_ada, g_norm, w_in, g_q, g_k, lam_q1, lam_k1, lam_q2, lam_k2, g_sub,
           w_f, b_f, ln_g, ln_b, w_s, b_s, w_br_a, w_br_b, w_br_c, w_out):
    b, n, d = x.shape
    n_ctx = ctx.shape[1]
    depth = w_in.shape[0]
    tm_lat = min(n, 512)
    tm_ctx = min(n_ctx, 512)

    n_rows = V7X_SUBLANES * pl.cdiv(b + 1, V7X_SUBLANES)
    rows = jnp.concatenate([c, c_ctx[None, :], jnp.zeros((n_rows - b - 1, d), _F32)], axis=0)
    mod_all = _mod_call(rows, w_ada, b_ada)

    rope_lat = _rope_tables(n)
    rope_ctx = _no_rope_tables(n_ctx)
    group_mean = np.kron(np.eye(V7X_LANES // A_QK_DIM), np.full((A_QK_DIM, A_QK_DIM), 1.0 / A_QK_DIM))
    bmat = jnp.asarray(group_mean, _BF16)

    w_in_bf = w_in.astype(_BF16)
    w_s_bf = w_s.astype(_BF16)
    merge_w = (w_in_bf, w_br_a.astype(_BF16), w_br_b.astype(_BF16), w_br_c.astype(_BF16), w_out.astype(_BF16))

    for l in range(depth):
        last = l == depth - 1
        lam_init = 0.8 - 0.6 * math.exp(-0.3 * l)
        lamv = jnp.stack([lam_q1[l], lam_k1[l], lam_q2[l], lam_k2[l]], axis=0)
        shared = dict(
            gn=g_norm[l][None, :], w_bf=w_in_bf,
            gq=jnp.tile(g_q[l], 2)[None, :], gk=jnp.tile(g_k[l], 2)[None, :], bmat=bmat,
            lng=ln_g[l][None, :], lnb=ln_b[l][None, :], ws_bf=w_s_bf,
            bsm=jnp.repeat(b_s[l].T, C_GROUP_DIM, axis=1), layer=l)
        gsub = g_sub[l][None, :]
        wf_bd = jax.scipy.linalg.block_diag(*[w_f[l, gi] for gi in range(B_GROUPS)]).astype(_BF16)
        bf_row = b_f[l].reshape(1, B_WIDTH)

        mod_lat = mod_all[l, :b][:, None, :]
        mod_ctx = jnp.broadcast_to(mod_all[l, b][None, None, :], (b, 1, 3 * d))

        cq2, ck, cvt, cf, coc, cga, cgb = _inproj_call(
            ctx, mod_ctx, cos=rope_ctx[0], sa=rope_ctx[1], sb=rope_ctx[2], tm=tm_ctx, sub=tm_ctx, **shared)
        q2, k, vt, f, oc, ga, gb = _inproj_call(
            x, mod_lat, cos=rope_lat[0], sa=rope_lat[1], sb=rope_lat[2], tm=min(n, 2 * tm_lat), sub=tm_lat,
            **shared)

        oa = _attention(lamv, q2, ga, gsub, [(ck, cvt), (k, vt)], g_q[l], g_k[l], tq=tm_lat,
                        nq=min(4, n // tm_lat), lam_init=lam_init)
        ob = _fourier(f, wf_bd, bf_row)
        x_new = _merge_call(oa, ob, gb, oc, x, mod_lat, shared["gn"], *merge_w, tm=min(n, 2 * tm_lat), sub=tm_lat,
                            layer=l)

        if not last:
            coa = _attention(lamv, cq2, cga, gsub, [(ck, cvt)], g_q[l], g_k[l], tq=tm_ctx, nq=1,
                             lam_init=lam_init)
            cob = _fourier(cf, wf_bd, bf_row)
            ctx = _merge_call(coa, cob, cgb, coc, ctx, mod_ctx, shared["gn"], *merge_w, tm=tm_ctx, sub=tm_ctx,
                              layer=l)
        x = x_new
    return x
```

```python
import functools
import math

import numpy as np
import jax
import jax.numpy as jnp
from jax import lax
from jax.experimental import pallas as pl
from jax.experimental.pallas import tpu as pltpu

A_HEADS = 4
A_QK_DIM = 64
A_V_DIM = 2 * A_QK_DIM
A_WIDTH = A_HEADS * A_V_DIM
B_GROUPS = 4
B_GROUP_DIM = 64
B_WIDTH = B_GROUPS * B_GROUP_DIM
C_GROUPS = 4
C_GROUP_DIM = 64
C_WIDTH = C_GROUPS * C_GROUP_DIM
CHUNK = 128
GRID_W = 64
ROPE_BASE = 10000.0
AXIS_DIM = A_QK_DIM // 2
EPS = 1e-6

OFF_Q = 0
OFF_K = OFF_Q + A_WIDTH
OFF_V = OFF_K + A_WIDTH
OFF_F = OFF_V + A_WIDTH
OFF_U = OFF_F + B_WIDTH
OFF_VC = OFF_U + C_WIDTH
OFF_GATE = OFF_VC + C_WIDTH
GATE_WIDTH = A_WIDTH + B_WIDTH + C_WIDTH
OFF_MERGE = OFF_GATE + GATE_WIDTH

V7X_LANES = 128
V7X_SUBLANES = 8
V7X_BF16_SUBLANES = 16
V7X_MXU_DIM = 256
V7X_VMEM_LIMIT_BYTES = 56 * 1024 * 1024

LOG2E = math.log2(math.e)
Q_SCALE = (A_QK_DIM ** -0.5) * LOG2E

_F32 = jnp.float32
_BF16 = jnp.bfloat16
_NT = (((1,), (1,)), ((), ()))


def _sigmoid(x):
    return 1.0 / (1.0 + jnp.exp(-x))


def _params(n_axes):
    return pltpu.CompilerParams(dimension_semantics=("arbitrary",) * n_axes,
                                vmem_limit_bytes=V7X_VMEM_LIMIT_BYTES)


def _mod_kernel(c_ref, w_ref, b_ref, o_ref):
    cc = c_ref[...]
    s = cc * _sigmoid(cc)
    o_ref[0] = jnp.dot(s, w_ref[0], preferred_element_type=_F32) + b_ref[0]


def _mod_call(rows, w_ada, b_ada):
    depth, d, d3 = w_ada.shape
    nb = d3 // d
    return pl.pallas_call(
        _mod_kernel,
        grid=(depth, nb),
        in_specs=[pl.BlockSpec((rows.shape[0], d), lambda l, j: (0, 0)),
                  pl.BlockSpec((1, d, d), lambda l, j: (l, 0, j)),
                  pl.BlockSpec((1, 1, d), lambda l, j: (l, 0, j))],
        out_specs=pl.BlockSpec((1, rows.shape[0], d), lambda l, j: (l, 0, j)),
        out_shape=jax.ShapeDtypeStruct((depth, rows.shape[0], d3), _F32),
        compiler_params=_params(2),
        name="adaln_mod",
    )(rows, w_ada, b_ada.reshape(depth, 1, d3))


def _modulated_norm(x, mod, gn, d):
    ms = jnp.mean(x * x, axis=-1, keepdims=True)
    h = (x * lax.rsqrt(ms + EPS) * gn) * (1.0 + mod[:, d:2 * d]) + mod[:, :d]
    return h.astype(_BF16)


def _inproj_kernel(x_ref, mod_ref, gn_ref, w_ref, gq_ref, gk_ref, cos_ref, sa_ref, sb_ref,
                   bmat_ref, lng_ref, lnb_ref, ws_ref, bsm_ref,
                   q2_ref, k_ref, vt_ref, f_ref, oc_ref, ga_ref, gb_ref, *, sub, d):
    bmat = bmat_ref[...]
    lo = lax.broadcasted_iota(jnp.int32, (sub, V7X_LANES), 1) < A_QK_DIM
    lo_c = lax.broadcasted_iota(jnp.int32, (CHUNK, V7X_LANES), 1) < C_GROUP_DIM
    bsm = bsm_ref[...]
    gq = gq_ref[...]
    gk = gk_ref[...]

    for si in range(x_ref.shape[1] // sub):
        rows = slice(si * sub, (si + 1) * sub)
        hb = _modulated_norm(x_ref[0, rows, :], mod_ref[0], gn_ref[...], d)

        def proj(off, width, hb=hb):
            return jnp.dot(hb, w_ref[:, off:off + width], preferred_element_type=_F32)

        cos = cos_ref[rows, :]
        sa = sa_ref[rows, :]
        sb = sb_ref[rows, :]

        def qk_head(p_h, g, cos=cos, sa=sa, sb=sb):
            msq = jnp.dot((p_h * p_h).astype(_BF16), bmat, preferred_element_type=_F32)
            t = p_h * lax.rsqrt(msq + EPS) * g
            return (t * cos + pltpu.roll(t, V7X_LANES - AXIS_DIM // 2, 1) * sa
                    + pltpu.roll(t, AXIS_DIM // 2, 1) * sb)

        pq = proj(OFF_Q, A_WIDTH)
        for hh in range(A_HEADS):
            t = qk_head(pq[:, hh * A_V_DIM:(hh + 1) * A_V_DIM], gq) * Q_SCALE
            q2_ref[0, hh, 0, rows, :] = jnp.where(lo, t, 0.0).astype(_BF16)
            q2_ref[0, hh, 1, rows, :] = jnp.where(lo, 0.0, t).astype(_BF16)

        pk = proj(OFF_K, A_WIDTH)
        for hh in range(A_HEADS):
            k_ref[0, hh, rows, :] = qk_head(pk[:, hh * A_V_DIM:(hh + 1) * A_V_DIM], gk).astype(_BF16)

        pv = proj(OFF_V, A_WIDTH)
        for hh in range(A_HEADS):
            vt_ref[0, hh, si] = pv[:, hh * A_V_DIM:(hh + 1) * A_V_DIM].T.astype(_BF16)

        f_ref[0, rows, :] = proj(OFF_F, B_WIDTH)

        u = proj(OFF_U, C_WIDTH)
        vc = proj(OFF_VC, C_WIDTH)
        mu = jnp.mean(vc, axis=-1, keepdims=True)
        dv = vc - mu
        var = jnp.mean(dv * dv, axis=-1, keepdims=True)
        vn = (dv * lax.rsqrt(var + EPS) * lng_ref[...] + lnb_ref[...]).astype(_BF16)

        pg = proj(OFF_GATE, GATE_WIDTH)
        gate = pg * _sigmoid(pg)
        ga_ref[0, rows, :] = gate[:, :A_WIDTH].astype(_BF16)
        gb_ref[0, rows, :] = gate[:, A_WIDTH:A_WIDTH + B_WIDTH].astype(_BF16)
        gate_c = gate[:, A_WIDTH + B_WIDTH:]

        for t in range(sub // CHUNK):
            crows = slice(t * CHUNK, (t + 1) * CHUNK)
            vl = vn[crows, :V7X_LANES]
            vr = vn[crows, V7X_LANES:]
            s_l = jnp.where(lo_c, jnp.dot(ws_ref[0], vl, preferred_element_type=_F32),
                            jnp.dot(ws_ref[1], vl, preferred_element_type=_F32))
            s_r = jnp.where(lo_c, jnp.dot(ws_ref[2], vr, preferred_element_type=_F32),
                            jnp.dot(ws_ref[3], vr, preferred_element_type=_F32))
            s = jnp.concatenate([s_l, s_r], axis=1) + bsm
            oc_ref[0, si * sub + t * CHUNK:si * sub + (t + 1) * CHUNK, :] = (
                u[crows] * s * gate_c[crows]).astype(_BF16)


def _inproj_call(x, mod, gn, w_bf, gq, gk, cos, sa, sb, bmat, lng, lnb, ws_bf, bsm, *, tm, sub, layer):
    b, n, d = x.shape
    nt = n // tm
    in_w = w_bf.shape[-1]
    full = lambda *shape: pl.BlockSpec(shape, lambda bi, i: (0,) * len(shape))
    of_layer = lambda *shape: pl.BlockSpec((None,) + shape, lambda bi, i: (layer,) + (0,) * len(shape))
    in_specs = [
        pl.BlockSpec((1, tm, d), lambda bi, i: (bi, i, 0)),
        pl.BlockSpec((1, 1, 3 * d), lambda bi, i: (bi, 0, 0)),
        full(1, d),
        of_layer(d, in_w),
        full(1, V7X_LANES),
        full(1, V7X_LANES),
        pl.BlockSpec((tm, V7X_LANES), lambda bi, i: (i, 0)),
        pl.BlockSpec((tm, V7X_LANES), lambda bi, i: (i, 0)),
        pl.BlockSpec((tm, V7X_LANES), lambda bi, i: (i, 0)),
        full(V7X_LANES, V7X_LANES),
        full(1, C_WIDTH),
        full(1, C_WIDTH),
        of_layer(C_GROUPS, CHUNK, CHUNK),
        full(CHUNK, C_WIDTH),
    ]
    out_shape = [
        jax.ShapeDtypeStruct((b, A_HEADS, 2, n, A_V_DIM), _BF16),
        jax.ShapeDtypeStruct((b, A_HEADS, n, A_V_DIM), _BF16),
        jax.ShapeDtypeStruct((b, A_HEADS, n // sub, A_V_DIM, sub), _BF16),
        jax.ShapeDtypeStruct((b, n, B_WIDTH), _F32),
        jax.ShapeDtypeStruct((b, n, C_WIDTH), _BF16),
        jax.ShapeDtypeStruct((b, n, A_WIDTH), _BF16),
        jax.ShapeDtypeStruct((b, n, B_WIDTH), _BF16),
    ]
    out_specs = [
        pl.BlockSpec((1, A_HEADS, 2, tm, A_V_DIM), lambda bi, i: (bi, 0, 0, i, 0)),
        pl.BlockSpec((1, A_HEADS, tm, A_V_DIM), lambda bi, i: (bi, 0, i, 0)),
        pl.BlockSpec((1, A_HEADS, tm // sub, A_V_DIM, sub), lambda bi, i: (bi, 0, i, 0, 0)),
        pl.BlockSpec((1, tm, B_WIDTH), lambda bi, i: (bi, i, 0)),
        pl.BlockSpec((1, tm, C_WIDTH), lambda bi, i: (bi, i, 0)),
        pl.BlockSpec((1, tm, A_WIDTH), lambda bi, i: (bi, i, 0)),
        pl.BlockSpec((1, tm, B_WIDTH), lambda bi, i: (bi, i, 0)),
    ]
    return pl.pallas_call(
        functools.partial(_inproj_kernel, sub=sub, d=d),
        grid=(b, nt),
        in_specs=in_specs,
        out_specs=out_specs,
        out_shape=out_shape,
        compiler_params=_params(2),
        name="inproj",
    )(x, mod, gn, w_bf, gq, gk, cos, sa, sb, bmat, lng, lnb, ws_bf, bsm)


def _attn_kernel(*refs, tq, nq, n_src, tiles, lam_init, bounded):
    lam_ref, q2_ref, ga_ref, gsub_ref = refs[:4]
    src_refs = refs[4:4 + 2 * n_src]
    out_ref = refs[4 + 2 * n_src]
    p_sc, m_sc, l_sc, acc_sc = refs[5 + 2 * n_src:]

    def queries(u):
        rows = slice(u * tq, (u + 1) * tq)
        return jnp.concatenate([q2_ref[0, 0, 0, rows, :], q2_ref[0, 0, 1, rows, :]], axis=0)

    l_sc[...] = jnp.zeros(l_sc.shape, _F32)
    acc_sc[...] = jnp.zeros(acc_sc.shape, _F32)
    lv = lam_ref[...]
    lam = (jnp.exp(jnp.sum(lv[0:1] * lv[1:2], axis=-1, keepdims=True))
           - jnp.exp(jnp.sum(lv[2:3] * lv[3:4], axis=-1, keepdims=True)) + lam_init)

    def finalize(u):
        inv = 1.0 / l_sc[u]
        acc = acc_sc[u]
        o = acc[:, :tq] * inv[:, :tq] - lam * (acc[:, tq:] * inv[:, tq:])
        ms = jnp.mean(o * o, axis=0, keepdims=True)
        on = o * lax.rsqrt(ms + EPS)
        rows = slice(u * tq, (u + 1) * tq)
        out = on.T * (gsub_ref[...] * (1.0 - lam_init)) * ga_ref[0, rows, :].astype(_F32)
        out_ref[0, rows, :] = out.astype(_BF16)

    def tile_len(si):
        return src_refs[2 * si + 1].shape[-1]

    def k_tile(si, t):
        tk = tile_len(si)
        if isinstance(t, int):
            return src_refs[2 * si][0, 0, t * tk:(t + 1) * tk, :]
        return src_refs[2 * si][0, 0, pl.ds(pl.multiple_of(t * tk, tk), tk), :]

    def vt_tile(si, t):
        return src_refs[2 * si + 1][0, 0, t]

    if bounded:
        def exponentials(slot, u, k_t):
            p = jnp.exp2(lax.dot_general(k_t, queries(u), _NT, preferred_element_type=_F32))
            l_sc[u] += jnp.sum(p, axis=0, keepdims=True)
            p_sc[slot, :k_t.shape[0], :] = p.astype(_BF16)

        def weighted_values(slot, u, vt_t):
            acc_sc[u] += jnp.dot(vt_t, p_sc[slot, :vt_t.shape[1], :], preferred_element_type=_F32)

        stages = [(u, si, t) for u in range(nq) for si in range(n_src) for t in range(tiles[si])]
        exponentials(0, stages[0][0], k_tile(*stages[0][1:]))
        for j, (u, si, t) in enumerate(stages):
            if j + 1 < len(stages):
                exponentials((j + 1) % 2, stages[j + 1][0], k_tile(*stages[j + 1][1:]))
            weighted_values(j % 2, u, vt_tile(si, t))
            if j + 1 == len(stages) or stages[j + 1][0] != u:
                finalize(u)
    else:
        for u in range(nq):
            m_sc[...] = jnp.full(m_sc.shape, -jnp.inf, _F32)

            def step(k_t, vt_t, u=u):
                s = lax.dot_general(k_t, queries(u), _NT, preferred_element_type=_F32)
                m_prev = m_sc[...]
                m_new = jnp.maximum(m_prev, jnp.max(s, axis=0, keepdims=True))
                alpha = jnp.exp2(m_prev - m_new)
                p = jnp.exp2(s - m_new)
                l_sc[u] = alpha * l_sc[u] + jnp.sum(p, axis=0, keepdims=True)
                acc_sc[u] = alpha * acc_sc[u] + jnp.dot(vt_t, p.astype(_BF16), preferred_element_type=_F32)
                m_sc[...] = m_new

            for si in range(n_src):
                if tiles[si] == 1:
                    step(k_tile(si, 0), vt_tile(si, 0))
                else:
                    def body(t, carry, si=si, step=step):
                        step(k_tile(si, t), vt_tile(si, t))
                        return carry
                    lax.fori_loop(0, tiles[si], body, 0)
            finalize(u)


def _attn_call(lamv, q2, ga, gsub, sources, *, tq, nq, lam_init, bounded):
    b, _, _, n, _ = q2.shape
    rows = nq * tq
    in_specs = [
        pl.BlockSpec(lamv.shape, lambda bi, hi, i: (0, 0)),
        pl.BlockSpec((1, 1, 2, rows, A_V_DIM), lambda bi, hi, i: (bi, hi, 0, i, 0)),
        pl.BlockSpec((1, rows, A_V_DIM), lambda bi, hi, i: (bi, i, hi)),
        pl.BlockSpec((1, A_V_DIM), lambda bi, hi, i: (0, 0)),
    ]
    args = [lamv, q2, ga, gsub]
    tiles = []
    for k_s, vt_s in sources:
        ls = k_s.shape[2]
        nt, _, tk = vt_s.shape[2:]
        tiles.append(nt)
        in_specs.append(pl.BlockSpec((1, 1, ls, A_V_DIM), lambda bi, hi, i: (bi, hi, 0, 0)))
        in_specs.append(pl.BlockSpec((1, 1, nt, A_V_DIM, tk), lambda bi, hi, i: (bi, hi, 0, 0, 0)))
        args += [k_s, vt_s]
    tk_max = max(vt_s.shape[-1] for _, vt_s in sources)
    return pl.pallas_call(
        functools.partial(_attn_kernel, tq=tq, nq=nq, n_src=len(sources), tiles=tuple(tiles),
                          lam_init=lam_init, bounded=bounded),
        grid=(b, A_HEADS, n // rows),
        in_specs=in_specs,
        out_specs=pl.BlockSpec((1, rows, A_V_DIM), lambda bi, hi, i: (bi, i, hi)),
        out_shape=jax.ShapeDtypeStruct((b, n, A_WIDTH), _BF16),
        scratch_shapes=[pltpu.VMEM((2, tk_max, 2 * tq), _BF16),
                        pltpu.VMEM((1, 2 * tq), _F32),
                        pltpu.VMEM((nq, 1, 2 * tq), _F32),
                        pltpu.VMEM((nq, A_V_DIM, 2 * tq), _F32)],
        compiler_params=_params(3),
        name="diff_attn" if bounded else "diff_attn_general",
    )(*args)


SAFE_EXP2_RANGE = 60.0


def _attention(lamv, q2, ga, gsub, sources, gq, gk, *, tq, nq, lam_init):
    bound = A_QK_DIM * Q_SCALE * jnp.max(jnp.abs(gq)) * jnp.max(jnp.abs(gk))
    call = lambda bounded: functools.partial(_attn_call, tq=tq, nq=nq, lam_init=lam_init, bounded=bounded)
    return lax.cond(bound <= SAFE_EXP2_RANGE, call(True), call(False), lamv, q2, ga, gsub, sources)


def _dft1_kernel(f1_ref, x_ref, o_ref, *, rows_per_step):
    n1 = x_ref.shape[1]
    x = x_ref[0].reshape(n1, rows_per_step * B_WIDTH).astype(_BF16)
    z = jnp.dot(f1_ref[...], x, preferred_element_type=_F32)
    o_ref[0] = z.reshape(2 * n1, rows_per_step, B_WIDTH).astype(_BF16)


def _dft1_call(f1, x4d, *, rows_per_step):
    b, n1, n2, _ = x4d.shape
    return pl.pallas_call(
        functools.partial(_dft1_kernel, rows_per_step=rows_per_step),
        grid=(b, n2 // rows_per_step),
        in_specs=[pl.BlockSpec((2 * n1, n1), lambda bi, j: (0, 0)),
                  pl.BlockSpec((1, n1, rows_per_step, B_WIDTH), lambda bi, j: (bi, 0, j, 0))],
        out_specs=pl.BlockSpec((1, 2 * n1, rows_per_step, B_WIDTH), lambda bi, j: (bi, 0, j, 0)),
        out_shape=jax.ShapeDtypeStruct((b, 2 * n1, n2, B_WIDTH), _BF16),
        compiler_params=_params(2),
        name="dft_stage1",
    )(f1, x4d)


def _dft2_kernel(*refs, g, n2, two_stage, norm):
    if two_stage:
        ar_ref, ai_ref, twc_ref, tws_ref, w2c_ref, mix_ref, wf_ref, bf_ref, out_ref = refs
        ar = ar_ref[0, 0].reshape(g * n2, B_WIDTH).astype(_F32)
        ai = ai_ref[0, 0].reshape(g * n2, B_WIDTH).astype(_F32)
        tc = twc_ref[...].reshape(g * n2, V7X_LANES)
        ts = tws_ref[...].reshape(g * n2, V7X_LANES)
        c2 = jnp.concatenate([tc, tc], axis=1)
        s2 = jnp.concatenate([ts, ts], axis=1)
        a = jnp.concatenate([ar * c2 - ai * s2, ar * s2 + ai * c2], axis=1).astype(_BF16)
        uv = jnp.dot(a, mix_ref[...], preferred_element_type=_F32)
    else:
        ar_ref, w2c_ref, mix_ref, wf_ref, bf_ref, out_ref = refs
        uv = jnp.dot(ar_ref[0, 0, 0].astype(_BF16), mix_ref[:B_WIDTH, :], preferred_element_type=_F32)
    u = uv[:, :B_WIDTH].astype(_BF16)
    v = uv[:, B_WIDTH:].astype(_BF16)
    w2c = w2c_ref[...]
    frs = []
    for j in range(g):
        rows = slice(j * n2, (j + 1) * n2)
        frs.append(jnp.dot(w2c, jnp.concatenate([u[rows], v[rows]], axis=0), preferred_element_type=_F32))
    fr = jnp.concatenate(frs, axis=0) * norm
    y = jnp.dot(fr.astype(_BF16), wf_ref[...], preferred_element_type=_F32) + bf_ref[...]
    if two_stage:
        out_ref[0] = jnp.swapaxes(y.reshape(g, n2, B_WIDTH), 0, 1).astype(_BF16)
    else:
        out_ref[0] = y.astype(_BF16)


def _dft_tables(n1, n2):
    n = n1 * n2
    k1 = np.arange(n1)
    a1 = 2.0 * np.pi * ((k1[:, None] * k1[None, :]) % n1) / n1
    f1 = np.concatenate([np.cos(a1), np.sin(a1)], axis=0)
    k2 = np.arange(n2)
    a2 = 2.0 * np.pi * ((k2[:, None] * k2[None, :]) % n2) / n2
    w2c = np.concatenate([np.cos(a2), -np.sin(a2)], axis=1)
    gch = np.arange(B_GROUP_DIM)
    ag = 2.0 * np.pi * ((gch[:, None] * gch[None, :]) % B_GROUP_DIM) / B_GROUP_DIM
    eye = np.eye(B_GROUPS)
    cg = np.kron(eye, np.cos(ag))
    sg = np.kron(eye, np.sin(ag))
    mix = np.block([[cg, sg], [-sg, cg]])
    to_bf16 = lambda t: jnp.asarray(t, _F32).astype(_BF16)
    return to_bf16(f1), to_bf16(w2c), to_bf16(mix), 1.0 / math.sqrt(n * B_GROUP_DIM)


def _twiddles(n1, n2):
    k1 = jnp.arange(n1, dtype=jnp.int32)[:, None]
    m2 = jnp.arange(n2, dtype=jnp.int32)[None, :]
    ang = (k1 * m2).astype(_F32) * (2.0 * math.pi / (n1 * n2))
    shape = (n1, n2, V7X_LANES)
    return (jnp.broadcast_to(jnp.cos(ang)[:, :, None], shape), jnp.broadcast_to(jnp.sin(ang)[:, :, None], shape))


def _fourier(f, wf_bd, bf_row):
    b, n, _ = f.shape
    two_stage = n > 2 * V7X_LANES
    n2 = V7X_LANES if two_stage else n
    n1 = n // n2
    g = min(n1, V7X_BF16_SUBLANES)
    f1, w2c, mix, norm = _dft_tables(n1, n2)
    full = lambda *shape: pl.BlockSpec(shape, lambda bi, j: (0,) * len(shape))
    consts = [w2c, mix, wf_bd, bf_row]
    const_specs = [full(*w2c.shape), full(*mix.shape), full(B_WIDTH, B_WIDTH), full(1, B_WIDTH)]
    if two_stage:
        a = _dft1_call(f1, f.reshape(b, n1, n2, B_WIDTH), rows_per_step=2 * V7X_BF16_SUBLANES)
        a5 = a.reshape(b, 2, n1, n2, B_WIDTH)
        out_shape = jax.ShapeDtypeStruct((b, n2, n1, B_WIDTH), _BF16)
        out_spec = pl.BlockSpec((1, n2, g, B_WIDTH), lambda bi, j: (bi, 0, j, 0))
        twc, tws = _twiddles(n1, n2)
        args = [a5, a5, twc, tws] + consts
        in_specs = [pl.BlockSpec((1, 1, g, n2, B_WIDTH), lambda bi, j: (bi, 0, j, 0, 0)),
                    pl.BlockSpec((1, 1, g, n2, B_WIDTH), lambda bi, j: (bi, 1, j, 0, 0)),
                    pl.BlockSpec((g, n2, V7X_LANES), lambda bi, j: (j, 0, 0)),
                    pl.BlockSpec((g, n2, V7X_LANES), lambda bi, j: (j, 0, 0))] + const_specs
    else:
        args = [f.reshape(b, 1, 1, n2, B_WIDTH)] + consts
        in_specs = [pl.BlockSpec((1, 1, 1, n2, B_WIDTH), lambda bi, j: (bi, 0, 0, 0, 0))] + const_specs
        out_shape = jax.ShapeDtypeStruct((b, n2, B_WIDTH), _BF16)
        out_spec = pl.BlockSpec((1, n2, B_WIDTH), lambda bi, j: (bi, 0, 0))
    out = pl.pallas_call(
        functools.partial(_dft2_kernel, g=g, n2=n2, two_stage=two_stage, norm=norm),
        grid=(b, n1 // g),
        in_specs=in_specs,
        out_specs=out_spec,
        out_shape=out_shape,
        compiler_params=_params(2),
        name="dft_stage2",
    )(*args)
    return out.reshape(b, n, B_WIDTH)


def _merge_kernel(oa_ref, ob_ref, gb_ref, oc_ref, x_ref, mod_ref, gn_ref, wm_ref, wa_ref, wb_ref, wc_ref, wo_ref,
                  o_ref, *, d, sub):
    mod = mod_ref[0]
    for r0 in range(0, x_ref.shape[1], sub):
        rows = slice(r0, r0 + sub)
        x = x_ref[0, rows, :]
        hb = _modulated_norm(x, mod, gn_ref[...], d)
        branches = (jnp.dot(oa_ref[0, rows, :], wa_ref[...], preferred_element_type=_F32),
                    jnp.dot(ob_ref[0, rows, :] * gb_ref[0, rows, :], wb_ref[...], preferred_element_type=_F32),
                    jnp.dot(oc_ref[0, rows, :], wc_ref[...], preferred_element_type=_F32))
        y = None
        for j, yj in enumerate(branches):
            mj = _sigmoid(jnp.dot(hb, wm_ref[:, OFF_MERGE + j * d:OFF_MERGE + (j + 1) * d],
                                  preferred_element_type=_F32))
            y = mj * yj if y is None else y + mj * yj
        out = jnp.dot(y.astype(_BF16), wo_ref[...], preferred_element_type=_F32)
        o_ref[0, rows, :] = x + mod[:, 2 * d:] * out


def _merge_call(oa, ob, gb, oc, x, mod, gn, w_in_bf, wa, wb, wc, wo, *, tm, sub, layer):
    b, n, d = x.shape
    row = lambda w: pl.BlockSpec((1, tm, w), lambda bi, i: (bi, i, 0))
    of_layer = lambda *shape: pl.BlockSpec((None,) + shape, lambda bi, i: (layer,) + (0,) * len(shape))
    return pl.pallas_call(
        functools.partial(_merge_kernel, d=d, sub=sub),
        grid=(b, n // tm),
        in_specs=[row(A_WIDTH), row(B_WIDTH), row(B_WIDTH), row(C_WIDTH), row(d),
                  pl.BlockSpec((1, 1, 3 * d), lambda bi, i: (bi, 0, 0)),
                  pl.BlockSpec((1, d), lambda bi, i: (0, 0)),
                  of_layer(d, w_in_bf.shape[-1]), of_layer(A_WIDTH, d), of_layer(B_WIDTH, d),
                  of_layer(C_WIDTH, d), of_layer(d, d)],
        out_specs=row(d),
        out_shape=jax.ShapeDtypeStruct((b, n, d), _F32),
        compiler_params=_params(2),
        name="merge",
    )(oa, ob, gb, oc, x, mod, gn, w_in_bf, wa, wb, wc, wo)


def _rope_tables(n):
    n_rows = n // GRID_W
    freqs = ROPE_BASE ** (-jnp.arange(0, AXIS_DIM, 2, dtype=_F32) / AXIS_DIM)
    ang_r = jnp.arange(n_rows, dtype=_F32)[:, None] * freqs
    ang_c = jnp.arange(GRID_W, dtype=_F32)[:, None] * freqs

    def on_grid(fn):
        shape = (n_rows, GRID_W, AXIS_DIM // 2)
        r = jnp.broadcast_to(fn(ang_r)[:, None, :], shape)
        c = jnp.broadcast_to(fn(ang_c)[None, :, :], shape)
        return jnp.concatenate([r, r, c, c], axis=-1).reshape(n, A_QK_DIM)

    cos, sin = on_grid(jnp.cos), on_grid(jnp.sin)
    first_half = (jnp.arange(A_QK_DIM) % AXIS_DIM) < AXIS_DIM // 2
    sa = jnp.where(first_half, -sin, 0.0)
    sb = jnp.where(first_half, 0.0, sin)
    tile2 = lambda t: jnp.concatenate([t, t], axis=-1)
    return tile2(cos), tile2(sa), tile2(sb)


def _no_rope_tables(n):
    return (jnp.ones((n, V7X_LANES), _F32), jnp.zeros((n, V7X_LANES), _F32), jnp.zeros((n, V7X_LANES), _F32))


def kernel(x, c, ctx, c_ctx, w_ada, b_ada, g_norm, w_in, g_q, g_k, lam_q1, lam_k1, lam_q2, lam_k2, g_sub,
           w_f, b_f, ln_g, ln_b, w_s, b_s, w_br_a, w_br_b, w_br_c, w_out):
    b, n, d = x.shape
    n_ctx = ctx.shape[1]
    depth = w_in.shape[0]
    tm_lat = min(n, 512)
    tm_ctx = min(n_ctx, 512)

    n_rows = V7X_SUBLANES * pl.cdiv(b + 1, V7X_SUBLANES)
    rows = jnp.concatenate([c, c_ctx[None, :], jnp.zeros((n_rows - b - 1, d), _F32)], axis=0)
    mod_all = _mod_call(rows, w_ada, b_ada)

    rope_lat = _rope_tables(n)
    rope_ctx = _no_rope_tables(n_ctx)
    group_mean = np.kron(np.eye(V7X_LANES // A_QK_DIM), np.full((A_QK_DIM, A_QK_DIM), 1.0 / A_QK_DIM))
    bmat = jnp.asarray(group_mean, _BF16)

    w_in_bf = w_in.astype(_BF16)
    w_s_bf = w_s.astype(_BF16)
    merge_w = (w_in_bf, w_br_a.astype(_BF16), w_br_b.astype(_BF16), w_br_c.astype(_BF16), w_out.astype(_BF16))

    for l in range(depth):
        last = l == depth - 1
        lam_init = 0.8 - 0.6 * math.exp(-0.3 * l)
        lamv = jnp.stack([lam_q1[l], lam_k1[l], lam_q2[l], lam_k2[l]], axis=0)
        shared = dict(
            gn=g_norm[l][None, :], w_bf=w_in_bf,
            gq=jnp.tile(g_q[l], 2)[None, :], gk=jnp.tile(g_k[l], 2)[None, :], bmat=bmat,
            lng=ln_g[l][None, :], lnb=ln_b[l][None, :], ws_bf=w_s_bf,
            bsm=jnp.repeat(b_s[l].T, C_GROUP_DIM, axis=1), layer=l)
        gsub = g_sub[l][None, :]
        wf_bd = jax.scipy.linalg.block_diag(*[w_f[l, gi] for gi in range(B_GROUPS)]).astype(_BF16)
        bf_row = b_f[l].reshape(1, B_WIDTH)

        mod_lat = mod_all[l, :b][:, None, :]
        mod_ctx = jnp.broadcast_to(mod_all[l, b][None, None, :], (b, 1, 3 * d))

        cq2, ck, cvt, cf, coc, cga, cgb = _inproj_call(
            ctx, mod_ctx, cos=rope_ctx[0], sa=rope_ctx[1], sb=rope_ctx[2], tm=tm_ctx, sub=tm_ctx, **shared)
        q2, k, vt, f, oc, ga, gb = _inproj_call(
            x, mod_lat, cos=rope_lat[0], sa=rope_lat[1], sb=rope_lat[2], tm=min(n, 2 * tm_lat), sub=tm_lat,
            **shared)

        oa = _attention(lamv, q2, ga, gsub, [(ck, cvt), (k, vt)], g_q[l], g_k[l], tq=tm_lat,
                        nq=min(4, n // tm_lat), lam_init=lam_init)
        ob = _fourier(f, wf_bd, bf_row)
        x_new = _merge_call(oa, ob, gb, oc, x, mod_lat, shared["gn"], *merge_w, tm=min(n, 2 * tm_lat), sub=tm_lat,
                            layer=l)

        if not last:
            coa = _attention(lamv, cq2, cga, gsub, [(ck, cvt)], g_q[l], g_k[l], tq=tm_ctx, nq=1,
                             lam_init=lam_init)
            cob = _fourier(cf, wf_bd, bf_row)
            ctx = _merge_call(coa, cob, cgb, coc, ctx, mod_ctx, shared["gn"], *merge_w, tm=tm_ctx, sub=tm_ctx,
                              layer=l)
        x = x_new
    return x
```

```python
import functools
import math

import numpy as np
import jax
import jax.numpy as jnp
from jax import lax
from jax.experimental import pallas as pl
from jax.experimental.pallas import tpu as pltpu

A_HEADS = 4
A_QK_DIM = 64
A_V_DIM = 2 * A_QK_DIM
A_WIDTH = A_HEADS * A_V_DIM
B_GROUPS = 4
B_GROUP_DIM = 64
B_WIDTH = B_GROUPS * B_GROUP_DIM
C_GROUPS = 4
C_GROUP_DIM = 64
C_WIDTH = C_GROUPS * C_GROUP_DIM
CHUNK = 128
GRID_W = 64
ROPE_BASE = 10000.0
AXIS_DIM = A_QK_DIM // 2
EPS = 1e-6

OFF_Q = 0
OFF_K = OFF_Q + A_WIDTH
OFF_V = OFF_K + A_WIDTH
OFF_F = OFF_V + A_WIDTH
OFF_U = OFF_F + B_WIDTH
OFF_VC = OFF_U + C_WIDTH
OFF_GATE = OFF_VC + C_WIDTH
GATE_WIDTH = A_WIDTH + B_WIDTH + C_WIDTH
OFF_MERGE = OFF_GATE + GATE_WIDTH
N_BRANCH = 3

V7X_LANES = 128
V7X_SUBLANES = 8
V7X_BF16_SUBLANES = 16
V7X_MXU_DIM = 256
V7X_VMEM_LIMIT_BYTES = 56 * 1024 * 1024

LOG2E = math.log2(math.e)
Q_SCALE = (A_QK_DIM ** -0.5) * LOG2E

_F32 = jnp.float32
_BF16 = jnp.bfloat16
_NT = (((1,), (1,)), ((), ()))


def _sigmoid(x):
    return 1.0 / (1.0 + jnp.exp(-x))


def _params(n_axes):
    return pltpu.CompilerParams(dimension_semantics=("arbitrary",) * n_axes,
                                vmem_limit_bytes=V7X_VMEM_LIMIT_BYTES)


def _mod_kernel(c_ref, w_ref, b_ref, o_ref):
    cc = c_ref[...]
    s = cc * _sigmoid(cc)
    o_ref[0] = jnp.dot(s, w_ref[0], preferred_element_type=_F32) + b_ref[0]


def _mod_call(rows, w_ada, b_ada):
    depth, d, d3 = w_ada.shape
    nb = d3 // d
    return pl.pallas_call(
        _mod_kernel,
        grid=(depth, nb),
        in_specs=[pl.BlockSpec((rows.shape[0], d), lambda l, j: (0, 0)),
                  pl.BlockSpec((1, d, d), lambda l, j: (l, 0, j)),
                  pl.BlockSpec((1, 1, d), lambda l, j: (l, 0, j))],
        out_specs=pl.BlockSpec((1, rows.shape[0], d), lambda l, j: (l, 0, j)),
        out_shape=jax.ShapeDtypeStruct((depth, rows.shape[0], d3), _F32),
        compiler_params=_params(2),
        name="adaln_mod",
    )(rows, w_ada, b_ada.reshape(depth, 1, d3))


def _modulated_norm(x, mod, gn, d):
    ms = jnp.mean(x * x, axis=-1, keepdims=True)
    h = (x * lax.rsqrt(ms + EPS) * gn) * (1.0 + mod[:, d:2 * d]) + mod[:, :d]
    return h.astype(_BF16)


def _inproj_kernel(x_ref, mod_ref, gn_ref, w_ref, gq_ref, gk_ref, cos_ref, sa_ref, sb_ref,
                   bmat_ref, lng_ref, lnb_ref, ws_ref, bsm_ref, *out_refs, sub, d, kv_only):
    if kv_only:
        k_ref, vt_ref = out_refs
    else:
        q2_ref, k_ref, vt_ref, f_ref, oc_ref, ga_ref, gb_ref = out_refs
    pair = 2 * A_V_DIM
    bmat = bmat_ref[...]
    lo = (lax.broadcasted_iota(jnp.int32, (sub, pair), 1) % A_V_DIM) < A_QK_DIM
    lo_c = lax.broadcasted_iota(jnp.int32, (CHUNK, V7X_LANES), 1) < C_GROUP_DIM
    bsm = bsm_ref[...]
    twice = lambda t: jnp.concatenate([t, t], axis=1)
    gq = twice(gq_ref[...])
    gk = twice(gk_ref[...])

    for si in range(x_ref.shape[1] // sub):
        rows = slice(si * sub, (si + 1) * sub)
        hb = _modulated_norm(x_ref[0, rows, :], mod_ref[0], gn_ref[...], d)

        def proj(off, width, hb=hb):
            return jnp.dot(hb, w_ref[:, off:off + width], preferred_element_type=_F32)

        cos = twice(cos_ref[rows, :])
        sa = twice(sa_ref[rows, :])
        sb = twice(sb_ref[rows, :])

        def qk_heads(p_2h, g, cos=cos, sa=sa, sb=sb):
            msq = jnp.dot((p_2h * p_2h).astype(_BF16), bmat, preferred_element_type=_F32)
            t = p_2h * lax.rsqrt(msq + EPS) * g
            return (t * cos + pltpu.roll(t, pair - AXIS_DIM // 2, 1) * sa
                    + pltpu.roll(t, AXIS_DIM // 2, 1) * sb)

        if not kv_only:
            pq = proj(OFF_Q, A_WIDTH)
            for hp in range(A_HEADS // 2):
                t = qk_heads(pq[:, hp * pair:(hp + 1) * pair], gq) * Q_SCALE
                t_lo = jnp.where(lo, t, 0.0).astype(_BF16)
                t_hi = jnp.where(lo, 0.0, t).astype(_BF16)
                for h2 in range(2):
                    q2_ref[0, 2 * hp + h2, 0, rows, :] = t_lo[:, h2 * A_V_DIM:(h2 + 1) * A_V_DIM]
                    q2_ref[0, 2 * hp + h2, 1, rows, :] = t_hi[:, h2 * A_V_DIM:(h2 + 1) * A_V_DIM]

        pk = proj(OFF_K, A_WIDTH)
        for hp in range(A_HEADS // 2):
            t = qk_heads(pk[:, hp * pair:(hp + 1) * pair], gk).astype(_BF16)
            for h2 in range(2):
                k_ref[0, 2 * hp + h2, rows, :] = t[:, h2 * A_V_DIM:(h2 + 1) * A_V_DIM]

        pv = proj(OFF_V, A_WIDTH)
        for hh in range(A_HEADS):
            vt_ref[0, hh, si] = pv[:, hh * A_V_DIM:(hh + 1) * A_V_DIM].T.astype(_BF16)
        if kv_only:
            continue

        f_ref[0, rows, :] = proj(OFF_F, B_WIDTH)

        u = proj(OFF_U, C_WIDTH)
        vc = proj(OFF_VC, C_WIDTH)
        mu = jnp.mean(vc, axis=-1, keepdims=True)
        dv = vc - mu
        var = jnp.mean(dv * dv, axis=-1, keepdims=True)
        vn = (dv * lax.rsqrt(var + EPS) * lng_ref[...] + lnb_ref[...]).astype(_BF16)

        pg = proj(OFF_GATE, GATE_WIDTH)
        gate = pg * _sigmoid(pg)
        ga_ref[0, rows, :] = gate[:, :A_WIDTH].astype(_BF16)
        gb_ref[0, rows, :] = gate[:, A_WIDTH:A_WIDTH + B_WIDTH].astype(_BF16)
        gate_c = gate[:, A_WIDTH + B_WIDTH:]

        for t in range(sub // CHUNK):
            crows = slice(t * CHUNK, (t + 1) * CHUNK)
            vl = vn[crows, :V7X_LANES]
            vr = vn[crows, V7X_LANES:]
            s_l = jnp.where(lo_c, jnp.dot(ws_ref[0], vl, preferred_element_type=_F32),
                            jnp.dot(ws_ref[1], vl, preferred_element_type=_F32))
            s_r = jnp.where(lo_c, jnp.dot(ws_ref[2], vr, preferred_element_type=_F32),
                            jnp.dot(ws_ref[3], vr, preferred_element_type=_F32))
            s = jnp.concatenate([s_l, s_r], axis=1) + bsm
            oc_ref[0, si * sub + t * CHUNK:si * sub + (t + 1) * CHUNK, :] = (
                u[crows] * s * gate_c[crows]).astype(_BF16)


def _inproj_call(x, mod, gn, w_bf, gq, gk, cos, sa, sb, bmat, lng, lnb, ws_bf, bsm, *, tm, sub, layer,
                 kv_only=False):
    b, n, d = x.shape
    nt = n // tm
    full = lambda *shape: pl.BlockSpec(shape, lambda bi, i: (0,) * len(shape))
    of_layer = lambda *shape: pl.BlockSpec((None,) + shape, lambda bi, i: (layer,) + (0,) * len(shape))
    in_specs = [
        pl.BlockSpec((1, tm, d), lambda bi, i: (bi, i, 0)),
        pl.BlockSpec((1, 1, 3 * d), lambda bi, i: (bi, 0, 0)),
        full(1, d),
        pl.BlockSpec((None, d, OFF_MERGE), lambda bi, i: (layer, 0, 1)),
        full(1, V7X_LANES),
        full(1, V7X_LANES),
        pl.BlockSpec((tm, V7X_LANES), lambda bi, i: (i, 0)),
        pl.BlockSpec((tm, V7X_LANES), lambda bi, i: (i, 0)),
        pl.BlockSpec((tm, V7X_LANES), lambda bi, i: (i, 0)),
        full(2 * A_V_DIM, 2 * A_V_DIM),
        full(1, C_WIDTH),
        full(1, C_WIDTH),
        of_layer(C_GROUPS, CHUNK, CHUNK),
        full(CHUNK, C_WIDTH),
    ]
    out_shape = [
        jax.ShapeDtypeStruct((b, A_HEADS, 2, n, A_V_DIM), _BF16),
        jax.ShapeDtypeStruct((b, A_HEADS, n, A_V_DIM), _BF16),
        jax.ShapeDtypeStruct((b, A_HEADS, n // sub, A_V_DIM, sub), _BF16),
        jax.ShapeDtypeStruct((b, n, B_WIDTH), _F32),
        jax.ShapeDtypeStruct((b, n, C_WIDTH), _BF16),
        jax.ShapeDtypeStruct((b, n, A_WIDTH), _BF16),
        jax.ShapeDtypeStruct((b, n, B_WIDTH), _BF16),
    ]
    out_specs = [
        pl.BlockSpec((1, A_HEADS, 2, tm, A_V_DIM), lambda bi, i: (bi, 0, 0, i, 0)),
        pl.BlockSpec((1, A_HEADS, tm, A_V_DIM), lambda bi, i: (bi, 0, i, 0)),
        pl.BlockSpec((1, A_HEADS, tm // sub, A_V_DIM, sub), lambda bi, i: (bi, 0, i, 0, 0)),
        pl.BlockSpec((1, tm, B_WIDTH), lambda bi, i: (bi, i, 0)),
        pl.BlockSpec((1, tm, C_WIDTH), lambda bi, i: (bi, i, 0)),
        pl.BlockSpec((1, tm, A_WIDTH), lambda bi, i: (bi, i, 0)),
        pl.BlockSpec((1, tm, B_WIDTH), lambda bi, i: (bi, i, 0)),
    ]
    if kv_only:
        out_shape, out_specs = out_shape[1:3], out_specs[1:3]
    return pl.pallas_call(
        functools.partial(_inproj_kernel, sub=sub, d=d, kv_only=kv_only),
        grid=(b, nt),
        in_specs=in_specs,
        out_specs=out_specs,
        out_shape=out_shape,
        compiler_params=_params(2),
        name="inproj",
    )(x, mod, gn, w_bf, gq, gk, cos, sa, sb, bmat, lng, lnb, ws_bf, bsm)


def _attn_kernel(*refs, tq, nq, n_src, tiles, lam_init, bounded):
    lam_ref, q2_ref, ga_ref, gsub_ref = refs[:4]
    src_refs = refs[4:4 + 2 * n_src]
    out_ref = refs[4 + 2 * n_src]
    p_sc, m_sc, l_sc, acc_sc = refs[5 + 2 * n_src:]

    def queries(u):
        rows = slice(u * tq, (u + 1) * tq)
        return jnp.concatenate([q2_ref[0, 0, 0, rows, :], q2_ref[0, 0, 1, rows, :]], axis=0)

    l_sc[...] = jnp.zeros(l_sc.shape, _F32)
    acc_sc[...] = jnp.zeros(acc_sc.shape, _F32)
    lv = lam_ref[...]
    lam = (jnp.exp(jnp.sum(lv[0:1] * lv[1:2], axis=-1, keepdims=True))
           - jnp.exp(jnp.sum(lv[2:3] * lv[3:4], axis=-1, keepdims=True)) + lam_init)

    def finalize(u):
        inv = 1.0 / l_sc[u]
        acc = acc_sc[u]
        o = acc[:, :tq] * inv[:, :tq] - lam * (acc[:, tq:] * inv[:, tq:])
        ms = jnp.mean(o * o, axis=0, keepdims=True)
        on = o * lax.rsqrt(ms + EPS)
        rows = slice(u * tq, (u + 1) * tq)
        out = on.T * (gsub_ref[...] * (1.0 - lam_init)) * ga_ref[0, rows, :].astype(_F32)
        out_ref[0, rows, :] = out.astype(_BF16)

    def tile_len(si):
        return src_refs[2 * si + 1].shape[-1]

    def k_tile(si, t):
        tk = tile_len(si)
        if isinstance(t, int):
            return src_refs[2 * si][0, 0, t * tk:(t + 1) * tk, :]
        return src_refs[2 * si][0, 0, pl.ds(pl.multiple_of(t * tk, tk), tk), :]

    def vt_tile(si, t):
        return src_refs[2 * si + 1][0, 0, t]

    if bounded:
        def exponentials(slot, u, k_t):
            p = jnp.exp2(lax.dot_general(k_t, queries(u), _NT, preferred_element_type=_F32))
            l_sc[u] += jnp.sum(p, axis=0, keepdims=True)
            p_sc[slot, :k_t.shape[0], :] = p.astype(_BF16)

        def weighted_values(slot, u, vt_t):
            acc_sc[u] += jnp.dot(vt_t, p_sc[slot, :vt_t.shape[1], :], preferred_element_type=_F32)

        stages = [(u, si, t) for u in range(nq) for si in range(n_src) for t in range(tiles[si])]
        exponentials(0, stages[0][0], k_tile(*stages[0][1:]))
        for j, (u, si, t) in enumerate(stages):
            if j + 1 < len(stages):
                exponentials((j + 1) % 2, stages[j + 1][0], k_tile(*stages[j + 1][1:]))
            weighted_values(j % 2, u, vt_tile(si, t))
            if j + 1 == len(stages) or stages[j + 1][0] != u:
                finalize(u)
    else:
        for u in range(nq):
            m_sc[...] = jnp.full(m_sc.shape, -jnp.inf, _F32)

            def step(k_t, vt_t, u=u):
                s = lax.dot_general(k_t, queries(u), _NT, preferred_element_type=_F32)
                m_prev = m_sc[...]
                m_new = jnp.maximum(m_prev, jnp.max(s, axis=0, keepdims=True))
                alpha = jnp.exp2(m_prev - m_new)
                p = jnp.exp2(s - m_new)
                l_sc[u] = alpha * l_sc[u] + jnp.sum(p, axis=0, keepdims=True)
                acc_sc[u] = alpha * acc_sc[u] + jnp.dot(vt_t, p.astype(_BF16), preferred_element_type=_F32)
                m_sc[...] = m_new

            for si in range(n_src):
                if tiles[si] == 1:
                    step(k_tile(si, 0), vt_tile(si, 0))
                else:
                    def body(t, carry, si=si, step=step):
                        step(k_tile(si, t), vt_tile(si, t))
                        return carry
                    lax.fori_loop(0, tiles[si], body, 0)
            finalize(u)


def _attn_call(lamv, q2, ga, gsub, sources, *, tq, nq, lam_init, bounded):
    b, _, _, n, _ = q2.shape
    rows = nq * tq
    in_specs = [
        pl.BlockSpec(lamv.shape, lambda bi, hi, i: (0, 0)),
        pl.BlockSpec((1, 1, 2, rows, A_V_DIM), lambda bi, hi, i: (bi, hi, 0, i, 0)),
        pl.BlockSpec((1, rows, A_V_DIM), lambda bi, hi, i: (bi, i, hi)),
        pl.BlockSpec((1, A_V_DIM), lambda bi, hi, i: (0, 0)),
    ]
    args = [lamv, q2, ga, gsub]
    tiles = []
    for k_s, vt_s in sources:
        ls = k_s.shape[2]
        nt, _, tk = vt_s.shape[2:]
        tiles.append(nt)
        in_specs.append(pl.BlockSpec((1, 1, ls, A_V_DIM), lambda bi, hi, i: (bi, hi, 0, 0)))
        in_specs.append(pl.BlockSpec((1, 1, nt, A_V_DIM, tk), lambda bi, hi, i: (bi, hi, 0, 0, 0)))
        args += [k_s, vt_s]
    tk_max = max(vt_s.shape[-1] for _, vt_s in sources)
    return pl.pallas_call(
        functools.partial(_attn_kernel, tq=tq, nq=nq, n_src=len(sources), tiles=tuple(tiles),
                          lam_init=lam_init, bounded=bounded),
        grid=(b, A_HEADS, n // rows),
        in_specs=in_specs,
        out_specs=pl.BlockSpec((1, rows, A_V_DIM), lambda bi, hi, i: (bi, i, hi)),
        out_shape=jax.ShapeDtypeStruct((b, n, A_WIDTH), _BF16),
        scratch_shapes=[pltpu.VMEM((2, tk_max, 2 * tq), _BF16),
                        pltpu.VMEM((1, 2 * tq), _F32),
                        pltpu.VMEM((nq, 1, 2 * tq), _F32),
                        pltpu.VMEM((nq, A_V_DIM, 2 * tq), _F32)],
        compiler_params=_params(3),
        name="diff_attn" if bounded else "diff_attn_general",
    )(*args)


SAFE_EXP2_RANGE = 60.0


def _attention(lamv, q2, ga, gsub, sources, gq, gk, *, tq, nq, lam_init):
    bound = A_QK_DIM * Q_SCALE * jnp.max(jnp.abs(gq)) * jnp.max(jnp.abs(gk))
    call = lambda bounded: functools.partial(_attn_call, tq=tq, nq=nq, lam_init=lam_init, bounded=bounded)
    return lax.cond(bound <= SAFE_EXP2_RANGE, call(True), call(False), lamv, q2, ga, gsub, sources)


def _dft1_kernel(f1_ref, x_ref, o_ref, *, rows_per_step):
    n1 = x_ref.shape[1]
    x = x_ref[0].reshape(n1, rows_per_step * B_WIDTH).astype(_BF16)
    z = jnp.dot(f1_ref[...], x, preferred_element_type=_F32)
    o_ref[0] = z.reshape(2 * n1, rows_per_step, B_WIDTH).astype(_BF16)


def _dft1_call(f1, x4d, *, rows_per_step):
    b, n1, n2, _ = x4d.shape
    return pl.pallas_call(
        functools.partial(_dft1_kernel, rows_per_step=rows_per_step),
        grid=(b, n2 // rows_per_step),
        in_specs=[pl.BlockSpec((2 * n1, n1), lambda bi, j: (0, 0)),
                  pl.BlockSpec((1, n1, rows_per_step, B_WIDTH), lambda bi, j: (bi, 0, j, 0))],
        out_specs=pl.BlockSpec((1, 2 * n1, rows_per_step, B_WIDTH), lambda bi, j: (bi, 0, j, 0)),
        out_shape=jax.ShapeDtypeStruct((b, 2 * n1, n2, B_WIDTH), _BF16),
        compiler_params=_params(2),
        name="dft_stage1",
    )(f1, x4d)


def _dft2_kernel(*refs, g, n2, two_stage, norm):
    if two_stage:
        ar_ref, ai_ref, twc_ref, tws_ref, w2c_ref, mix_ref, wf_ref, bf_ref, out_ref = refs
        ar = ar_ref[0, 0].reshape(g * n2, B_WIDTH).astype(_F32)
        ai = ai_ref[0, 0].reshape(g * n2, B_WIDTH).astype(_F32)
        tc = twc_ref[...].reshape(g * n2, V7X_LANES)
        ts = tws_ref[...].reshape(g * n2, V7X_LANES)
        c2 = jnp.concatenate([tc, tc], axis=1)
        s2 = jnp.concatenate([ts, ts], axis=1)
        a = jnp.concatenate([ar * c2 - ai * s2, ar * s2 + ai * c2], axis=1).astype(_BF16)
        uv = jnp.dot(a, mix_ref[...], preferred_element_type=_F32)
    else:
        ar_ref, w2c_ref, mix_ref, wf_ref, bf_ref, out_ref = refs
        uv = jnp.dot(ar_ref[0, 0, 0].astype(_BF16), mix_ref[:B_WIDTH, :], preferred_element_type=_F32)
    u = uv[:, :B_WIDTH].astype(_BF16)
    v = uv[:, B_WIDTH:].astype(_BF16)
    w2c = w2c_ref[...]
    frs = []
    for j in range(g):
        rows = slice(j * n2, (j + 1) * n2)
        frs.append(jnp.dot(w2c, jnp.concatenate([u[rows], v[rows]], axis=0), preferred_element_type=_F32))
    fr = jnp.concatenate(frs, axis=0) * norm
    y = jnp.dot(fr.astype(_BF16), wf_ref[...], preferred_element_type=_F32) + bf_ref[...]
    if two_stage:
        out_ref[0] = jnp.swapaxes(y.reshape(g, n2, B_WIDTH), 0, 1).astype(_BF16)
    else:
        out_ref[0] = y.astype(_BF16)


def _dft_tables(n1, n2):
    n = n1 * n2
    k1 = np.arange(n1)
    a1 = 2.0 * np.pi * ((k1[:, None] * k1[None, :]) % n1) / n1
    f1 = np.concatenate([np.cos(a1), np.sin(a1)], axis=0)
    k2 = np.arange(n2)
    a2 = 2.0 * np.pi * ((k2[:, None] * k2[None, :]) % n2) / n2
    w2c = np.concatenate([np.cos(a2), -np.sin(a2)], axis=1)
    gch = np.arange(B_GROUP_DIM)
    ag = 2.0 * np.pi * ((gch[:, None] * gch[None, :]) % B_GROUP_DIM) / B_GROUP_DIM
    eye = np.eye(B_GROUPS)
    cg = np.kron(eye, np.cos(ag))
    sg = np.kron(eye, np.sin(ag))
    mix = np.block([[cg, sg], [-sg, cg]])
    to_bf16 = lambda t: jnp.asarray(t, _F32).astype(_BF16)
    return to_bf16(f1), to_bf16(w2c), to_bf16(mix), 1.0 / math.sqrt(n * B_GROUP_DIM)


def _twiddles(n1, n2):
    k1 = jnp.arange(n1, dtype=jnp.int32)[:, None]
    m2 = jnp.arange(n2, dtype=jnp.int32)[None, :]
    ang = (k1 * m2).astype(_F32) * (2.0 * math.pi / (n1 * n2))
    shape = (n1, n2, V7X_LANES)
    return (jnp.broadcast_to(jnp.cos(ang)[:, :, None], shape), jnp.broadcast_to(jnp.sin(ang)[:, :, None], shape))


def _fourier(f, wf_bd, bf_row):
    b, n, _ = f.shape
    two_stage = n > 2 * V7X_LANES
    n2 = V7X_LANES if two_stage else n
    n1 = n // n2
    g = min(n1, V7X_BF16_SUBLANES)
    f1, w2c, mix, norm = _dft_tables(n1, n2)
    full = lambda *shape: pl.BlockSpec(shape, lambda bi, j: (0,) * len(shape))
    consts = [w2c, mix, wf_bd, bf_row]
    const_specs = [full(*w2c.shape), full(*mix.shape), full(B_WIDTH, B_WIDTH), full(1, B_WIDTH)]
    if two_stage:
        a = _dft1_call(f1, f.reshape(b, n1, n2, B_WIDTH), rows_per_step=2 * V7X_BF16_SUBLANES)
        a5 = a.reshape(b, 2, n1, n2, B_WIDTH)
        out_shape = jax.ShapeDtypeStruct((b, n2, n1, B_WIDTH), _BF16)
        out_spec = pl.BlockSpec((1, n2, g, B_WIDTH), lambda bi, j: (bi, 0, j, 0))
        twc, tws = _twiddles(n1, n2)
        args = [a5, a5, twc, tws] + consts
        in_specs = [pl.BlockSpec((1, 1, g, n2, B_WIDTH), lambda bi, j: (bi, 0, j, 0, 0)),
                    pl.BlockSpec((1, 1, g, n2, B_WIDTH), lambda bi, j: (bi, 1, j, 0, 0)),
                    pl.BlockSpec((g, n2, V7X_LANES), lambda bi, j: (j, 0, 0)),
                    pl.BlockSpec((g, n2, V7X_LANES), lambda bi, j: (j, 0, 0))] + const_specs
    else:
        args = [f.reshape(b, 1, 1, n2, B_WIDTH)] + consts
        in_specs = [pl.BlockSpec((1, 1, 1, n2, B_WIDTH), lambda bi, j: (bi, 0, 0, 0, 0))] + const_specs
        out_shape = jax.ShapeDtypeStruct((b, n2, B_WIDTH), _BF16)
        out_spec = pl.BlockSpec((1, n2, B_WIDTH), lambda bi, j: (bi, 0, 0))
    out = pl.pallas_call(
        functools.partial(_dft2_kernel, g=g, n2=n2, two_stage=two_stage, norm=norm),
        grid=(b, n1 // g),
        in_specs=in_specs,
        out_specs=out_spec,
        out_shape=out_shape,
        compiler_params=_params(2),
        name="dft_stage2",
    )(*args)
    return out.reshape(b, n, B_WIDTH)


def _merge_kernel(oa_ref, ob_ref, gb_ref, oc_ref, x_ref, mod_ref, gn_ref, wm_ref, wa_ref, wb_ref, wc_ref, wo_ref,
                  o_ref, *, d, sub):
    mod = mod_ref[0]
    for r0 in range(0, x_ref.shape[1], sub):
        rows = slice(r0, r0 + sub)
        x = x_ref[0, rows, :]
        hb = _modulated_norm(x, mod, gn_ref[...], d)
        branches = (jnp.dot(oa_ref[0, rows, :], wa_ref[...], preferred_element_type=_F32),
                    jnp.dot(ob_ref[0, rows, :] * gb_ref[0, rows, :], wb_ref[...], preferred_element_type=_F32),
                    jnp.dot(oc_ref[0, rows, :], wc_ref[...], preferred_element_type=_F32))
        y = None
        for j, yj in enumerate(branches):
            mj = _sigmoid(jnp.dot(hb, wm_ref[:, j * d:(j + 1) * d], preferred_element_type=_F32))
            y = mj * yj if y is None else y + mj * yj
        out = jnp.dot(y.astype(_BF16), wo_ref[...], preferred_element_type=_F32)
        o_ref[0, rows, :] = x + mod[:, 2 * d:] * out


def _merge_call(oa, ob, gb, oc, x, mod, gn, w_in_bf, wa, wb, wc, wo, *, tm, sub, layer):
    b, n, d = x.shape
    row = lambda w: pl.BlockSpec((1, tm, w), lambda bi, i: (bi, i, 0))
    of_layer = lambda *shape: pl.BlockSpec((None,) + shape, lambda bi, i: (layer,) + (0,) * len(shape))
    return pl.pallas_call(
        functools.partial(_merge_kernel, d=d, sub=sub),
        grid=(b, n // tm),
        in_specs=[row(A_WIDTH), row(B_WIDTH), row(B_WIDTH), row(C_WIDTH), row(d),
                  pl.BlockSpec((1, 1, 3 * d), lambda bi, i: (bi, 0, 0)),
                  pl.BlockSpec((1, d), lambda bi, i: (0, 0)),
                  of_layer(d, N_BRANCH * d),
                  of_layer(A_WIDTH, d), of_layer(B_WIDTH, d), of_layer(C_WIDTH, d), of_layer(d, d)],
        out_specs=row(d),
        out_shape=jax.ShapeDtypeStruct((b, n, d), _F32),
        compiler_params=_params(2),
        name="merge",
    )(oa, ob, gb, oc, x, mod, gn, w_in_bf, wa, wb, wc, wo)


def _rope_tables(n):
    n_rows = n // GRID_W
    freqs = ROPE_BASE ** (-jnp.arange(0, AXIS_DIM, 2, dtype=_F32) / AXIS_DIM)
    ang_r = jnp.arange(n_rows, dtype=_F32)[:, None] * freqs
    ang_c = jnp.arange(GRID_W, dtype=_F32)[:, None] * freqs

    def on_grid(fn):
        shape = (n_rows, GRID_W, AXIS_DIM // 2)
        r = jnp.broadcast_to(fn(ang_r)[:, None, :], shape)
        c = jnp.broadcast_to(fn(ang_c)[None, :, :], shape)
        return jnp.concatenate([r, r, c, c], axis=-1).reshape(n, A_QK_DIM)

    cos, sin = on_grid(jnp.cos), on_grid(jnp.sin)
    first_half = (jnp.arange(A_QK_DIM) % AXIS_DIM) < AXIS_DIM // 2
    sa = jnp.where(first_half, -sin, 0.0)
    sb = jnp.where(first_half, 0.0, sin)
    tile2 = lambda t: jnp.concatenate([t, t], axis=-1)
    return tile2(cos), tile2(sa), tile2(sb)


def _no_rope_tables(n):
    return (jnp.ones((n, V7X_LANES), _F32), jnp.zeros((n, V7X_LANES), _F32), jnp.zeros((n, V7X_LANES), _F32))


def kernel(x, c, ctx, c_ctx, w_ada, b_ada, g_norm, w_in, g_q, g_k, lam_q1, lam_k1, lam_q2, lam_k2, g_sub,
           w_f, b_f, ln_g, ln_b, w_s, b_s, w_br_a, w_br_b, w_br_c, w_out):
    b, n, d = x.shape
    n_ctx = ctx.shape[1]
    depth = w_in.shape[0]
    tm_lat = min(n, 512)
    tm_ctx = min(n_ctx, 512)

    n_rows = V7X_SUBLANES * pl.cdiv(b + 1, V7X_SUBLANES)
    rows = jnp.concatenate([c, c_ctx[None, :], jnp.zeros((n_rows - b - 1, d), _F32)], axis=0)
    mod_all = _mod_call(rows, w_ada, b_ada)

    rope_lat = _rope_tables(n)
    rope_ctx = _no_rope_tables(n_ctx)
    group_mean = np.kron(np.eye(2 * A_V_DIM // A_QK_DIM), np.full((A_QK_DIM, A_QK_DIM), 1.0 / A_QK_DIM))
    bmat = jnp.asarray(group_mean, _BF16)

    w_in_bf = jnp.concatenate(
        [w_in[:, :, OFF_MERGE:], jnp.zeros((depth, d, OFF_MERGE - N_BRANCH * d), w_in.dtype), w_in[:, :, :OFF_MERGE]],
        axis=-1).astype(_BF16)
    w_s_bf = w_s.astype(_BF16)
    merge_w = (w_in_bf, w_br_a.astype(_BF16), w_br_b.astype(_BF16), w_br_c.astype(_BF16), w_out.astype(_BF16))

    for l in range(depth):
        last = l == depth - 1
        lam_init = 0.8 - 0.6 * math.exp(-0.3 * l)
        lamv = jnp.stack([lam_q1[l], lam_k1[l], lam_q2[l], lam_k2[l]], axis=0)
        shared = dict(
            gn=g_norm[l][None, :], w_bf=w_in_bf,
            gq=jnp.tile(g_q[l], 2)[None, :], gk=jnp.tile(g_k[l], 2)[None, :], bmat=bmat,
            lng=ln_g[l][None, :], lnb=ln_b[l][None, :], ws_bf=w_s_bf,
            bsm=jnp.repeat(b_s[l].T, C_GROUP_DIM, axis=1), layer=l)
        gsub = g_sub[l][None, :]
        wf_bd = jax.scipy.linalg.block_diag(*[w_f[l, gi] for gi in range(B_GROUPS)]).astype(_BF16)
        bf_row = b_f[l].reshape(1, B_WIDTH)

        mod_lat = mod_all[l, :b][:, None, :]
        mod_ctx = jnp.broadcast_to(mod_all[l, b][None, None, :], (b, 1, 3 * d))

        ctx_proj = _inproj_call(ctx, mod_ctx, cos=rope_ctx[0], sa=rope_ctx[1], sb=rope_ctx[2], tm=tm_ctx,
                                sub=tm_ctx, kv_only=last, **shared)
        if last:
            ck, cvt = ctx_proj
        else:
            cq2, ck, cvt, cf, coc, cga, cgb = ctx_proj
        q2, k, vt, f, oc, ga, gb = _inproj_call(
            x, mod_lat, cos=rope_lat[0], sa=rope_lat[1], sb=rope_lat[2], tm=min(n, 2 * tm_lat), sub=tm_lat,
            **shared)

        oa = _attention(lamv, q2, ga, gsub, [(ck, cvt), (k, vt)], g_q[l], g_k[l], tq=tm_lat,
                        nq=min(4, n // tm_lat), lam_init=lam_init)
        ob = _fourier(f, wf_bd, bf_row)
        x_new = _merge_call(oa, ob, gb, oc, x, mod_lat, shared["gn"], *merge_w, tm=min(n, 2 * tm_lat), sub=tm_lat,
                            layer=l)

        if not last:
            coa = _attention(lamv, cq2, cga, gsub, [(ck, cvt)], g_q[l], g_k[l], tq=tm_ctx, nq=1,
                             lam_init=lam_init)
            cob = _fourier(cf, wf_bd, bf_row)
            ctx = _merge_call(coa, cob, cgb, coc, ctx, mod_ctx, shared["gn"], *merge_w, tm=tm_ctx, sub=tm_ctx,
                              layer=l)
        x = x_new
    return x
```

```python
import functools
import math

import numpy as np
import jax
import jax.numpy as jnp
from jax import lax
from jax.experimental import pallas as pl
from jax.experimental.pallas import tpu as pltpu

A_HEADS = 4
A_QK_DIM = 64
A_V_DIM = 2 * A_QK_DIM
A_WIDTH = A_HEADS * A_V_DIM
B_GROUPS = 4
B_GROUP_DIM = 64
B_WIDTH = B_GROUPS * B_GROUP_DIM
C_GROUPS = 4
C_GROUP_DIM = 64
C_WIDTH = C_GROUPS * C_GROUP_DIM
CHUNK = 128
GRID_W = 64
ROPE_BASE = 10000.0
AXIS_DIM = A_QK_DIM // 2
EPS = 1e-6

OFF_Q = 0
OFF_K = OFF_Q + A_WIDTH
OFF_V = OFF_K + A_WIDTH
OFF_F = OFF_V + A_WIDTH
OFF_U = OFF_F + B_WIDTH
OFF_VC = OFF_U + C_WIDTH
OFF_GATE = OFF_VC + C_WIDTH
GATE_WIDTH = A_WIDTH + B_WIDTH + C_WIDTH
OFF_MERGE = OFF_GATE + GATE_WIDTH

V7X_LANES = 128
V7X_SUBLANES = 8
V7X_BF16_SUBLANES = 16
V7X_MXU_DIM = 256
V7X_VMEM_LIMIT_BYTES = 56 * 1024 * 1024

LOG2E = math.log2(math.e)
Q_SCALE = (A_QK_DIM ** -0.5) * LOG2E

_F32 = jnp.float32
_BF16 = jnp.bfloat16
_NT = (((1,), (1,)), ((), ()))


def _sigmoid(x):
    return 1.0 / (1.0 + jnp.exp(-x))


def _params(n_axes):
    return pltpu.CompilerParams(dimension_semantics=("arbitrary",) * n_axes,
                                vmem_limit_bytes=V7X_VMEM_LIMIT_BYTES)


def _mod_kernel(c_ref, w_ref, b_ref, o_ref):
    cc = c_ref[...]
    s = cc * _sigmoid(cc)
    o_ref[0] = jnp.dot(s, w_ref[0], preferred_element_type=_F32) + b_ref[0]


def _mod_call(rows, w_ada, b_ada):
    depth, d, d3 = w_ada.shape
    nb = d3 // d
    return pl.pallas_call(
        _mod_kernel,
        grid=(depth, nb),
        in_specs=[pl.BlockSpec((rows.shape[0], d), lambda l, j: (0, 0)),
                  pl.BlockSpec((1, d, d), lambda l, j: (l, 0, j)),
                  pl.BlockSpec((1, 1, d), lambda l, j: (l, 0, j))],
        out_specs=pl.BlockSpec((1, rows.shape[0], d), lambda l, j: (l, 0, j)),
        out_shape=jax.ShapeDtypeStruct((depth, rows.shape[0], d3), _F32),
        compiler_params=_params(2),
        name="adaln_mod",
    )(rows, w_ada, b_ada.reshape(depth, 1, d3))


def _modulated_norm(x, mod, gn, d):
    ms = jnp.mean(x * x, axis=-1, keepdims=True)
    h = (x * lax.rsqrt(ms + EPS) * gn) * (1.0 + mod[:, d:2 * d]) + mod[:, :d]
    return h.astype(_BF16)


def _inproj_kernel(x_ref, mod_ref, gn_ref, w_ref, gq_ref, gk_ref, cos_ref, sa_ref, sb_ref,
                   bmat_ref, lng_ref, lnb_ref, ws_ref, bsm_ref, *out_refs, sub, d, kv_only):
    if kv_only:
        k_ref, vt_ref = out_refs
    else:
        q2_ref, k_ref, vt_ref, f_ref, oc_ref, ga_ref, gb_ref = out_refs
    pair = 2 * A_V_DIM
    bmat = bmat_ref[...]
    lo = (lax.broadcasted_iota(jnp.int32, (sub, pair), 1) % A_V_DIM) < A_QK_DIM
    lo_c = lax.broadcasted_iota(jnp.int32, (CHUNK, V7X_LANES), 1) < C_GROUP_DIM
    bsm = bsm_ref[...]
    twice = lambda t: jnp.concatenate([t, t], axis=1)
    gq = twice(gq_ref[...])
    gk = twice(gk_ref[...])

    for si in range(x_ref.shape[1] // sub):
        rows = slice(si * sub, (si + 1) * sub)
        hb = _modulated_norm(x_ref[0, rows, :], mod_ref[0], gn_ref[...], d)

        def proj(off, width, hb=hb):
            return jnp.dot(hb, w_ref[:, off:off + width], preferred_element_type=_F32)

        cos = twice(cos_ref[rows, :])
        sa = twice(sa_ref[rows, :])
        sb = twice(sb_ref[rows, :])

        def qk_heads(p_2h, g, cos=cos, sa=sa, sb=sb):
            msq = jnp.dot((p_2h * p_2h).astype(_BF16), bmat, preferred_element_type=_F32)
            t = p_2h * lax.rsqrt(msq + EPS) * g
            return (t * cos + pltpu.roll(t, pair - AXIS_DIM // 2, 1) * sa
                    + pltpu.roll(t, AXIS_DIM // 2, 1) * sb)

        if not kv_only:
            pq = proj(OFF_Q, A_WIDTH)
            for hp in range(A_HEADS // 2):
                t = qk_heads(pq[:, hp * pair:(hp + 1) * pair], gq) * Q_SCALE
                t_lo = jnp.where(lo, t, 0.0).astype(_BF16)
                t_hi = jnp.where(lo, 0.0, t).astype(_BF16)
                for h2 in range(2):
                    q2_ref[0, 2 * hp + h2, 0, rows, :] = t_lo[:, h2 * A_V_DIM:(h2 + 1) * A_V_DIM]
                    q2_ref[0, 2 * hp + h2, 1, rows, :] = t_hi[:, h2 * A_V_DIM:(h2 + 1) * A_V_DIM]

        pk = proj(OFF_K, A_WIDTH)
        for hp in range(A_HEADS // 2):
            t = qk_heads(pk[:, hp * pair:(hp + 1) * pair], gk).astype(_BF16)
            for h2 in range(2):
                k_ref[0, 2 * hp + h2, rows, :] = t[:, h2 * A_V_DIM:(h2 + 1) * A_V_DIM]

        pv = proj(OFF_V, A_WIDTH)
        for hh in range(A_HEADS):
            vt_ref[0, hh, si] = pv[:, hh * A_V_DIM:(hh + 1) * A_V_DIM].T.astype(_BF16)
        if kv_only:
            continue

        f_ref[0, rows, :] = proj(OFF_F, B_WIDTH)

        u = proj(OFF_U, C_WIDTH)
        vc = proj(OFF_VC, C_WIDTH)
        mu = jnp.mean(vc, axis=-1, keepdims=True)
        dv = vc - mu
        var = jnp.mean(dv * dv, axis=-1, keepdims=True)
        vn = (dv * lax.rsqrt(var + EPS) * lng_ref[...] + lnb_ref[...]).astype(_BF16)

        pg = proj(OFF_GATE, GATE_WIDTH)
        gate = pg * _sigmoid(pg)
        ga_ref[0, rows, :] = gate[:, :A_WIDTH].astype(_BF16)
        gb_ref[0, rows, :] = gate[:, A_WIDTH:A_WIDTH + B_WIDTH].astype(_BF16)
        gate_c = gate[:, A_WIDTH + B_WIDTH:]

        for t in range(sub // CHUNK):
            crows = slice(t * CHUNK, (t + 1) * CHUNK)
            vl = vn[crows, :V7X_LANES]
            vr = vn[crows, V7X_LANES:]
            s_l = jnp.where(lo_c, jnp.dot(ws_ref[0], vl, preferred_element_type=_F32),
                            jnp.dot(ws_ref[1], vl, preferred_element_type=_F32))
            s_r = jnp.where(lo_c, jnp.dot(ws_ref[2], vr, preferred_element_type=_F32),
                            jnp.dot(ws_ref[3], vr, preferred_element_type=_F32))
            s = jnp.concatenate([s_l, s_r], axis=1) + bsm
            oc_ref[0, si * sub + t * CHUNK:si * sub + (t + 1) * CHUNK, :] = (
                u[crows] * s * gate_c[crows]).astype(_BF16)


def _inproj_call(x, mod, gn, w_bf, gq, gk, cos, sa, sb, bmat, lng, lnb, ws_bf, bsm, *, tm, sub, layer,
                 kv_only=False):
    b, n, d = x.shape
    nt = n // tm
    full = lambda *shape: pl.BlockSpec(shape, lambda bi, i: (0,) * len(shape))
    of_layer = lambda *shape: pl.BlockSpec((None,) + shape, lambda bi, i: (layer,) + (0,) * len(shape))
    in_specs = [
        pl.BlockSpec((1, tm, d), lambda bi, i: (bi, i, 0)),
        pl.BlockSpec((1, 1, 3 * d), lambda bi, i: (bi, 0, 0)),
        full(1, d),
        of_layer(d, w_bf.shape[-1]),
        full(1, V7X_LANES),
        full(1, V7X_LANES),
        pl.BlockSpec((tm, V7X_LANES), lambda bi, i: (i, 0)),
        pl.BlockSpec((tm, V7X_LANES), lambda bi, i: (i, 0)),
        pl.BlockSpec((tm, V7X_LANES), lambda bi, i: (i, 0)),
        full(2 * A_V_DIM, 2 * A_V_DIM),
        full(1, C_WIDTH),
        full(1, C_WIDTH),
        of_layer(C_GROUPS, CHUNK, CHUNK),
        full(CHUNK, C_WIDTH),
    ]
    out_shape = [
        jax.ShapeDtypeStruct((b, A_HEADS, 2, n, A_V_DIM), _BF16),
        jax.ShapeDtypeStruct((b, A_HEADS, n, A_V_DIM), _BF16),
        jax.ShapeDtypeStruct((b, A_HEADS, n // sub, A_V_DIM, sub), _BF16),
        jax.ShapeDtypeStruct((b, n, B_WIDTH), _F32),
        jax.ShapeDtypeStruct((b, n, C_WIDTH), _BF16),
        jax.ShapeDtypeStruct((b, n, A_WIDTH), _BF16),
        jax.ShapeDtypeStruct((b, n, B_WIDTH), _BF16),
    ]
    out_specs = [
        pl.BlockSpec((1, A_HEADS, 2, tm, A_V_DIM), lambda bi, i: (bi, 0, 0, i, 0)),
        pl.BlockSpec((1, A_HEADS, tm, A_V_DIM), lambda bi, i: (bi, 0, i, 0)),
        pl.BlockSpec((1, A_HEADS, tm // sub, A_V_DIM, sub), lambda bi, i: (bi, 0, i, 0, 0)),
        pl.BlockSpec((1, tm, B_WIDTH), lambda bi, i: (bi, i, 0)),
        pl.BlockSpec((1, tm, C_WIDTH), lambda bi, i: (bi, i, 0)),
        pl.BlockSpec((1, tm, A_WIDTH), lambda bi, i: (bi, i, 0)),
        pl.BlockSpec((1, tm, B_WIDTH), lambda bi, i: (bi, i, 0)),
    ]
    if kv_only:
        out_shape, out_specs = out_shape[1:3], out_specs[1:3]
    return pl.pallas_call(
        functools.partial(_inproj_kernel, sub=sub, d=d, kv_only=kv_only),
        grid=(b, nt),
        in_specs=in_specs,
        out_specs=out_specs,
        out_shape=out_shape,
        compiler_params=_params(2),
        name="inproj",
    )(x, mod, gn, w_bf, gq, gk, cos, sa, sb, bmat, lng, lnb, ws_bf, bsm)


def _attn_kernel(*refs, tq, nq, n_src, tiles, lam_init, bounded):
    lam_ref, q2_ref, ga_ref, gsub_ref = refs[:4]
    src_refs = refs[4:4 + 2 * n_src]
    out_ref = refs[4 + 2 * n_src]
    p_sc, m_sc, l_sc, acc_sc = refs[5 + 2 * n_src:]

    def queries(u):
        rows = slice(u * tq, (u + 1) * tq)
        return jnp.concatenate([q2_ref[0, 0, 0, rows, :], q2_ref[0, 0, 1, rows, :]], axis=0)

    l_sc[...] = jnp.zeros(l_sc.shape, _F32)
    acc_sc[...] = jnp.zeros(acc_sc.shape, _F32)
    lv = lam_ref[...]
    lam = (jnp.exp(jnp.sum(lv[0:1] * lv[1:2], axis=-1, keepdims=True))
           - jnp.exp(jnp.sum(lv[2:3] * lv[3:4], axis=-1, keepdims=True)) + lam_init)

    def finalize(u):
        inv = 1.0 / l_sc[u]
        acc = acc_sc[u]
        o = acc[:, :tq] * inv[:, :tq] - lam * (acc[:, tq:] * inv[:, tq:])
        ms = jnp.mean(o * o, axis=0, keepdims=True)
        on = o * lax.rsqrt(ms + EPS)
        rows = slice(u * tq, (u + 1) * tq)
        out = on.T * (gsub_ref[...] * (1.0 - lam_init)) * ga_ref[0, rows, :].astype(_F32)
        out_ref[0, rows, :] = out.astype(_BF16)

    def tile_len(si):
        return src_refs[2 * si + 1].shape[-1]

    def k_tile(si, t):
        tk = tile_len(si)
        if isinstance(t, int):
            return src_refs[2 * si][0, 0, t * tk:(t + 1) * tk, :]
        return src_refs[2 * si][0, 0, pl.ds(pl.multiple_of(t * tk, tk), tk), :]

    def vt_tile(si, t):
        return src_refs[2 * si + 1][0, 0, t]

    if bounded:
        def exponentials(slot, u, k_t):
            p = jnp.exp2(lax.dot_general(k_t, queries(u), _NT, preferred_element_type=_F32))
            l_sc[u] += jnp.sum(p, axis=0, keepdims=True)
            p_sc[slot, :k_t.shape[0], :] = p.astype(_BF16)

        def weighted_values(slot, u, vt_t):
            acc_sc[u] += jnp.dot(vt_t, p_sc[slot, :vt_t.shape[1], :], preferred_element_type=_F32)

        stages = [(u, si, t) for u in range(nq) for si in range(n_src) for t in range(tiles[si])]
        exponentials(0, stages[0][0], k_tile(*stages[0][1:]))
        for j, (u, si, t) in enumerate(stages):
            if j + 1 < len(stages):
                exponentials((j + 1) % 2, stages[j + 1][0], k_tile(*stages[j + 1][1:]))
            weighted_values(j % 2, u, vt_tile(si, t))
            if j + 1 == len(stages) or stages[j + 1][0] != u:
                finalize(u)
    else:
        for u in range(nq):
            m_sc[...] = jnp.full(m_sc.shape, -jnp.inf, _F32)

            def step(k_t, vt_t, u=u):
                s = lax.dot_general(k_t, queries(u), _NT, preferred_element_type=_F32)
                m_prev = m_sc[...]
                m_new = jnp.maximum(m_prev, jnp.max(s, axis=0, keepdims=True))
                alpha = jnp.exp2(m_prev - m_new)
                p = jnp.exp2(s - m_new)
                l_sc[u] = alpha * l_sc[u] + jnp.sum(p, axis=0, keepdims=True)
                acc_sc[u] = alpha * acc_sc[u] + jnp.dot(vt_t, p.astype(_BF16), preferred_element_type=_F32)
                m_sc[...] = m_new

            for si in range(n_src):
                if tiles[si] == 1:
                    step(k_tile(si, 0), vt_tile(si, 0))
                else:
                    def body(t, carry, si=si, step=step):
                        step(k_tile(si, t), vt_tile(si, t))
                        return carry
                    lax.fori_loop(0, tiles[si], body, 0)
            finalize(u)


def _attn_call(lamv, q2, ga, gsub, sources, *, tq, nq, lam_init, bounded):
    b, _, _, n, _ = q2.shape
    rows = nq * tq
    in_specs = [
        pl.BlockSpec(lamv.shape, lambda bi, hi, i: (0, 0)),
        pl.BlockSpec((1, 1, 2, rows, A_V_DIM), lambda bi, hi, i: (bi, hi, 0, i, 0)),
        pl.BlockSpec((1, rows, A_V_DIM), lambda bi, hi, i: (bi, i, hi)),
        pl.BlockSpec((1, A_V_DIM), lambda bi, hi, i: (0, 0)),
    ]
    args = [lamv, q2, ga, gsub]
    tiles = []
    for k_s, vt_s in sources:
        ls = k_s.shape[2]
        nt, _, tk = vt_s.shape[2:]
        tiles.append(nt)
        in_specs.append(pl.BlockSpec((1, 1, ls, A_V_DIM), lambda bi, hi, i: (bi, hi, 0, 0)))
        in_specs.append(pl.BlockSpec((1, 1, nt, A_V_DIM, tk), lambda bi, hi, i: (bi, hi, 0, 0, 0)))
        args += [k_s, vt_s]
    tk_max = max(vt_s.shape[-1] for _, vt_s in sources)
    return pl.pallas_call(
        functools.partial(_attn_kernel, tq=tq, nq=nq, n_src=len(sources), tiles=tuple(tiles),
                          lam_init=lam_init, bounded=bounded),
        grid=(b, A_HEADS, n // rows),
        in_specs=in_specs,
        out_specs=pl.BlockSpec((1, rows, A_V_DIM), lambda bi, hi, i: (bi, i, hi)),
        out_shape=jax.ShapeDtypeStruct((b, n, A_WIDTH), _BF16),
        scratch_shapes=[pltpu.VMEM((2, tk_max, 2 * tq), _BF16),
                        pltpu.VMEM((1, 2 * tq), _F32),
                        pltpu.VMEM((nq, 1, 2 * tq), _F32),
                        pltpu.VMEM((nq, A_V_DIM, 2 * tq), _F32)],
        compiler_params=_params(3),
        name="diff_attn" if bounded else "diff_attn_general",
    )(*args)


SAFE_EXP2_RANGE = 60.0


def _attention(lamv, q2, ga, gsub, sources, gq, gk, *, tq, nq, lam_init):
    bound = A_QK_DIM * Q_SCALE * jnp.max(jnp.abs(gq)) * jnp.max(jnp.abs(gk))
    call = lambda bounded: functools.partial(_attn_call, tq=tq, nq=nq, lam_init=lam_init, bounded=bounded)
    return lax.cond(bound <= SAFE_EXP2_RANGE, call(True), call(False), lamv, q2, ga, gsub, sources)


def _dft1_kernel(f1_ref, x_ref, o_ref, *, rows_per_step):
    n1 = x_ref.shape[1]
    x = x_ref[0].reshape(n1, rows_per_step * B_WIDTH).astype(_BF16)
    z = jnp.dot(f1_ref[...], x, preferred_element_type=_F32)
    o_ref[0] = z.reshape(2 * n1, rows_per_step, B_WIDTH).astype(_BF16)


def _dft1_call(f1, x4d, *, rows_per_step):
    b, n1, n2, _ = x4d.shape
    return pl.pallas_call(
        functools.partial(_dft1_kernel, rows_per_step=rows_per_step),
        grid=(b, n2 // rows_per_step),
        in_specs=[pl.BlockSpec((2 * n1, n1), lambda bi, j: (0, 0)),
                  pl.BlockSpec((1, n1, rows_per_step, B_WIDTH), lambda bi, j: (bi, 0, j, 0))],
        out_specs=pl.BlockSpec((1, 2 * n1, rows_per_step, B_WIDTH), lambda bi, j: (bi, 0, j, 0)),
        out_shape=jax.ShapeDtypeStruct((b, 2 * n1, n2, B_WIDTH), _BF16),
        compiler_params=_params(2),
        name="dft_stage1",
    )(f1, x4d)


def _dft2_kernel(*refs, g, n2, two_stage, norm):
    if two_stage:
        ar_ref, ai_ref, twc_ref, tws_ref, w2c_ref, mix_ref, wf_ref, bf_ref, out_ref = refs
        ar = ar_ref[0, 0].reshape(g * n2, B_WIDTH).astype(_F32)
        ai = ai_ref[0, 0].reshape(g * n2, B_WIDTH).astype(_F32)
        tc = twc_ref[...].reshape(g * n2, V7X_LANES)
        ts = tws_ref[...].reshape(g * n2, V7X_LANES)
        c2 = jnp.concatenate([tc, tc], axis=1)
        s2 = jnp.concatenate([ts, ts], axis=1)
        a = jnp.concatenate([ar * c2 - ai * s2, ar * s2 + ai * c2], axis=1).astype(_BF16)
        uv = jnp.dot(a, mix_ref[...], preferred_element_type=_F32)
    else:
        ar_ref, w2c_ref, mix_ref, wf_ref, bf_ref, out_ref = refs
        uv = jnp.dot(ar_ref[0, 0, 0].astype(_BF16), mix_ref[:B_WIDTH, :], preferred_element_type=_F32)
    u = uv[:, :B_WIDTH].astype(_BF16)
    v = uv[:, B_WIDTH:].astype(_BF16)
    w2c = w2c_ref[...]
    frs = []
    for j in range(g):
        rows = slice(j * n2, (j + 1) * n2)
        frs.append(jnp.dot(w2c, jnp.concatenate([u[rows], v[rows]], axis=0), preferred_element_type=_F32))
    fr = jnp.concatenate(frs, axis=0) * norm
    y = jnp.dot(fr.astype(_BF16), wf_ref[...], preferred_element_type=_F32) + bf_ref[...]
    if two_stage:
        out_ref[0] = jnp.swapaxes(y.reshape(g, n2, B_WIDTH), 0, 1).astype(_BF16)
    else:
        out_ref[0] = y.astype(_BF16)


def _dft_tables(n1, n2):
    n = n1 * n2
    k1 = np.arange(n1)
    a1 = 2.0 * np.pi * ((k1[:, None] * k1[None, :]) % n1) / n1
    f1 = np.concatenate([np.cos(a1), np.sin(a1)], axis=0)
    k2 = np.arange(n2)
    a2 = 2.0 * np.pi * ((k2[:, None] * k2[None, :]) % n2) / n2
    w2c = np.concatenate([np.cos(a2), -np.sin(a2)], axis=1)
    gch = np.arange(B_GROUP_DIM)
    ag = 2.0 * np.pi * ((gch[:, None] * gch[None, :]) % B_GROUP_DIM) / B_GROUP_DIM
    eye = np.eye(B_GROUPS)
    cg = np.kron(eye, np.cos(ag))
    sg = np.kron(eye, np.sin(ag))
    mix = np.block([[cg, sg], [-sg, cg]])
    to_bf16 = lambda t: jnp.asarray(t, _F32).astype(_BF16)
    return to_bf16(f1), to_bf16(w2c), to_bf16(mix), 1.0 / math.sqrt(n * B_GROUP_DIM)


def _twiddles(n1, n2):
    k1 = jnp.arange(n1, dtype=jnp.int32)[:, None]
    m2 = jnp.arange(n2, dtype=jnp.int32)[None, :]
    ang = (k1 * m2).astype(_F32) * (2.0 * math.pi / (n1 * n2))
    shape = (n1, n2, V7X_LANES)
    return (jnp.broadcast_to(jnp.cos(ang)[:, :, None], shape), jnp.broadcast_to(jnp.sin(ang)[:, :, None], shape))


def _fourier(f, wf_bd, bf_row):
    b, n, _ = f.shape
    two_stage = n > 2 * V7X_LANES
    n2 = V7X_LANES if two_stage else n
    n1 = n // n2
    g = min(n1, V7X_BF16_SUBLANES)
    f1, w2c, mix, norm = _dft_tables(n1, n2)
    full = lambda *shape: pl.BlockSpec(shape, lambda bi, j: (0,) * len(shape))
    consts = [w2c, mix, wf_bd, bf_row]
    const_specs = [full(*w2c.shape), full(*mix.shape), full(B_WIDTH, B_WIDTH), full(1, B_WIDTH)]
    if two_stage:
        a = _dft1_call(f1, f.reshape(b, n1, n2, B_WIDTH), rows_per_step=2 * V7X_BF16_SUBLANES)
        a5 = a.reshape(b, 2, n1, n2, B_WIDTH)
        out_shape = jax.ShapeDtypeStruct((b, n2, n1, B_WIDTH), _BF16)
        out_spec = pl.BlockSpec((1, n2, g, B_WIDTH), lambda bi, j: (bi, 0, j, 0))
        twc, tws = _twiddles(n1, n2)
        args = [a5, a5, twc, tws] + consts
        in_specs = [pl.BlockSpec((1, 1, g, n2, B_WIDTH), lambda bi, j: (bi, 0, j, 0, 0)),
                    pl.BlockSpec((1, 1, g, n2, B_WIDTH), lambda bi, j: (bi, 1, j, 0, 0)),
                    pl.BlockSpec((g, n2, V7X_LANES), lambda bi, j: (j, 0, 0)),
                    pl.BlockSpec((g, n2, V7X_LANES), lambda bi, j: (j, 0, 0))] + const_specs
    else:
        args = [f.reshape(b, 1, 1, n2, B_WIDTH)] + consts
        in_specs = [pl.BlockSpec((1, 1, 1, n2, B_WIDTH), lambda bi, j: (bi, 0, 0, 0, 0))] + const_specs
        out_shape = jax.ShapeDtypeStruct((b, n2, B_WIDTH), _BF16)
        out_spec = pl.BlockSpec((1, n2, B_WIDTH), lambda bi, j: (bi, 0, 0))
    out = pl.pallas_call(
        functools.partial(_dft2_kernel, g=g, n2=n2, two_stage=two_stage, norm=norm),
        grid=(b, n1 // g),
        in_specs=in_specs,
        out_specs=out_spec,
        out_shape=out_shape,
        compiler_params=_params(2),
        name="dft_stage2",
    )(*args)
    return out.reshape(b, n, B_WIDTH)


def _merge_kernel(oa_ref, ob_ref, gb_ref, oc_ref, x_ref, mod_ref, gn_ref, wm_ref, wa_ref, wb_ref, wc_ref, wo_ref,
                  o_ref, *, d, sub):
    mod = mod_ref[0]
    for r0 in range(0, x_ref.shape[1], sub):
        rows = slice(r0, r0 + sub)
        x = x_ref[0, rows, :]
        hb = _modulated_norm(x, mod, gn_ref[...], d)
        branches = (jnp.dot(oa_ref[0, rows, :], wa_ref[...], preferred_element_type=_F32),
                    jnp.dot(ob_ref[0, rows, :] * gb_ref[0, rows, :], wb_ref[...], preferred_element_type=_F32),
                    jnp.dot(oc_ref[0, rows, :], wc_ref[...], preferred_element_type=_F32))
        y = None
        for j, yj in enumerate(branches):
            mj = _sigmoid(jnp.dot(hb, wm_ref[:, OFF_MERGE + j * d:OFF_MERGE + (j + 1) * d],
                                  preferred_element_type=_F32))
            y = mj * yj if y is None else y + mj * yj
        out = jnp.dot(y.astype(_BF16), wo_ref[...], preferred_element_type=_F32)
        o_ref[0, rows, :] = x + mod[:, 2 * d:] * out


def _merge_call(oa, ob, gb, oc, x, mod, gn, w_in_bf, wa, wb, wc, wo, *, tm, sub, layer):
    b, n, d = x.shape
    row = lambda w: pl.BlockSpec((1, tm, w), lambda bi, i: (bi, i, 0))
    of_layer = lambda *shape: pl.BlockSpec((None,) + shape, lambda bi, i: (layer,) + (0,) * len(shape))
    return pl.pallas_call(
        functools.partial(_merge_kernel, d=d, sub=sub),
        grid=(b, n // tm),
        in_specs=[row(A_WIDTH), row(B_WIDTH), row(B_WIDTH), row(C_WIDTH), row(d),
                  pl.BlockSpec((1, 1, 3 * d), lambda bi, i: (bi, 0, 0)),
                  pl.BlockSpec((1, d), lambda bi, i: (0, 0)),
                  of_layer(d, w_in_bf.shape[-1]), of_layer(A_WIDTH, d), of_layer(B_WIDTH, d),
                  of_layer(C_WIDTH, d), of_layer(d, d)],
        out_specs=row(d),
        out_shape=jax.ShapeDtypeStruct((b, n, d), _F32),
        compiler_params=_params(2),
        name="merge",
    )(oa, ob, gb, oc, x, mod, gn, w_in_bf, wa, wb, wc, wo)


def _rope_tables(n):
    n_rows = n // GRID_W
    freqs = ROPE_BASE ** (-jnp.arange(0, AXIS_DIM, 2, dtype=_F32) / AXIS_DIM)
    ang_r = jnp.arange(n_rows, dtype=_F32)[:, None] * freqs
    ang_c = jnp.arange(GRID_W, dtype=_F32)[:, None] * freqs

    def on_grid(fn):
        shape = (n_rows, GRID_W, AXIS_DIM // 2)
        r = jnp.broadcast_to(fn(ang_r)[:, None, :], shape)
        c = jnp.broadcast_to(fn(ang_c)[None, :, :], shape)
        return jnp.concatenate([r, r, c, c], axis=-1).reshape(n, A_QK_DIM)

    cos, sin = on_grid(jnp.cos), on_grid(jnp.sin)
    first_half = (jnp.arange(A_QK_DIM) % AXIS_DIM) < AXIS_DIM // 2
    sa = jnp.where(first_half, -sin, 0.0)
    sb = jnp.where(first_half, 0.0, sin)
    tile2 = lambda t: jnp.concatenate([t, t], axis=-1)
    return tile2(cos), tile2(sa), tile2(sb)


def _no_rope_tables(n):
    return (jnp.ones((n, V7X_LANES), _F32), jnp.zeros((n, V7X_LANES), _F32), jnp.zeros((n, V7X_LANES), _F32))


def kernel(x, c, ctx, c_ctx, w_ada, b_ada, g_norm, w_in, g_q, g_k, lam_q1, lam_k1, lam_q2, lam_k2, g_sub,
           w_f, b_f, ln_g, ln_b, w_s, b_s, w_br_a, w_br_b, w_br_c, w_out):
    b, n, d = x.shape
    n_ctx = ctx.shape[1]
    depth = w_in.shape[0]
    tm_lat = min(n, 512)
    tm_ctx = min(n_ctx, 512)

    n_rows = V7X_SUBLANES * pl.cdiv(b + 1, V7X_SUBLANES)
    rows = jnp.concatenate([c, c_ctx[None, :], jnp.zeros((n_rows - b - 1, d), _F32)], axis=0)
    mod_all = _mod_call(rows, w_ada, b_ada)

    rope_lat = _rope_tables(n)
    rope_ctx = _no_rope_tables(n_ctx)
    group_mean = np.kron(np.eye(2 * A_V_DIM // A_QK_DIM), np.full((A_QK_DIM, A_QK_DIM), 1.0 / A_QK_DIM))
    bmat = jnp.asarray(group_mean, _BF16)

    w_in_bf = w_in.astype(_BF16)
    w_s_bf = w_s.astype(_BF16)
    merge_w = (w_in_bf, w_br_a.astype(_BF16), w_br_b.astype(_BF16), w_br_c.astype(_BF16), w_out.astype(_BF16))

    for l in range(depth):
        last = l == depth - 1
        lam_init = 0.8 - 0.6 * math.exp(-0.3 * l)
        lamv = jnp.stack([lam_q1[l], lam_k1[l], lam_q2[l], lam_k2[l]], axis=0)
        shared = dict(
            gn=g_norm[l][None, :], w_bf=w_in_bf,
            gq=jnp.tile(g_q[l], 2)[None, :], gk=jnp.tile(g_k[l], 2)[None, :], bmat=bmat,
            lng=ln_g[l][None, :], lnb=ln_b[l][None, :], ws_bf=w_s_bf,
            bsm=jnp.repeat(b_s[l].T, C_GROUP_DIM, axis=1), layer=l)
        gsub = g_sub[l][None, :]
        wf_bd = jax.scipy.linalg.block_diag(*[w_f[l, gi] for gi in range(B_GROUPS)]).astype(_BF16)
        bf_row = b_f[l].reshape(1, B_WIDTH)

        mod_lat = mod_all[l, :b][:, None, :]
        mod_ctx = jnp.broadcast_to(mod_all[l, b][None, None, :], (b, 1, 3 * d))

        ctx_proj = _inproj_call(ctx, mod_ctx, cos=rope_ctx[0], sa=rope_ctx[1], sb=rope_ctx[2], tm=tm_ctx,
                                sub=tm_ctx, kv_only=last, **shared)
        if last:
            ck, cvt = ctx_proj
        else:
            cq2, ck, cvt, cf, coc, cga, cgb = ctx_proj
        q2, k, vt, f, oc, ga, gb = _inproj_call(
            x, mod_lat, cos=rope_lat[0], sa=rope_lat[1], sb=rope_lat[2], tm=min(n, 2 * tm_lat), sub=tm_lat,
            **shared)

        oa = _attention(lamv, q2, ga, gsub, [(ck, cvt), (k, vt)], g_q[l], g_k[l], tq=tm_lat,
                        nq=min(4, n // tm_lat), lam_init=lam_init)
        ob = _fourier(f, wf_bd, bf_row)
        x_new = _merge_call(oa, ob, gb, oc, x, mod_lat, shared["gn"], *merge_w, tm=min(n, 2 * tm_lat), sub=tm_lat,
                            layer=l)

        if not last:
            coa = _attention(lamv, cq2, cga, gsub, [(ck, cvt)], g_q[l], g_k[l], tq=tm_ctx, nq=1,
                             lam_init=lam_init)
            cob = _fourier(cf, wf_bd, bf_row)
            ctx = _merge_call(coa, cob, cgb, coc, ctx, mod_ctx, shared["gn"], *merge_w, tm=tm_ctx, sub=tm_ctx,
                              layer=l)
        x = x_new
    return x
```

```python
import functools
import math

import numpy as np
import jax
import jax.numpy as jnp
from jax import lax
from jax.experimental import pallas as pl
from jax.experimental.pallas import tpu as pltpu

A_HEADS = 4
A_QK_DIM = 64
A_V_DIM = 2 * A_QK_DIM
A_WIDTH = A_HEADS * A_V_DIM
B_GROUPS = 4
B_GROUP_DIM = 64
B_WIDTH = B_GROUPS * B_GROUP_DIM
C_GROUPS = 4
C_GROUP_DIM = 64
C_WIDTH = C_GROUPS * C_GROUP_DIM
CHUNK = 128
GRID_W = 64
ROPE_BASE = 10000.0
AXIS_DIM = A_QK_DIM // 2
EPS = 1e-6

OFF_Q = 0
OFF_K = OFF_Q + A_WIDTH
OFF_V = OFF_K + A_WIDTH
OFF_F = OFF_V + A_WIDTH
OFF_U = OFF_F + B_WIDTH
OFF_VC = OFF_U + C_WIDTH
OFF_GATE = OFF_VC + C_WIDTH
GATE_WIDTH = A_WIDTH + B_WIDTH + C_WIDTH
OFF_MERGE = OFF_GATE + GATE_WIDTH

V7X_LANES = 128
V7X_SUBLANES = 8
V7X_BF16_SUBLANES = 16
V7X_MXU_DIM = 256
V7X_VMEM_LIMIT_BYTES = 56 * 1024 * 1024

LOG2E = math.log2(math.e)
Q_SCALE = (A_QK_DIM ** -0.5) * LOG2E

_F32 = jnp.float32
_BF16 = jnp.bfloat16
_NT = (((1,), (1,)), ((), ()))


def _sigmoid(x):
    return 0.5 * jnp.tanh(0.5 * x) + 0.5


def _params(n_axes):
    return pltpu.CompilerParams(dimension_semantics=("arbitrary",) * n_axes,
                                vmem_limit_bytes=V7X_VMEM_LIMIT_BYTES)


def _mod_kernel(c_ref, w_ref, b_ref, o_ref):
    cc = c_ref[...]
    s = cc * _sigmoid(cc)
    o_ref[0] = jnp.dot(s, w_ref[0], preferred_element_type=_F32) + b_ref[0]


def _mod_call(rows, w_ada, b_ada):
    depth, d, d3 = w_ada.shape
    nb = d3 // d
    return pl.pallas_call(
        _mod_kernel,
        grid=(depth, nb),
        in_specs=[pl.BlockSpec((rows.shape[0], d), lambda l, j: (0, 0)),
                  pl.BlockSpec((1, d, d), lambda l, j: (l, 0, j)),
                  pl.BlockSpec((1, 1, d), lambda l, j: (l, 0, j))],
        out_specs=pl.BlockSpec((1, rows.shape[0], d), lambda l, j: (l, 0, j)),
        out_shape=jax.ShapeDtypeStruct((depth, rows.shape[0], d3), _F32),
        compiler_params=_params(2),
        name="adaln_mod",
    )(rows, w_ada, b_ada.reshape(depth, 1, d3))


def _modulated_norm(x, mod, gn, d):
    ms = jnp.mean(x * x, axis=-1, keepdims=True)
    h = (x * lax.rsqrt(ms + EPS) * gn) * (1.0 + mod[:, d:2 * d]) + mod[:, :d]
    return h.astype(_BF16)


def _inproj_kernel(x_ref, mod_ref, gn_ref, w_ref, gq_ref, gk_ref, cos_ref, sa_ref, sb_ref,
                   bmat_ref, lng_ref, lnb_ref, ws_ref, bsm_ref, *out_refs, sub, tk, d, kv_only):
    if kv_only:
        k_ref, vt_ref = out_refs
    else:
        q2_ref, k_ref, vt_ref, f_ref, oc_ref, ga_ref, gb_ref = out_refs
    pair = 2 * A_V_DIM
    bmat = bmat_ref[...]
    lo = (lax.broadcasted_iota(jnp.int32, (sub, pair), 1) % A_V_DIM) < A_QK_DIM
    lo_c = lax.broadcasted_iota(jnp.int32, (CHUNK, V7X_LANES), 1) < C_GROUP_DIM
    bsm = bsm_ref[...]
    twice = lambda t: jnp.concatenate([t, t], axis=1)
    gq = twice(gq_ref[...])
    gk = twice(gk_ref[...])

    for si in range(x_ref.shape[1] // sub):
        rows = slice(si * sub, (si + 1) * sub)
        hb = _modulated_norm(x_ref[0, rows, :], mod_ref[0], gn_ref[...], d)

        def proj(off, width, hb=hb):
            return jnp.dot(hb, w_ref[:, off:off + width], preferred_element_type=_F32)

        cos = twice(cos_ref[rows, :])
        sa = twice(sa_ref[rows, :])
        sb = twice(sb_ref[rows, :])

        def qk_heads(p_2h, g, cos=cos, sa=sa, sb=sb):
            msq = jnp.dot((p_2h * p_2h).astype(_BF16), bmat, preferred_element_type=_F32)
            t = p_2h * lax.rsqrt(msq + EPS) * g
            return (t * cos + pltpu.roll(t, pair - AXIS_DIM // 2, 1) * sa
                    + pltpu.roll(t, AXIS_DIM // 2, 1) * sb)

        if not kv_only:
            pq = proj(OFF_Q, A_WIDTH)
            for hp in range(A_HEADS // 2):
                t = qk_heads(pq[:, hp * pair:(hp + 1) * pair], gq) * Q_SCALE
                t_lo = jnp.where(lo, t, 0.0).astype(_BF16)
                t_hi = jnp.where(lo, 0.0, t).astype(_BF16)
                for h2 in range(2):
                    q2_ref[0, 2 * hp + h2, 0, rows, :] = t_lo[:, h2 * A_V_DIM:(h2 + 1) * A_V_DIM]
                    q2_ref[0, 2 * hp + h2, 1, rows, :] = t_hi[:, h2 * A_V_DIM:(h2 + 1) * A_V_DIM]

        pk = proj(OFF_K, A_WIDTH)
        for hp in range(A_HEADS // 2):
            t = qk_heads(pk[:, hp * pair:(hp + 1) * pair], gk).astype(_BF16)
            for h2 in range(2):
                k_ref[0, 2 * hp + h2, rows, :] = t[:, h2 * A_V_DIM:(h2 + 1) * A_V_DIM]

        pv = proj(OFF_V, A_WIDTH)
        for hh in range(A_HEADS):
            r0 = si * sub
            vt_ref[0, hh, r0 // tk, :, r0 % tk:r0 % tk + sub] = (
                pv[:, hh * A_V_DIM:(hh + 1) * A_V_DIM].T.astype(_BF16))
        if kv_only:
            continue

        f_ref[0, rows, :] = proj(OFF_F, B_WIDTH)

        u = proj(OFF_U, C_WIDTH)
        vc = proj(OFF_VC, C_WIDTH)
        mu = jnp.mean(vc, axis=-1, keepdims=True)
        dv = vc - mu
        var = jnp.mean(dv * dv, axis=-1, keepdims=True)
        vn = (dv * lax.rsqrt(var + EPS) * lng_ref[...] + lnb_ref[...]).astype(_BF16)

        pg = proj(OFF_GATE, GATE_WIDTH)
        gate = pg * _sigmoid(pg)
        ga_ref[0, rows, :] = gate[:, :A_WIDTH].astype(_BF16)
        gb_ref[0, rows, :] = gate[:, A_WIDTH:A_WIDTH + B_WIDTH].astype(_BF16)
        gate_c = gate[:, A_WIDTH + B_WIDTH:]

        for t in range(sub // CHUNK):
            crows = slice(t * CHUNK, (t + 1) * CHUNK)
            vl = vn[crows, :V7X_LANES]
            vr = vn[crows, V7X_LANES:]
            s_l = jnp.where(lo_c, jnp.dot(ws_ref[0], vl, preferred_element_type=_F32),
                            jnp.dot(ws_ref[1], vl, preferred_element_type=_F32))
            s_r = jnp.where(lo_c, jnp.dot(ws_ref[2], vr, preferred_element_type=_F32),
                            jnp.dot(ws_ref[3], vr, preferred_element_type=_F32))
            s = jnp.concatenate([s_l, s_r], axis=1) + bsm
            oc_ref[0, si * sub + t * CHUNK:si * sub + (t + 1) * CHUNK, :] = (
                u[crows] * s * gate_c[crows]).astype(_BF16)


def _inproj_call(x, mod, gn, w_bf, gq, gk, cos, sa, sb, bmat, lng, lnb, ws_bf, bsm, *, tm, sub, tk, layer,
                 kv_only=False):
    b, n, d = x.shape
    nt = n // tm
    full = lambda *shape: pl.BlockSpec(shape, lambda bi, i: (0,) * len(shape))
    of_layer = lambda *shape: pl.BlockSpec((None,) + shape, lambda bi, i: (layer,) + (0,) * len(shape))
    in_specs = [
        pl.BlockSpec((1, tm, d), lambda bi, i: (bi, i, 0)),
        pl.BlockSpec((1, 1, 3 * d), lambda bi, i: (bi, 0, 0)),
        full(1, d),
        of_layer(d, w_bf.shape[-1]),
        full(1, V7X_LANES),
        full(1, V7X_LANES),
        pl.BlockSpec((tm, V7X_LANES), lambda bi, i: (i, 0)),
        pl.BlockSpec((tm, V7X_LANES), lambda bi, i: (i, 0)),
        pl.BlockSpec((tm, V7X_LANES), lambda bi, i: (i, 0)),
        full(2 * A_V_DIM, 2 * A_V_DIM),
        full(1, C_WIDTH),
        full(1, C_WIDTH),
        of_layer(C_GROUPS, CHUNK, CHUNK),
        full(CHUNK, C_WIDTH),
    ]
    out_shape = [
        jax.ShapeDtypeStruct((b, A_HEADS, 2, n, A_V_DIM), _BF16),
        jax.ShapeDtypeStruct((b, A_HEADS, n, A_V_DIM), _BF16),
        jax.ShapeDtypeStruct((b, A_HEADS, n // tk, A_V_DIM, tk), _BF16),
        jax.ShapeDtypeStruct((b, n, B_WIDTH), _F32),
        jax.ShapeDtypeStruct((b, n, C_WIDTH), _BF16),
        jax.ShapeDtypeStruct((b, n, A_WIDTH), _BF16),
        jax.ShapeDtypeStruct((b, n, B_WIDTH), _BF16),
    ]
    out_specs = [
        pl.BlockSpec((1, A_HEADS, 2, tm, A_V_DIM), lambda bi, i: (bi, 0, 0, i, 0)),
        pl.BlockSpec((1, A_HEADS, tm, A_V_DIM), lambda bi, i: (bi, 0, i, 0)),
        pl.BlockSpec((1, A_HEADS, tm // tk, A_V_DIM, tk), lambda bi, i: (bi, 0, i, 0, 0)),
        pl.BlockSpec((1, tm, B_WIDTH), lambda bi, i: (bi, i, 0)),
        pl.BlockSpec((1, tm, C_WIDTH), lambda bi, i: (bi, i, 0)),
        pl.BlockSpec((1, tm, A_WIDTH), lambda bi, i: (bi, i, 0)),
        pl.BlockSpec((1, tm, B_WIDTH), lambda bi, i: (bi, i, 0)),
    ]
    if kv_only:
        out_shape, out_specs = out_shape[1:3], out_specs[1:3]
    return pl.pallas_call(
        functools.partial(_inproj_kernel, sub=sub, tk=tk, d=d, kv_only=kv_only),
        grid=(b, nt),
        in_specs=in_specs,
        out_specs=out_specs,
        out_shape=out_shape,
        compiler_params=_params(2),
        name="inproj",
    )(x, mod, gn, w_bf, gq, gk, cos, sa, sb, bmat, lng, lnb, ws_bf, bsm)


def _attn_kernel(*refs, tq, nq, n_src, tiles, lam_init, bounded):
    lam_ref, q2_ref, ga_ref, gsub_ref = refs[:4]
    src_refs = refs[4:4 + 2 * n_src]
    out_ref = refs[4 + 2 * n_src]
    p_sc, m_sc, l_sc, acc_sc = refs[5 + 2 * n_src:]

    def queries(u):
        rows = slice(u * tq, (u + 1) * tq)
        return jnp.concatenate([q2_ref[0, 0, 0, rows, :], q2_ref[0, 0, 1, rows, :]], axis=0)

    l_sc[...] = jnp.zeros(l_sc.shape, _F32)
    acc_sc[...] = jnp.zeros(acc_sc.shape, _F32)
    lv = lam_ref[...]
    lam = (jnp.exp(jnp.sum(lv[0:1] * lv[1:2], axis=-1, keepdims=True))
           - jnp.exp(jnp.sum(lv[2:3] * lv[3:4], axis=-1, keepdims=True)) + lam_init)

    def finalize(u):
        inv = 1.0 / l_sc[u]
        acc = acc_sc[u]
        o = acc[:, :tq] * inv[:, :tq] - lam * (acc[:, tq:] * inv[:, tq:])
        ms = jnp.mean(o * o, axis=0, keepdims=True)
        on = o * lax.rsqrt(ms + EPS)
        rows = slice(u * tq, (u + 1) * tq)
        out = on.T * (gsub_ref[...] * (1.0 - lam_init)) * ga_ref[0, rows, :].astype(_F32)
        out_ref[0, rows, :] = out.astype(_BF16)

    def tile_len(si):
        return src_refs[2 * si + 1].shape[-1]

    def k_tile(si, t):
        tk = tile_len(si)
        if isinstance(t, int):
            return src_refs[2 * si][0, 0, t * tk:(t + 1) * tk, :]
        return src_refs[2 * si][0, 0, pl.ds(pl.multiple_of(t * tk, tk), tk), :]

    def vt_tile(si, t):
        return src_refs[2 * si + 1][0, 0, t]

    if bounded:
        def exponentials(slot, u, k_t):
            p = jnp.exp2(lax.dot_general(k_t, queries(u), _NT, preferred_element_type=_F32))
            l_sc[u] += jnp.sum(p, axis=0, keepdims=True)
            p_sc[slot, :k_t.shape[0], :] = p.astype(_BF16)

        def weighted_values(slot, u, vt_t):
            acc_sc[u] += jnp.dot(vt_t, p_sc[slot, :vt_t.shape[1], :], preferred_element_type=_F32)

        stages = [(u, si, t) for u in range(nq) for si in range(n_src) for t in range(tiles[si])]
        exponentials(0, stages[0][0], k_tile(*stages[0][1:]))
        for j, (u, si, t) in enumerate(stages):
            if j + 1 < len(stages):
                exponentials((j + 1) % 2, stages[j + 1][0], k_tile(*stages[j + 1][1:]))
            weighted_values(j % 2, u, vt_tile(si, t))
            if j + 1 == len(stages) or stages[j + 1][0] != u:
                finalize(u)
    else:
        for u in range(nq):
            m_sc[...] = jnp.full(m_sc.shape, -jnp.inf, _F32)

            def step(k_t, vt_t, u=u):
                s = lax.dot_general(k_t, queries(u), _NT, preferred_element_type=_F32)
                m_prev = m_sc[...]
                m_new = jnp.maximum(m_prev, jnp.max(s, axis=0, keepdims=True))
                alpha = jnp.exp2(m_prev - m_new)
                p = jnp.exp2(s - m_new)
                l_sc[u] = alpha * l_sc[u] + jnp.sum(p, axis=0, keepdims=True)
                acc_sc[u] = alpha * acc_sc[u] + jnp.dot(vt_t, p.astype(_BF16), preferred_element_type=_F32)
                m_sc[...] = m_new

            for si in range(n_src):
                if tiles[si] == 1:
                    step(k_tile(si, 0), vt_tile(si, 0))
                else:
                    def body(t, carry, si=si, step=step):
                        step(k_tile(si, t), vt_tile(si, t))
                        return carry
                    lax.fori_loop(0, tiles[si], body, 0)
            finalize(u)


def _attn_call(lamv, q2, ga, gsub, sources, *, tq, nq, lam_init, bounded):
    b, _, _, n, _ = q2.shape
    rows = nq * tq
    in_specs = [
        pl.BlockSpec(lamv.shape, lambda bi, hi, i: (0, 0)),
        pl.BlockSpec((1, 1, 2, rows, A_V_DIM), lambda bi, hi, i: (bi, hi, 0, i, 0)),
        pl.BlockSpec((1, rows, A_V_DIM), lambda bi, hi, i: (bi, i, hi)),
        pl.BlockSpec((1, A_V_DIM), lambda bi, hi, i: (0, 0)),
    ]
    args = [lamv, q2, ga, gsub]
    tiles = []
    for k_s, vt_s in sources:
        ls = k_s.shape[2]
        nt, _, tk = vt_s.shape[2:]
        tiles.append(nt)
        in_specs.append(pl.BlockSpec((1, 1, ls, A_V_DIM), lambda bi, hi, i: (bi, hi, 0, 0)))
        in_specs.append(pl.BlockSpec((1, 1, nt, A_V_DIM, tk), lambda bi, hi, i: (bi, hi, 0, 0, 0)))
        args += [k_s, vt_s]
    tk_max = max(vt_s.shape[-1] for _, vt_s in sources)
    return pl.pallas_call(
        functools.partial(_attn_kernel, tq=tq, nq=nq, n_src=len(sources), tiles=tuple(tiles),
                          lam_init=lam_init, bounded=bounded),
        grid=(b, A_HEADS, n // rows),
        in_specs=in_specs,
        out_specs=pl.BlockSpec((1, rows, A_V_DIM), lambda bi, hi, i: (bi, i, hi)),
        out_shape=jax.ShapeDtypeStruct((b, n, A_WIDTH), _BF16),
        scratch_shapes=[pltpu.VMEM((2, tk_max, 2 * tq), _BF16),
                        pltpu.VMEM((1, 2 * tq), _F32),
                        pltpu.VMEM((nq, 1, 2 * tq), _F32),
                        pltpu.VMEM((nq, A_V_DIM, 2 * tq), _F32)],
        compiler_params=_params(3),
        name="diff_attn" if bounded else "diff_attn_general",
    )(*args)


SAFE_EXP2_RANGE = 60.0


def _attention(lamv, q2, ga, gsub, sources, gq, gk, *, tq, nq, lam_init):
    bound = A_QK_DIM * Q_SCALE * jnp.max(jnp.abs(gq)) * jnp.max(jnp.abs(gk))
    call = lambda bounded: functools.partial(_attn_call, tq=tq, nq=nq, lam_init=lam_init, bounded=bounded)
    return lax.cond(bound <= SAFE_EXP2_RANGE, call(True), call(False), lamv, q2, ga, gsub, sources)


def _dft1_kernel(f1_ref, x_ref, o_ref, *, rows_per_step):
    n1 = x_ref.shape[1]
    x = x_ref[0].reshape(n1, rows_per_step * B_WIDTH).astype(_BF16)
    z = jnp.dot(f1_ref[...], x, preferred_element_type=_F32)
    o_ref[0] = z.reshape(2 * n1, rows_per_step, B_WIDTH).astype(_BF16)


def _dft1_call(f1, x4d, *, rows_per_step):
    b, n1, n2, _ = x4d.shape
    return pl.pallas_call(
        functools.partial(_dft1_kernel, rows_per_step=rows_per_step),
        grid=(b, n2 // rows_per_step),
        in_specs=[pl.BlockSpec((2 * n1, n1), lambda bi, j: (0, 0)),
                  pl.BlockSpec((1, n1, rows_per_step, B_WIDTH), lambda bi, j: (bi, 0, j, 0))],
        out_specs=pl.BlockSpec((1, 2 * n1, rows_per_step, B_WIDTH), lambda bi, j: (bi, 0, j, 0)),
        out_shape=jax.ShapeDtypeStruct((b, 2 * n1, n2, B_WIDTH), _BF16),
        compiler_params=_params(2),
        name="dft_stage1",
    )(f1, x4d)


def _dft2_kernel(*refs, g, n2, two_stage, norm):
    if two_stage:
        ar_ref, ai_ref, twc_ref, tws_ref, w2c_ref, mix_ref, wf_ref, bf_ref, out_ref = refs
        ar = ar_ref[0, 0].reshape(g * n2, B_WIDTH).astype(_F32)
        ai = ai_ref[0, 0].reshape(g * n2, B_WIDTH).astype(_F32)
        tc = twc_ref[...].reshape(g * n2, V7X_LANES)
        ts = tws_ref[...].reshape(g * n2, V7X_LANES)
        c2 = jnp.concatenate([tc, tc], axis=1)
        s2 = jnp.concatenate([ts, ts], axis=1)
        a = jnp.concatenate([ar * c2 - ai * s2, ar * s2 + ai * c2], axis=1).astype(_BF16)
        uv = jnp.dot(a, mix_ref[...], preferred_element_type=_F32)
    else:
        ar_ref, w2c_ref, mix_ref, wf_ref, bf_ref, out_ref = refs
        uv = jnp.dot(ar_ref[0, 0, 0].astype(_BF16), mix_ref[:B_WIDTH, :], preferred_element_type=_F32)
    u = uv[:, :B_WIDTH].astype(_BF16)
    v = uv[:, B_WIDTH:].astype(_BF16)
    w2c = w2c_ref[...]
    frs = []
    for j in range(g):
        rows = slice(j * n2, (j + 1) * n2)
        frs.append(jnp.dot(w2c, jnp.concatenate([u[rows], v[rows]], axis=0), preferred_element_type=_F32))
    fr = jnp.concatenate(frs, axis=0) * norm
    y = jnp.dot(fr.astype(_BF16), wf_ref[...], preferred_element_type=_F32) + bf_ref[...]
    if two_stage:
        out_ref[0] = jnp.swapaxes(y.reshape(g, n2, B_WIDTH), 0, 1).astype(_BF16)
    else:
        out_ref[0] = y.astype(_BF16)


def _dft_tables(n1, n2):
    n = n1 * n2
    k1 = np.arange(n1)
    a1 = 2.0 * np.pi * ((k1[:, None] * k1[None, :]) % n1) / n1
    f1 = np.concatenate([np.cos(a1), np.sin(a1)], axis=0)
    k2 = np.arange(n2)
    a2 = 2.0 * np.pi * ((k2[:, None] * k2[None, :]) % n2) / n2
    w2c = np.concatenate([np.cos(a2), -np.sin(a2)], axis=1)
    gch = np.arange(B_GROUP_DIM)
    ag = 2.0 * np.pi * ((gch[:, None] * gch[None, :]) % B_GROUP_DIM) / B_GROUP_DIM
    eye = np.eye(B_GROUPS)
    cg = np.kron(eye, np.cos(ag))
    sg = np.kron(eye, np.sin(ag))
    mix = np.block([[cg, sg], [-sg, cg]])
    to_bf16 = lambda t: jnp.asarray(t, _F32).astype(_BF16)
    return to_bf16(f1), to_bf16(w2c), to_bf16(mix), 1.0 / math.sqrt(n * B_GROUP_DIM)


def _twiddles(n1, n2):
    k1 = jnp.arange(n1, dtype=jnp.int32)[:, None]
    m2 = jnp.arange(n2, dtype=jnp.int32)[None, :]
    ang = (k1 * m2).astype(_F32) * (2.0 * math.pi / (n1 * n2))
    shape = (n1, n2, V7X_LANES)
    return (jnp.broadcast_to(jnp.cos(ang)[:, :, None], shape), jnp.broadcast_to(jnp.sin(ang)[:, :, None], shape))


def _fourier(f, wf_bd, bf_row):
    b, n, _ = f.shape
    two_stage = n > 2 * V7X_LANES
    n2 = V7X_LANES if two_stage else n
    n1 = n // n2
    g = min(n1, V7X_BF16_SUBLANES)
    f1, w2c, mix, norm = _dft_tables(n1, n2)
    full = lambda *shape: pl.BlockSpec(shape, lambda bi, j: (0,) * len(shape))
    consts = [w2c, mix, wf_bd, bf_row]
    const_specs = [full(*w2c.shape), full(*mix.shape), full(B_WIDTH, B_WIDTH), full(1, B_WIDTH)]
    if two_stage:
        a = _dft1_call(f1, f.reshape(b, n1, n2, B_WIDTH), rows_per_step=2 * V7X_BF16_SUBLANES)
        a5 = a.reshape(b, 2, n1, n2, B_WIDTH)
        out_shape = jax.ShapeDtypeStruct((b, n2, n1, B_WIDTH), _BF16)
        out_spec = pl.BlockSpec((1, n2, g, B_WIDTH), lambda bi, j: (bi, 0, j, 0))
        twc, tws = _twiddles(n1, n2)
        args = [a5, a5, twc, tws] + consts
        in_specs = [pl.BlockSpec((1, 1, g, n2, B_WIDTH), lambda bi, j: (bi, 0, j, 0, 0)),
                    pl.BlockSpec((1, 1, g, n2, B_WIDTH), lambda bi, j: (bi, 1, j, 0, 0)),
                    pl.BlockSpec((g, n2, V7X_LANES), lambda bi, j: (j, 0, 0)),
                    pl.BlockSpec((g, n2, V7X_LANES), lambda bi, j: (j, 0, 0))] + const_specs
    else:
        args = [f.reshape(b, 1, 1, n2, B_WIDTH)] + consts
        in_specs = [pl.BlockSpec((1, 1, 1, n2, B_WIDTH), lambda bi, j: (bi, 0, 0, 0, 0))] + const_specs
        out_shape = jax.ShapeDtypeStruct((b, n2, B_WIDTH), _BF16)
        out_spec = pl.BlockSpec((1, n2, B_WIDTH), lambda bi, j: (bi, 0, 0))
    out = pl.pallas_call(
        functools.partial(_dft2_kernel, g=g, n2=n2, two_stage=two_stage, norm=norm),
        grid=(b, n1 // g),
        in_specs=in_specs,
        out_specs=out_spec,
        out_shape=out_shape,
        compiler_params=_params(2),
        name="dft_stage2",
    )(*args)
    return out.reshape(b, n, B_WIDTH)


def _merge_kernel(oa_ref, ob_ref, gb_ref, oc_ref, x_ref, mod_ref, gn_ref, wm_ref, wa_ref, wb_ref, wc_ref, wo_ref,
                  o_ref, *, d, sub):
    mod = mod_ref[0]
    for r0 in range(0, x_ref.shape[1], sub):
        rows = slice(r0, r0 + sub)
        x = x_ref[0, rows, :]
        hb = _modulated_norm(x, mod, gn_ref[...], d)
        branches = (jnp.dot(oa_ref[0, rows, :], wa_ref[...], preferred_element_type=_F32),
                    jnp.dot(ob_ref[0, rows, :] * gb_ref[0, rows, :], wb_ref[...], preferred_element_type=_F32),
                    jnp.dot(oc_ref[0, rows, :], wc_ref[...], preferred_element_type=_F32))
        y = None
        for j, yj in enumerate(branches):
            mj = _sigmoid(jnp.dot(hb, wm_ref[:, OFF_MERGE + j * d:OFF_MERGE + (j + 1) * d],
                                  preferred_element_type=_F32))
            y = mj * yj if y is None else y + mj * yj
        out = jnp.dot(y.astype(_BF16), wo_ref[...], preferred_element_type=_F32)
        o_ref[0, rows, :] = x + mod[:, 2 * d:] * out


def _merge_call(oa, ob, gb, oc, x, mod, gn, w_in_bf, wa, wb, wc, wo, *, tm, sub, layer):
    b, n, d = x.shape
    row = lambda w: pl.BlockSpec((1, tm, w), lambda bi, i: (bi, i, 0))
    of_layer = lambda *shape: pl.BlockSpec((None,) + shape, lambda bi, i: (layer,) + (0,) * len(shape))
    return pl.pallas_call(
        functools.partial(_merge_kernel, d=d, sub=sub),
        grid=(b, n // tm),
        in_specs=[row(A_WIDTH), row(B_WIDTH), row(B_WIDTH), row(C_WIDTH), row(d),
                  pl.BlockSpec((1, 1, 3 * d), lambda bi, i: (bi, 0, 0)),
                  pl.BlockSpec((1, d), lambda bi, i: (0, 0)),
                  of_layer(d, w_in_bf.shape[-1]), of_layer(A_WIDTH, d), of_layer(B_WIDTH, d),
                  of_layer(C_WIDTH, d), of_layer(d, d)],
        out_specs=row(d),
        out_shape=jax.ShapeDtypeStruct((b, n, d), _F32),
        compiler_params=_params(2),
        name="merge",
    )(oa, ob, gb, oc, x, mod, gn, w_in_bf, wa, wb, wc, wo)


def _rope_tables(n):
    n_rows = n // GRID_W
    freqs = ROPE_BASE ** (-jnp.arange(0, AXIS_DIM, 2, dtype=_F32) / AXIS_DIM)
    ang_r = jnp.arange(n_rows, dtype=_F32)[:, None] * freqs
    ang_c = jnp.arange(GRID_W, dtype=_F32)[:, None] * freqs

    def on_grid(fn):
        shape = (n_rows, GRID_W, AXIS_DIM // 2)
        r = jnp.broadcast_to(fn(ang_r)[:, None, :], shape)
        c = jnp.broadcast_to(fn(ang_c)[None, :, :], shape)
        return jnp.concatenate([r, r, c, c], axis=-1).reshape(n, A_QK_DIM)

    cos, sin = on_grid(jnp.cos), on_grid(jnp.sin)
    first_half = (jnp.arange(A_QK_DIM) % AXIS_DIM) < AXIS_DIM // 2
    sa = jnp.where(first_half, -sin, 0.0)
    sb = jnp.where(first_half, 0.0, sin)
    tile2 = lambda t: jnp.concatenate([t, t], axis=-1)
    return tile2(cos), tile2(sa), tile2(sb)


def _no_rope_tables(n):
    return (jnp.ones((n, V7X_LANES), _F32), jnp.zeros((n, V7X_LANES), _F32), jnp.zeros((n, V7X_LANES), _F32))


def kernel(x, c, ctx, c_ctx, w_ada, b_ada, g_norm, w_in, g_q, g_k, lam_q1, lam_k1, lam_q2, lam_k2, g_sub,
           w_f, b_f, ln_g, ln_b, w_s, b_s, w_br_a, w_br_b, w_br_c, w_out):
    b, n, d = x.shape
    n_ctx = ctx.shape[1]
    depth = w_in.shape[0]
    tm_lat = min(n, 512)
    tm_ctx = min(n_ctx, 512)

    n_rows = V7X_SUBLANES * pl.cdiv(b + 1, V7X_SUBLANES)
    rows = jnp.concatenate([c, c_ctx[None, :], jnp.zeros((n_rows - b - 1, d), _F32)], axis=0)
    mod_all = _mod_call(rows, w_ada, b_ada)

    rope_lat = _rope_tables(n)
    rope_ctx = _no_rope_tables(n_ctx)
    group_mean = np.kron(np.eye(2 * A_V_DIM // A_QK_DIM), np.full((A_QK_DIM, A_QK_DIM), 1.0 / A_QK_DIM))
    bmat = jnp.asarray(group_mean, _BF16)

    w_in_bf = w_in.astype(_BF16)
    w_s_bf = w_s.astype(_BF16)
    merge_w = (w_in_bf, w_br_a.astype(_BF16), w_br_b.astype(_BF16), w_br_c.astype(_BF16), w_out.astype(_BF16))

    for l in range(depth):
        last = l == depth - 1
        lam_init = 0.8 - 0.6 * math.exp(-0.3 * l)
        lamv = jnp.stack([lam_q1[l], lam_k1[l], lam_q2[l], lam_k2[l]], axis=0)
        shared = dict(
            gn=g_norm[l][None, :], w_bf=w_in_bf,
            gq=jnp.tile(g_q[l], 2)[None, :], gk=jnp.tile(g_k[l], 2)[None, :], bmat=bmat,
            lng=ln_g[l][None, :], lnb=ln_b[l][None, :], ws_bf=w_s_bf,
            bsm=jnp.repeat(b_s[l].T, C_GROUP_DIM, axis=1), layer=l)
        gsub = g_sub[l][None, :]
        wf_bd = jax.scipy.linalg.block_diag(*[w_f[l, gi] for gi in range(B_GROUPS)]).astype(_BF16)
        bf_row = b_f[l].reshape(1, B_WIDTH)

        mod_lat = mod_all[l, :b][:, None, :]
        mod_ctx = jnp.broadcast_to(mod_all[l, b][None, None, :], (b, 1, 3 * d))

        ctx_proj = _inproj_call(ctx, mod_ctx, cos=rope_ctx[0], sa=rope_ctx[1], sb=rope_ctx[2], tm=tm_ctx,
                                sub=tm_ctx, tk=tm_ctx, kv_only=last, **shared)
        if last:
            ck, cvt = ctx_proj
        else:
            cq2, ck, cvt, cf, coc, cga, cgb = ctx_proj
        q2, k, vt, f, oc, ga, gb = _inproj_call(
            x, mod_lat, cos=rope_lat[0], sa=rope_lat[1], sb=rope_lat[2], tm=min(n, 2 * tm_lat), sub=tm_lat,
            tk=tm_lat, **shared)

        oa = _attention(lamv, q2, ga, gsub, [(ck, cvt), (k, vt)], g_q[l], g_k[l], tq=tm_lat,
                        nq=min(4, n // tm_lat), lam_init=lam_init)
        ob = _fourier(f, wf_bd, bf_row)
        x_new = _merge_call(oa, ob, gb, oc, x, mod_lat, shared["gn"], *merge_w, tm=min(n, 2 * tm_lat), sub=tm_lat,
                            layer=l)

        if not last:
            coa = _attention(lamv, cq2, cga, gsub, [(ck, cvt)], g_q[l], g_k[l], tq=tm_ctx, nq=1,
                             lam_init=lam_init)
            cob = _fourier(cf, wf_bd, bf_row)
            ctx = _merge_call(coa, cob, cgb, coc, ctx, mod_ctx, shared["gn"], *merge_w, tm=tm_ctx, sub=tm_ctx,
                              layer=l)
        x = x_new
    return x
```

```python
import functools
import math

import numpy as np
import jax
import jax.numpy as jnp
from jax import lax
from jax.experimental import pallas as pl
from jax.experimental.pallas import tpu as pltpu

A_HEADS = 4
A_QK_DIM = 64
A_V_DIM = 2 * A_QK_DIM
A_WIDTH = A_HEADS * A_V_DIM
B_GROUPS = 4
B_GROUP_DIM = 64
B_WIDTH = B_GROUPS * B_GROUP_DIM
C_GROUPS = 4
C_GROUP_DIM = 64
C_WIDTH = C_GROUPS * C_GROUP_DIM
CHUNK = 128
GRID_W = 64
ROPE_BASE = 10000.0
AXIS_DIM = A_QK_DIM // 2
EPS = 1e-6

OFF_Q = 0
OFF_K = OFF_Q + A_WIDTH
OFF_V = OFF_K + A_WIDTH
OFF_F = OFF_V + A_WIDTH
OFF_U = OFF_F + B_WIDTH
OFF_VC = OFF_U + C_WIDTH
OFF_GATE = OFF_VC + C_WIDTH
GATE_WIDTH = A_WIDTH + B_WIDTH + C_WIDTH
OFF_MERGE = OFF_GATE + GATE_WIDTH

V7X_LANES = 128
V7X_SUBLANES = 8
V7X_BF16_SUBLANES = 16
V7X_MXU_DIM = 256
V7X_VMEM_LIMIT_BYTES = 56 * 1024 * 1024

LOG2E = math.log2(math.e)
Q_SCALE = (A_QK_DIM ** -0.5) * LOG2E

_F32 = jnp.float32
_BF16 = jnp.bfloat16
_NT = (((1,), (1,)), ((), ()))


def _sigmoid(x):
    return 0.5 * jnp.tanh(0.5 * x) + 0.5


def _params(n_axes):
    return pltpu.CompilerParams(dimension_semantics=("arbitrary",) * n_axes,
                                vmem_limit_bytes=V7X_VMEM_LIMIT_BYTES)


def _mod_kernel(c_ref, w_ref, b_ref, o_ref):
    cc = c_ref[...]
    s = cc * _sigmoid(cc)
    o_ref[0] = jnp.dot(s, w_ref[0], preferred_element_type=_F32) + b_ref[0]


def _mod_call(rows, w_ada, b_ada):
    depth, d, d3 = w_ada.shape
    nb = d3 // d
    return pl.pallas_call(
        _mod_kernel,
        grid=(depth, nb),
        in_specs=[pl.BlockSpec((rows.shape[0], d), lambda l, j: (0, 0)),
                  pl.BlockSpec((1, d, d), lambda l, j: (l, 0, j)),
                  pl.BlockSpec((1, 1, d), lambda l, j: (l, 0, j))],
        out_specs=pl.BlockSpec((1, rows.shape[0], d), lambda l, j: (l, 0, j)),
        out_shape=jax.ShapeDtypeStruct((depth, rows.shape[0], d3), _F32),
        compiler_params=_params(2),
        name="adaln_mod",
    )(rows, w_ada, b_ada.reshape(depth, 1, d3))


def _modulated_norm(x, mod, gn, d):
    ms = jnp.mean(x * x, axis=-1, keepdims=True)
    gain = gn * (1.0 + mod[:, d:2 * d])
    return (x * lax.rsqrt(ms + EPS) * gain + mod[:, :d]).astype(_BF16)


def _inproj_kernel(x_ref, mod_ref, gn_ref, w_ref, gq_ref, gk_ref, cos_ref, sa_ref, sb_ref,
                   bmat_ref, lng_ref, lnb_ref, ws_ref, bsm_ref, *out_refs, sub, tk, d, kv_only):
    if kv_only:
        k_ref, vt_ref = out_refs
    else:
        q2_ref, k_ref, vt_ref, f_ref, oc_ref, ga_ref, gb_ref = out_refs
    pair = 2 * A_V_DIM
    bmat = bmat_ref[...]
    lo = (lax.broadcasted_iota(jnp.int32, (sub, pair), 1) % A_V_DIM) < A_QK_DIM
    lo_c = lax.broadcasted_iota(jnp.int32, (CHUNK, V7X_LANES), 1) < C_GROUP_DIM
    bsm = bsm_ref[...]
    twice = lambda t: jnp.concatenate([t, t], axis=1)
    gq = twice(gq_ref[...]) * Q_SCALE
    gk = twice(gk_ref[...])

    for si in range(x_ref.shape[1] // sub):
        rows = slice(si * sub, (si + 1) * sub)
        hb = _modulated_norm(x_ref[0, rows, :], mod_ref[0], gn_ref[...], d)

        def proj(off, width, hb=hb):
            return jnp.dot(hb, w_ref[:, off:off + width], preferred_element_type=_F32)

        cos = twice(cos_ref[rows, :])
        sa = twice(sa_ref[rows, :])
        sb = twice(sb_ref[rows, :])

        def qk_heads(p_2h, g, cos=cos, sa=sa, sb=sb):
            msq = jnp.dot((p_2h * p_2h).astype(_BF16), bmat, preferred_element_type=_F32)
            t = p_2h * lax.rsqrt(msq + EPS) * g
            return (t * cos + pltpu.roll(t, pair - AXIS_DIM // 2, 1) * sa
                    + pltpu.roll(t, AXIS_DIM // 2, 1) * sb)

        if not kv_only:
            pq = proj(OFF_Q, A_WIDTH)
            for hp in range(A_HEADS // 2):
                t = qk_heads(pq[:, hp * pair:(hp + 1) * pair], gq)
                t_lo = jnp.where(lo, t, 0.0).astype(_BF16)
                t_hi = jnp.where(lo, 0.0, t).astype(_BF16)
                for h2 in range(2):
                    q2_ref[0, 2 * hp + h2, 0, rows, :] = t_lo[:, h2 * A_V_DIM:(h2 + 1) * A_V_DIM]
                    q2_ref[0, 2 * hp + h2, 1, rows, :] = t_hi[:, h2 * A_V_DIM:(h2 + 1) * A_V_DIM]

        pk = proj(OFF_K, A_WIDTH)
        for hp in range(A_HEADS // 2):
            t = qk_heads(pk[:, hp * pair:(hp + 1) * pair], gk).astype(_BF16)
            for h2 in range(2):
                k_ref[0, 2 * hp + h2, rows, :] = t[:, h2 * A_V_DIM:(h2 + 1) * A_V_DIM]

        pv = proj(OFF_V, A_WIDTH)
        for hh in range(A_HEADS):
            r0 = si * sub
            vt_ref[0, hh, r0 // tk, :, r0 % tk:r0 % tk + sub] = (
                pv[:, hh * A_V_DIM:(hh + 1) * A_V_DIM].T.astype(_BF16))
        if kv_only:
            continue

        f_ref[0, rows, :] = proj(OFF_F, B_WIDTH)

        u = proj(OFF_U, C_WIDTH)
        vc = proj(OFF_VC, C_WIDTH)
        mu = jnp.mean(vc, axis=-1, keepdims=True)
        dv = vc - mu
        var = jnp.mean(dv * dv, axis=-1, keepdims=True)
        vn = (dv * lax.rsqrt(var + EPS) * lng_ref[...] + lnb_ref[...]).astype(_BF16)

        pg = proj(OFF_GATE, GATE_WIDTH)
        gate = pg * _sigmoid(pg)
        ga_ref[0, rows, :] = gate[:, :A_WIDTH].astype(_BF16)
        gb_ref[0, rows, :] = gate[:, A_WIDTH:A_WIDTH + B_WIDTH].astype(_BF16)
        gate_c = gate[:, A_WIDTH + B_WIDTH:]

        for t in range(sub // CHUNK):
            crows = slice(t * CHUNK, (t + 1) * CHUNK)
            vl = vn[crows, :V7X_LANES]
            vr = vn[crows, V7X_LANES:]
            s_l = jnp.where(lo_c, jnp.dot(ws_ref[0], vl, preferred_element_type=_F32),
                            jnp.dot(ws_ref[1], vl, preferred_element_type=_F32))
            s_r = jnp.where(lo_c, jnp.dot(ws_ref[2], vr, preferred_element_type=_F32),
                            jnp.dot(ws_ref[3], vr, preferred_element_type=_F32))
            s = jnp.concatenate([s_l, s_r], axis=1) + bsm
            oc_ref[0, si * sub + t * CHUNK:si * sub + (t + 1) * CHUNK, :] = (
                u[crows] * s * gate_c[crows]).astype(_BF16)


def _inproj_call(x, mod, gn, w_bf, gq, gk, cos, sa, sb, bmat, lng, lnb, ws_bf, bsm, *, tm, sub, tk, layer,
                 kv_only=False):
    b, n, d = x.shape
    nt = n // tm
    full = lambda *shape: pl.BlockSpec(shape, lambda bi, i: (0,) * len(shape))
    of_layer = lambda *shape: pl.BlockSpec((None,) + shape, lambda bi, i: (layer,) + (0,) * len(shape))
    in_specs = [
        pl.BlockSpec((1, tm, d), lambda bi, i: (bi, i, 0)),
        pl.BlockSpec((1, 1, 3 * d), lambda bi, i: (bi, 0, 0)),
        full(1, d),
        of_layer(d, w_bf.shape[-1]),
        full(1, V7X_LANES),
        full(1, V7X_LANES),
        pl.BlockSpec((tm, V7X_LANES), lambda bi, i: (i, 0)),
        pl.BlockSpec((tm, V7X_LANES), lambda bi, i: (i, 0)),
        pl.BlockSpec((tm, V7X_LANES), lambda bi, i: (i, 0)),
        full(2 * A_V_DIM, 2 * A_V_DIM),
        full(1, C_WIDTH),
        full(1, C_WIDTH),
        of_layer(C_GROUPS, CHUNK, CHUNK),
        full(CHUNK, C_WIDTH),
    ]
    out_shape = [
        jax.ShapeDtypeStruct((b, A_HEADS, 2, n, A_V_DIM), _BF16),
        jax.ShapeDtypeStruct((b, A_HEADS, n, A_V_DIM), _BF16),
        jax.ShapeDtypeStruct((b, A_HEADS, n // tk, A_V_DIM, tk), _BF16),
        jax.ShapeDtypeStruct((b, n, B_WIDTH), _F32),
        jax.ShapeDtypeStruct((b, n, C_WIDTH), _BF16),
        jax.ShapeDtypeStruct((b, n, A_WIDTH), _BF16),
        jax.ShapeDtypeStruct((b, n, B_WIDTH), _BF16),
    ]
    out_specs = [
        pl.BlockSpec((1, A_HEADS, 2, tm, A_V_DIM), lambda bi, i: (bi, 0, 0, i, 0)),
        pl.BlockSpec((1, A_HEADS, tm, A_V_DIM), lambda bi, i: (bi, 0, i, 0)),
        pl.BlockSpec((1, A_HEADS, tm // tk, A_V_DIM, tk), lambda bi, i: (bi, 0, i, 0, 0)),
        pl.BlockSpec((1, tm, B_WIDTH), lambda bi, i: (bi, i, 0)),
        pl.BlockSpec((1, tm, C_WIDTH), lambda bi, i: (bi, i, 0)),
        pl.BlockSpec((1, tm, A_WIDTH), lambda bi, i: (bi, i, 0)),
        pl.BlockSpec((1, tm, B_WIDTH), lambda bi, i: (bi, i, 0)),
    ]
    if kv_only:
        out_shape, out_specs = out_shape[1:3], out_specs[1:3]
    return pl.pallas_call(
        functools.partial(_inproj_kernel, sub=sub, tk=tk, d=d, kv_only=kv_only),
        grid=(b, nt),
        in_specs=in_specs,
        out_specs=out_specs,
        out_shape=out_shape,
        compiler_params=_params(2),
        name="inproj",
    )(x, mod, gn, w_bf, gq, gk, cos, sa, sb, bmat, lng, lnb, ws_bf, bsm)


def _attn_kernel(*refs, tq, nq, n_src, tiles, lam_init, bounded):
    lam_ref, q2_ref, ga_ref, gsub_ref = refs[:4]
    src_refs = refs[4:4 + 2 * n_src]
    out_ref = refs[4 + 2 * n_src]
    p_sc, m_sc, l_sc, acc_sc = refs[5 + 2 * n_src:]

    def queries(u):
        rows = slice(u * tq, (u + 1) * tq)
        return jnp.concatenate([q2_ref[0, 0, 0, rows, :], q2_ref[0, 0, 1, rows, :]], axis=0)

    l_sc[...] = jnp.zeros(l_sc.shape, _F32)
    acc_sc[...] = jnp.zeros(acc_sc.shape, _F32)
    lv = lam_ref[...]
    lam = (jnp.exp(jnp.sum(lv[0:1] * lv[1:2], axis=-1, keepdims=True))
           - jnp.exp(jnp.sum(lv[2:3] * lv[3:4], axis=-1, keepdims=True)) + lam_init)

    def finalize(u):
        inv = 1.0 / l_sc[u]
        acc = acc_sc[u]
        o = acc[:, :tq] * inv[:, :tq] - lam * (acc[:, tq:] * inv[:, tq:])
        ms = jnp.mean(o * o, axis=0, keepdims=True)
        on = o * lax.rsqrt(ms + EPS)
        rows = slice(u * tq, (u + 1) * tq)
        out = on.T * (gsub_ref[...] * (1.0 - lam_init)) * ga_ref[0, rows, :].astype(_F32)
        out_ref[0, rows, :] = out.astype(_BF16)

    def tile_len(si):
        return src_refs[2 * si + 1].shape[-1]

    def k_tile(si, t):
        tk = tile_len(si)
        if isinstance(t, int):
            return src_refs[2 * si][0, 0, t * tk:(t + 1) * tk, :]
        return src_refs[2 * si][0, 0, pl.ds(pl.multiple_of(t * tk, tk), tk), :]

    def vt_tile(si, t):
        return src_refs[2 * si + 1][0, 0, t]

    if bounded:
        def exponentials(slot, u, k_t):
            p = jnp.exp2(lax.dot_general(k_t, queries(u), _NT, preferred_element_type=_F32))
            l_sc[u] += jnp.sum(p, axis=0, keepdims=True)
            p_sc[slot, :k_t.shape[0], :] = p.astype(_BF16)

        def weighted_values(slot, u, vt_t):
            acc_sc[u] += jnp.dot(vt_t, p_sc[slot, :vt_t.shape[1], :], preferred_element_type=_F32)

        stages = [(u, si, t) for u in range(nq) for si in range(n_src) for t in range(tiles[si])]
        exponentials(0, stages[0][0], k_tile(*stages[0][1:]))
        for j, (u, si, t) in enumerate(stages):
            if j + 1 < len(stages):
                exponentials((j + 1) % 2, stages[j + 1][0], k_tile(*stages[j + 1][1:]))
            weighted_values(j % 2, u, vt_tile(si, t))
            if j + 1 == len(stages) or stages[j + 1][0] != u:
                finalize(u)
    else:
        for u in range(nq):
            m_sc[...] = jnp.full(m_sc.shape, -jnp.inf, _F32)

            def step(k_t, vt_t, u=u):
                s = lax.dot_general(k_t, queries(u), _NT, preferred_element_type=_F32)
                m_prev = m_sc[...]
                m_new = jnp.maximum(m_prev, jnp.max(s, axis=0, keepdims=True))
                alpha = jnp.exp2(m_prev - m_new)
                p = jnp.exp2(s - m_new)
                l_sc[u] = alpha * l_sc[u] + jnp.sum(p, axis=0, keepdims=True)
                acc_sc[u] = alpha * acc_sc[u] + jnp.dot(vt_t, p.astype(_BF16), preferred_element_type=_F32)
                m_sc[...] = m_new

            for si in range(n_src):
                if tiles[si] == 1:
                    step(k_tile(si, 0), vt_tile(si, 0))
                else:
                    def body(t, carry, si=si, step=step):
                        step(k_tile(si, t), vt_tile(si, t))
                        return carry
                    lax.fori_loop(0, tiles[si], body, 0)
            finalize(u)


def _attn_call(lamv, q2, ga, gsub, sources, *, tq, nq, lam_init, bounded):
    b, _, _, n, _ = q2.shape
    rows = nq * tq
    in_specs = [
        pl.BlockSpec(lamv.shape, lambda bi, hi, i: (0, 0)),
        pl.BlockSpec((1, 1, 2, rows, A_V_DIM), lambda bi, hi, i: (bi, hi, 0, i, 0)),
        pl.BlockSpec((1, rows, A_V_DIM), lambda bi, hi, i: (bi, i, hi)),
        pl.BlockSpec((1, A_V_DIM), lambda bi, hi, i: (0, 0)),
    ]
    args = [lamv, q2, ga, gsub]
    tiles = []
    for k_s, vt_s in sources:
        ls = k_s.shape[2]
        nt, _, tk = vt_s.shape[2:]
        tiles.append(nt)
        in_specs.append(pl.BlockSpec((1, 1, ls, A_V_DIM), lambda bi, hi, i: (bi, hi, 0, 0)))
        in_specs.append(pl.BlockSpec((1, 1, nt, A_V_DIM, tk), lambda bi, hi, i: (bi, hi, 0, 0, 0)))
        args += [k_s, vt_s]
    tk_max = max(vt_s.shape[-1] for _, vt_s in sources)
    return pl.pallas_call(
        functools.partial(_attn_kernel, tq=tq, nq=nq, n_src=len(sources), tiles=tuple(tiles),
                          lam_init=lam_init, bounded=bounded),
        grid=(b, A_HEADS, n // rows),
        in_specs=in_specs,
        out_specs=pl.BlockSpec((1, rows, A_V_DIM), lambda bi, hi, i: (bi, i, hi)),
        out_shape=jax.ShapeDtypeStruct((b, n, A_WIDTH), _BF16),
        scratch_shapes=[pltpu.VMEM((2, tk_max, 2 * tq), _BF16),
                        pltpu.VMEM((1, 2 * tq), _F32),
                        pltpu.VMEM((nq, 1, 2 * tq), _F32),
                        pltpu.VMEM((nq, A_V_DIM, 2 * tq), _F32)],
        compiler_params=_params(3),
        name="diff_attn" if bounded else "diff_attn_general",
    )(*args)


SAFE_EXP2_RANGE = 60.0


def _attention(lamv, q2, ga, gsub, sources, gq, gk, *, tq, nq, lam_init):
    bound = A_QK_DIM * Q_SCALE * jnp.max(jnp.abs(gq)) * jnp.max(jnp.abs(gk))
    call = lambda bounded: functools.partial(_attn_call, tq=tq, nq=nq, lam_init=lam_init, bounded=bounded)
    return lax.cond(bound <= SAFE_EXP2_RANGE, call(True), call(False), lamv, q2, ga, gsub, sources)


def _dft1_kernel(f1_ref, x_ref, o_ref, *, rows_per_step):
    n1 = x_ref.shape[1]
    x = x_ref[0].reshape(n1, rows_per_step * B_WIDTH).astype(_BF16)
    z = jnp.dot(f1_ref[...], x, preferred_element_type=_F32)
    o_ref[0] = z.reshape(2 * n1, rows_per_step, B_WIDTH).astype(_BF16)


def _dft1_call(f1, x4d, *, rows_per_step):
    b, n1, n2, _ = x4d.shape
    return pl.pallas_call(
        functools.partial(_dft1_kernel, rows_per_step=rows_per_step),
        grid=(b, n2 // rows_per_step),
        in_specs=[pl.BlockSpec((2 * n1, n1), lambda bi, j: (0, 0)),
                  pl.BlockSpec((1, n1, rows_per_step, B_WIDTH), lambda bi, j: (bi, 0, j, 0))],
        out_specs=pl.BlockSpec((1, 2 * n1, rows_per_step, B_WIDTH), lambda bi, j: (bi, 0, j, 0)),
        out_shape=jax.ShapeDtypeStruct((b, 2 * n1, n2, B_WIDTH), _BF16),
        compiler_params=_params(2),
        name="dft_stage1",
    )(f1, x4d)


def _dft2_kernel(*refs, g, n2, two_stage, norm):
    if two_stage:
        ar_ref, ai_ref, twc_ref, tws_ref, w2c_ref, mix_ref, wf_ref, bf_ref, out_ref = refs
        ar = ar_ref[0, 0].reshape(g * n2, B_WIDTH).astype(_F32)
        ai = ai_ref[0, 0].reshape(g * n2, B_WIDTH).astype(_F32)
        tc = twc_ref[...].reshape(g * n2, V7X_LANES)
        ts = tws_ref[...].reshape(g * n2, V7X_LANES)
        c2 = jnp.concatenate([tc, tc], axis=1)
        s2 = jnp.concatenate([ts, ts], axis=1)
        a = jnp.concatenate([ar * c2 - ai * s2, ar * s2 + ai * c2], axis=1).astype(_BF16)
        uv = jnp.dot(a, mix_ref[...], preferred_element_type=_F32)
    else:
        ar_ref, w2c_ref, mix_ref, wf_ref, bf_ref, out_ref = refs
        uv = jnp.dot(ar_ref[0, 0, 0].astype(_BF16), mix_ref[:B_WIDTH, :], preferred_element_type=_F32)
    u = uv[:, :B_WIDTH].astype(_BF16)
    v = uv[:, B_WIDTH:].astype(_BF16)
    w2c = w2c_ref[...]
    frs = []
    for j in range(g):
        rows = slice(j * n2, (j + 1) * n2)
        frs.append(jnp.dot(w2c, jnp.concatenate([u[rows], v[rows]], axis=0), preferred_element_type=_F32))
    fr = jnp.concatenate(frs, axis=0) * norm
    y = jnp.dot(fr.astype(_BF16), wf_ref[...], preferred_element_type=_F32) + bf_ref[...]
    if two_stage:
        out_ref[0] = jnp.swapaxes(y.reshape(g, n2, B_WIDTH), 0, 1).astype(_BF16)
    else:
        out_ref[0] = y.astype(_BF16)


def _dft_tables(n1, n2):
    n = n1 * n2
    k1 = np.arange(n1)
    a1 = 2.0 * np.pi * ((k1[:, None] * k1[None, :]) % n1) / n1
    f1 = np.concatenate([np.cos(a1), np.sin(a1)], axis=0)
    k2 = np.arange(n2)
    a2 = 2.0 * np.pi * ((k2[:, None] * k2[None, :]) % n2) / n2
    w2c = np.concatenate([np.cos(a2), -np.sin(a2)], axis=1)
    gch = np.arange(B_GROUP_DIM)
    ag = 2.0 * np.pi * ((gch[:, None] * gch[None, :]) % B_GROUP_DIM) / B_GROUP_DIM
    eye = np.eye(B_GROUPS)
    cg = np.kron(eye, np.cos(ag))
    sg = np.kron(eye, np.sin(ag))
    mix = np.block([[cg, sg], [-sg, cg]])
    to_bf16 = lambda t: jnp.asarray(t, _F32).astype(_BF16)
    return to_bf16(f1), to_bf16(w2c), to_bf16(mix), 1.0 / math.sqrt(n * B_GROUP_DIM)


def _twiddles(n1, n2):
    k1 = jnp.arange(n1, dtype=jnp.int32)[:, None]
    m2 = jnp.arange(n2, dtype=jnp.int32)[None, :]
    ang = (k1 * m2).astype(_F32) * (2.0 * math.pi / (n1 * n2))
    shape = (n1, n2, V7X_LANES)
    return (jnp.broadcast_to(jnp.cos(ang)[:, :, None], shape), jnp.broadcast_to(jnp.sin(ang)[:, :, None], shape))


def _fourier(f, wf_bd, bf_row):
    b, n, _ = f.shape
    two_stage = n > 2 * V7X_LANES
    n2 = V7X_LANES if two_stage else n
    n1 = n // n2
    g = min(n1, V7X_BF16_SUBLANES)
    f1, w2c, mix, norm = _dft_tables(n1, n2)
    full = lambda *shape: pl.BlockSpec(shape, lambda bi, j: (0,) * len(shape))
    consts = [w2c, mix, wf_bd, bf_row]
    const_specs = [full(*w2c.shape), full(*mix.shape), full(B_WIDTH, B_WIDTH), full(1, B_WIDTH)]
    if two_stage:
        a = _dft1_call(f1, f.reshape(b, n1, n2, B_WIDTH), rows_per_step=2 * V7X_BF16_SUBLANES)
        a5 = a.reshape(b, 2, n1, n2, B_WIDTH)
        out_shape = jax.ShapeDtypeStruct((b, n2, n1, B_WIDTH), _BF16)
        out_spec = pl.BlockSpec((1, n2, g, B_WIDTH), lambda bi, j: (bi, 0, j, 0))
        twc, tws = _twiddles(n1, n2)
        args = [a5, a5, twc, tws] + consts
        in_specs = [pl.BlockSpec((1, 1, g, n2, B_WIDTH), lambda bi, j: (bi, 0, j, 0, 0)),
                    pl.BlockSpec((1, 1, g, n2, B_WIDTH), lambda bi, j: (bi, 1, j, 0, 0)),
                    pl.BlockSpec((g, n2, V7X_LANES), lambda bi, j: (j, 0, 0)),
                    pl.BlockSpec((g, n2, V7X_LANES), lambda bi, j: (j, 0, 0))] + const_specs
    else:
        args = [f.reshape(b, 1, 1, n2, B_WIDTH)] + consts
        in_specs = [pl.BlockSpec((1, 1, 1, n2, B_WIDTH), lambda bi, j: (bi, 0, 0, 0, 0))] + const_specs
        out_shape = jax.ShapeDtypeStruct((b, n2, B_WIDTH), _BF16)
        out_spec = pl.BlockSpec((1, n2, B_WIDTH), lambda bi, j: (bi, 0, 0))
    out = pl.pallas_call(
        functools.partial(_dft2_kernel, g=g, n2=n2, two_stage=two_stage, norm=norm),
        grid=(b, n1 // g),
        in_specs=in_specs,
        out_specs=out_spec,
        out_shape=out_shape,
        compiler_params=_params(2),
        name="dft_stage2",
    )(*args)
    return out.reshape(b, n, B_WIDTH)


def _merge_kernel(oa_ref, ob_ref, gb_ref, oc_ref, x_ref, mod_ref, gn_ref, wm_ref, wa_ref, wb_ref, wc_ref, wo_ref,
                  o_ref, *, d, sub):
    mod = mod_ref[0]
    for r0 in range(0, x_ref.shape[1], sub):
        rows = slice(r0, r0 + sub)
        x = x_ref[0, rows, :]
        hb = _modulated_norm(x, mod, gn_ref[...], d)
        branches = (jnp.dot(oa_ref[0, rows, :], wa_ref[...], preferred_element_type=_F32),
                    jnp.dot(ob_ref[0, rows, :] * gb_ref[0, rows, :], wb_ref[...], preferred_element_type=_F32),
                    jnp.dot(oc_ref[0, rows, :], wc_ref[...], preferred_element_type=_F32))
        m = _sigmoid(jnp.dot(hb, wm_ref[:, OFF_MERGE:], preferred_element_type=_F32))
        y = None
        for j, yj in enumerate(branches):
            mj = m[:, j * d:(j + 1) * d]
            y = mj * yj if y is None else y + mj * yj
        out = jnp.dot(y.astype(_BF16), wo_ref[...], preferred_element_type=_F32)
        o_ref[0, rows, :] = x + mod[:, 2 * d:] * out


def _merge_call(oa, ob, gb, oc, x, mod, gn, w_in_bf, wa, wb, wc, wo, *, tm, sub, layer):
    b, n, d = x.shape
    row = lambda w: pl.BlockSpec((1, tm, w), lambda bi, i: (bi, i, 0))
    of_layer = lambda *shape: pl.BlockSpec((None,) + shape, lambda bi, i: (layer,) + (0,) * len(shape))
    return pl.pallas_call(
        functools.partial(_merge_kernel, d=d, sub=sub),
        grid=(b, n // tm),
        in_specs=[row(A_WIDTH), row(B_WIDTH), row(B_WIDTH), row(C_WIDTH), row(d),
                  pl.BlockSpec((1, 1, 3 * d), lambda bi, i: (bi, 0, 0)),
                  pl.BlockSpec((1, d), lambda bi, i: (0, 0)),
                  of_layer(d, w_in_bf.shape[-1]), of_layer(A_WIDTH, d), of_layer(B_WIDTH, d),
                  of_layer(C_WIDTH, d), of_layer(d, d)],
        out_specs=row(d),
        out_shape=jax.ShapeDtypeStruct((b, n, d), _F32),
        compiler_params=_params(2),
        name="merge",
    )(oa, ob, gb, oc, x, mod, gn, w_in_bf, wa, wb, wc, wo)


def _rope_tables(n):
    n_rows = n // GRID_W
    freqs = ROPE_BASE ** (-jnp.arange(0, AXIS_DIM, 2, dtype=_F32) / AXIS_DIM)
    ang_r = jnp.arange(n_rows, dtype=_F32)[:, None] * freqs
    ang_c = jnp.arange(GRID_W, dtype=_F32)[:, None] * freqs

    def on_grid(fn):
        shape = (n_rows, GRID_W, AXIS_DIM // 2)
        r = jnp.broadcast_to(fn(ang_r)[:, None, :], shape)
        c = jnp.broadcast_to(fn(ang_c)[None, :, :], shape)
        return jnp.concatenate([r, r, c, c], axis=-1).reshape(n, A_QK_DIM)

    cos, sin = on_grid(jnp.cos), on_grid(jnp.sin)
    first_half = (jnp.arange(A_QK_DIM) % AXIS_DIM) < AXIS_DIM // 2
    sa = jnp.where(first_half, -sin, 0.0)
    sb = jnp.where(first_half, 0.0, sin)
    tile2 = lambda t: jnp.concatenate([t, t], axis=-1)
    return tile2(cos), tile2(sa), tile2(sb)


def _no_rope_tables(n):
    return (jnp.ones((n, V7X_LANES), _F32), jnp.zeros((n, V7X_LANES), _F32), jnp.zeros((n, V7X_LANES), _F32))


def kernel(x, c, ctx, c_ctx, w_ada, b_ada, g_norm, w_in, g_q, g_k, lam_q1, lam_k1, lam_q2, lam_k2, g_sub,
           w_f, b_f, ln_g, ln_b, w_s, b_s, w_br_a, w_br_b, w_br_c, w_out):
    b, n, d = x.shape
    n_ctx = ctx.shape[1]
    depth = w_in.shape[0]
    tm_lat = min(n, 512)
    tm_ctx = min(n_ctx, 512)

    n_rows = V7X_SUBLANES * pl.cdiv(b + 1, V7X_SUBLANES)
    rows = jnp.concatenate([c, c_ctx[None, :], jnp.zeros((n_rows - b - 1, d), _F32)], axis=0)
    mod_all = _mod_call(rows, w_ada, b_ada)

    rope_lat = _rope_tables(n)
    rope_ctx = _no_rope_tables(n_ctx)
    group_mean = np.kron(np.eye(2 * A_V_DIM // A_QK_DIM), np.full((A_QK_DIM, A_QK_DIM), 1.0 / A_QK_DIM))
    bmat = jnp.asarray(group_mean, _BF16)

    w_in_bf = w_in.astype(_BF16)
    w_s_bf = w_s.astype(_BF16)
    merge_w = (w_in_bf, w_br_a.astype(_BF16), w_br_b.astype(_BF16), w_br_c.astype(_BF16), w_out.astype(_BF16))

    for l in range(depth):
        last = l == depth - 1
        lam_init = 0.8 - 0.6 * math.exp(-0.3 * l)
        lamv = jnp.stack([lam_q1[l], lam_k1[l], lam_q2[l], lam_k2[l]], axis=0)
        shared = dict(
            gn=g_norm[l][None, :], w_bf=w_in_bf,
            gq=jnp.tile(g_q[l], 2)[None, :], gk=jnp.tile(g_k[l], 2)[None, :], bmat=bmat,
            lng=ln_g[l][None, :], lnb=ln_b[l][None, :], ws_bf=w_s_bf,
            bsm=jnp.repeat(b_s[l].T, C_GROUP_DIM, axis=1), layer=l)
        gsub = g_sub[l][None, :]
        wf_bd = jax.scipy.linalg.block_diag(*[w_f[l, gi] for gi in range(B_GROUPS)]).astype(_BF16)
        bf_row = b_f[l].reshape(1, B_WIDTH)

        mod_lat = mod_all[l, :b][:, None, :]
        mod_ctx = jnp.broadcast_to(mod_all[l, b][None, None, :], (b, 1, 3 * d))

        ctx_proj = _inproj_call(ctx, mod_ctx, cos=rope_ctx[0], sa=rope_ctx[1], sb=rope_ctx[2], tm=tm_ctx,
                                sub=tm_ctx, tk=tm_ctx, kv_only=last, **shared)
        if last:
            ck, cvt = ctx_proj
        else:
            cq2, ck, cvt, cf, coc, cga, cgb = ctx_proj
        q2, k, vt, f, oc, ga, gb = _inproj_call(
            x, mod_lat, cos=rope_lat[0], sa=rope_lat[1], sb=rope_lat[2], tm=min(n, 2 * tm_lat), sub=tm_lat,
            tk=tm_lat, **shared)

        oa = _attention(lamv, q2, ga, gsub, [(ck, cvt), (k, vt)], g_q[l], g_k[l], tq=tm_lat,
                        nq=min(4, n // tm_lat), lam_init=lam_init)
        ob = _fourier(f, wf_bd, bf_row)
        x_new = _merge_call(oa, ob, gb, oc, x, mod_lat, shared["gn"], *merge_w, tm=min(n, 2 * tm_lat), sub=tm_lat,
                            layer=l)

        if not last:
            coa = _attention(lamv, cq2, cga, gsub, [(ck, cvt)], g_q[l], g_k[l], tq=tm_ctx, nq=1,
                             lam_init=lam_init)
            cob = _fourier(cf, wf_bd, bf_row)
            ctx = _merge_call(coa, cob, cgb, coc, ctx, mod_ctx, shared["gn"], *merge_w, tm=tm_ctx, sub=tm_ctx,
                              layer=l)
        x = x_new
    return x
```

```python
import functools
import math

import numpy as np
import jax
import jax.numpy as jnp
from jax import lax
from jax.experimental import pallas as pl
from jax.experimental.pallas import tpu as pltpu

A_HEADS = 4
A_QK_DIM = 64
A_V_DIM = 2 * A_QK_DIM
A_WIDTH = A_HEADS * A_V_DIM
B_GROUPS = 4
B_GROUP_DIM = 64
B_WIDTH = B_GROUPS * B_GROUP_DIM
C_GROUPS = 4
C_GROUP_DIM = 64
C_WIDTH = C_GROUPS * C_GROUP_DIM
CHUNK = 128
GRID_W = 64
ROPE_BASE = 10000.0
AXIS_DIM = A_QK_DIM // 2
EPS = 1e-6

OFF_Q = 0
OFF_K = OFF_Q + A_WIDTH
OFF_V = OFF_K + A_WIDTH
OFF_F = OFF_V + A_WIDTH
OFF_U = OFF_F + B_WIDTH
OFF_VC = OFF_U + C_WIDTH
OFF_GATE = OFF_VC + C_WIDTH
GATE_WIDTH = A_WIDTH + B_WIDTH + C_WIDTH
OFF_MERGE = OFF_GATE + GATE_WIDTH

V7X_LANES = 128
V7X_SUBLANES = 8
V7X_BF16_SUBLANES = 16
V7X_MXU_DIM = 256
V7X_VMEM_LIMIT_BYTES = 56 * 1024 * 1024

LOG2E = math.log2(math.e)
Q_SCALE = (A_QK_DIM ** -0.5) * LOG2E

_F32 = jnp.float32
_BF16 = jnp.bfloat16
_NT = (((1,), (1,)), ((), ()))


def _sigmoid(x):
    return 0.5 * jnp.tanh(0.5 * x) + 0.5


def _params(n_axes):
    return pltpu.CompilerParams(dimension_semantics=("arbitrary",) * n_axes,
                                vmem_limit_bytes=V7X_VMEM_LIMIT_BYTES)


def _mod_kernel(c_ref, w_ref, b_ref, o_ref):
    cc = c_ref[...]
    s = cc * _sigmoid(cc)
    o_ref[0] = jnp.dot(s, w_ref[0], preferred_element_type=_F32) + b_ref[0]


def _mod_call(rows, w_ada, b_ada):
    depth, d, d3 = w_ada.shape
    nb = d3 // d
    return pl.pallas_call(
        _mod_kernel,
        grid=(depth, nb),
        in_specs=[pl.BlockSpec((rows.shape[0], d), lambda l, j: (0, 0)),
                  pl.BlockSpec((1, d, d), lambda l, j: (l, 0, j)),
                  pl.BlockSpec((1, 1, d), lambda l, j: (l, 0, j))],
        out_specs=pl.BlockSpec((1, rows.shape[0], d), lambda l, j: (l, 0, j)),
        out_shape=jax.ShapeDtypeStruct((depth, rows.shape[0], d3), _F32),
        compiler_params=_params(2),
        name="adaln_mod",
    )(rows, w_ada, b_ada.reshape(depth, 1, d3))


def _modulated_norm(x, mod, gn, d):
    ms = jnp.mean(x * x, axis=-1, keepdims=True)
    gain = gn * (1.0 + mod[:, d:2 * d])
    return (x * lax.rsqrt(ms + EPS) * gain + mod[:, :d]).astype(_BF16)


def _inproj_kernel(x_ref, mod_ref, gn_ref, w_ref, gq_ref, gk_ref, cos_ref, sa_ref, sb_ref,
                   bmat_ref, lng_ref, lnb_ref, ws_ref, bsm_ref, *out_refs, sub, tk, d, kv_only):
    if kv_only:
        k_ref, vt_ref = out_refs
    else:
        q2_ref, k_ref, vt_ref, f_ref, oc_ref, ga_ref, gb_ref = out_refs
    pair = 2 * A_V_DIM
    bmat = bmat_ref[...]
    lo = (lax.broadcasted_iota(jnp.int32, (sub, pair), 1) % A_V_DIM) < A_QK_DIM
    lo_c = lax.broadcasted_iota(jnp.int32, (CHUNK, V7X_LANES), 1) < C_GROUP_DIM
    bsm = bsm_ref[...]
    twice = lambda t: jnp.concatenate([t, t], axis=1)
    gq = twice(gq_ref[...]) * Q_SCALE
    gk = twice(gk_ref[...])

    for si in range(x_ref.shape[1] // sub):
        rows = slice(si * sub, (si + 1) * sub)
        hb = _modulated_norm(x_ref[0, rows, :], mod_ref[0], gn_ref[...], d)

        def proj(off, width, hb=hb):
            return jnp.dot(hb, w_ref[:, off:off + width], preferred_element_type=_F32)

        cos = twice(cos_ref[rows, :])
        sa = twice(sa_ref[rows, :])
        sb = twice(sb_ref[rows, :])

        def qk_heads(p_2h, g, cos=cos, sa=sa, sb=sb):
            msq = jnp.dot((p_2h * p_2h).astype(_BF16), bmat, preferred_element_type=_F32)
            t = p_2h * lax.rsqrt(msq + EPS) * g
            return (t * cos + pltpu.roll(t, pair - AXIS_DIM // 2, 1) * sa
                    + pltpu.roll(t, AXIS_DIM // 2, 1) * sb)

        if not kv_only:
            pq = proj(OFF_Q, A_WIDTH)
            for hp in range(A_HEADS // 2):
                t = qk_heads(pq[:, hp * pair:(hp + 1) * pair], gq)
                t_lo = jnp.where(lo, t, 0.0).astype(_BF16)
                t_hi = jnp.where(lo, 0.0, t).astype(_BF16)
                for h2 in range(2):
                    q2_ref[0, 2 * hp + h2, 0, rows, :] = t_lo[:, h2 * A_V_DIM:(h2 + 1) * A_V_DIM]
                    q2_ref[0, 2 * hp + h2, 1, rows, :] = t_hi[:, h2 * A_V_DIM:(h2 + 1) * A_V_DIM]

        pk = proj(OFF_K, A_WIDTH)
        for hp in range(A_HEADS // 2):
            t = qk_heads(pk[:, hp * pair:(hp + 1) * pair], gk).astype(_BF16)
            for h2 in range(2):
                k_ref[0, 2 * hp + h2, rows, :] = t[:, h2 * A_V_DIM:(h2 + 1) * A_V_DIM]

        pv = proj(OFF_V, A_WIDTH)
        for hh in range(A_HEADS):
            r0 = si * sub
            vt_ref[0, hh, r0 // tk, :, r0 % tk:r0 % tk + sub] = (
                pv[:, hh * A_V_DIM:(hh + 1) * A_V_DIM].T.astype(_BF16))
        if kv_only:
            continue

        f_ref[0, rows, :] = proj(OFF_F, B_WIDTH)

        u = proj(OFF_U, C_WIDTH)
        vc = proj(OFF_VC, C_WIDTH)
        mu = jnp.mean(vc, axis=-1, keepdims=True)
        dv = vc - mu
        var = jnp.mean(dv * dv, axis=-1, keepdims=True)
        vn = (dv * lax.rsqrt(var + EPS) * lng_ref[...] + lnb_ref[...]).astype(_BF16)

        pg = proj(OFF_GATE, GATE_WIDTH)
        gate = pg * _sigmoid(pg)
        ga_ref[0, rows, :] = gate[:, :A_WIDTH].astype(_BF16)
        gb_ref[0, rows, :] = gate[:, A_WIDTH:A_WIDTH + B_WIDTH].astype(_BF16)
        gate_c = gate[:, A_WIDTH + B_WIDTH:]

        for t in range(sub // CHUNK):
            crows = slice(t * CHUNK, (t + 1) * CHUNK)
            vl = vn[crows, :V7X_LANES]
            vr = vn[crows, V7X_LANES:]
            s_l = jnp.where(lo_c, jnp.dot(ws_ref[0], vl, preferred_element_type=_F32),
                            jnp.dot(ws_ref[1], vl, preferred_element_type=_F32))
            s_r = jnp.where(lo_c, jnp.dot(ws_ref[2], vr, preferred_element_type=_F32),
                            jnp.dot(ws_ref[3], vr, preferred_element_type=_F32))
            s = jnp.concatenate([s_l, s_r], axis=1) + bsm
            oc_ref[0, si * sub + t * CHUNK:si * sub + (t + 1) * CHUNK, :] = (
                u[crows] * s * gate_c[crows]).astype(_BF16)


def _inproj_call(x, mod, gn, w_bf, gq, gk, cos, sa, sb, bmat, lng, lnb, ws_bf, bsm, *, tm, sub, tk, layer,
                 kv_only=False):
    b, n, d = x.shape
    nt = n // tm
    full = lambda *shape: pl.BlockSpec(shape, lambda bi, i: (0,) * len(shape))
    of_layer = lambda *shape: pl.BlockSpec((None,) + shape, lambda bi, i: (layer,) + (0,) * len(shape))
    in_specs = [
        pl.BlockSpec((1, tm, d), lambda bi, i: (bi, i, 0)),
        pl.BlockSpec((1, 1, 3 * d), lambda bi, i: (bi, 0, 0)),
        full(1, d),
        of_layer(d, w_bf.shape[-1]),
        full(1, V7X_LANES),
        full(1, V7X_LANES),
        pl.BlockSpec((tm, V7X_LANES), lambda bi, i: (i, 0)),
        pl.BlockSpec((tm, V7X_LANES), lambda bi, i: (i, 0)),
        pl.BlockSpec((tm, V7X_LANES), lambda bi, i: (i, 0)),
        full(2 * A_V_DIM, 2 * A_V_DIM),
        full(1, C_WIDTH),
        full(1, C_WIDTH),
        of_layer(C_GROUPS, CHUNK, CHUNK),
        full(CHUNK, C_WIDTH),
    ]
    out_shape = [
        jax.ShapeDtypeStruct((b, A_HEADS, 2, n, A_V_DIM), _BF16),
        jax.ShapeDtypeStruct((b, A_HEADS, n, A_V_DIM), _BF16),
        jax.ShapeDtypeStruct((b, A_HEADS, n // tk, A_V_DIM, tk), _BF16),
        jax.ShapeDtypeStruct((b, n, B_WIDTH), _F32),
        jax.ShapeDtypeStruct((b, n, C_WIDTH), _BF16),
        jax.ShapeDtypeStruct((b, n, A_WIDTH), _BF16),
        jax.ShapeDtypeStruct((b, n, B_WIDTH), _BF16),
    ]
    out_specs = [
        pl.BlockSpec((1, A_HEADS, 2, tm, A_V_DIM), lambda bi, i: (bi, 0, 0, i, 0)),
        pl.BlockSpec((1, A_HEADS, tm, A_V_DIM), lambda bi, i: (bi, 0, i, 0)),
        pl.BlockSpec((1, A_HEADS, tm // tk, A_V_DIM, tk), lambda bi, i: (bi, 0, i, 0, 0)),
        pl.BlockSpec((1, tm, B_WIDTH), lambda bi, i: (bi, i, 0)),
        pl.BlockSpec((1, tm, C_WIDTH), lambda bi, i: (bi, i, 0)),
        pl.BlockSpec((1, tm, A_WIDTH), lambda bi, i: (bi, i, 0)),
        pl.BlockSpec((1, tm, B_WIDTH), lambda bi, i: (bi, i, 0)),
    ]
    if kv_only:
        out_shape, out_specs = out_shape[1:3], out_specs[1:3]
    return pl.pallas_call(
        functools.partial(_inproj_kernel, sub=sub, tk=tk, d=d, kv_only=kv_only),
        grid=(b, nt),
        in_specs=in_specs,
        out_specs=out_specs,
        out_shape=out_shape,
        compiler_params=_params(2),
        name="inproj",
    )(x, mod, gn, w_bf, gq, gk, cos, sa, sb, bmat, lng, lnb, ws_bf, bsm)


def _attn_kernel(*refs, tq, nq, n_src, tiles, lam_init, bounded):
    lam_ref, q2_ref, ga_ref, gsub_ref = refs[:4]
    src_refs = refs[4:4 + 2 * n_src]
    out_ref = refs[4 + 2 * n_src]
    p_sc, m_sc, l_sc, acc_sc = refs[5 + 2 * n_src:]

    def queries(u):
        rows = slice(u * tq, (u + 1) * tq)
        return jnp.concatenate([q2_ref[0, 0, 0, rows, :], q2_ref[0, 0, 1, rows, :]], axis=0)

    l_sc[...] = jnp.zeros(l_sc.shape, _F32)
    acc_sc[...] = jnp.zeros(acc_sc.shape, _F32)
    lv = lam_ref[...]
    lam = (jnp.exp(jnp.sum(lv[0:1] * lv[1:2], axis=-1, keepdims=True))
           - jnp.exp(jnp.sum(lv[2:3] * lv[3:4], axis=-1, keepdims=True)) + lam_init)

    def finalize(u):
        inv = 1.0 / l_sc[u]
        acc = acc_sc[u]
        o = acc[:, :tq] * inv[:, :tq] - lam * (acc[:, tq:] * inv[:, tq:])
        ms = jnp.mean(o * o, axis=0, keepdims=True)
        on = o * lax.rsqrt(ms + EPS)
        rows = slice(u * tq, (u + 1) * tq)
        out = on.T * (gsub_ref[...] * (1.0 - lam_init)) * ga_ref[0, rows, :].astype(_F32)
        out_ref[0, rows, :] = out.astype(_BF16)

    def tile_len(si):
        return src_refs[2 * si + 1].shape[-1]

    def k_tile(si, t):
        tk = tile_len(si)
        if isinstance(t, int):
            return src_refs[2 * si][0, 0, t * tk:(t + 1) * tk, :]
        return src_refs[2 * si][0, 0, pl.ds(pl.multiple_of(t * tk, tk), tk), :]

    def vt_tile(si, t):
        return src_refs[2 * si + 1][0, 0, t]

    if bounded:
        def exponentials(slot, u, k_t):
            p = jnp.exp2(lax.dot_general(k_t, queries(u), _NT, preferred_element_type=_F32))
            l_sc[u] += jnp.sum(p, axis=0, keepdims=True)
            p_sc[slot, :k_t.shape[0], :] = p.astype(_BF16)

        def weighted_values(slot, u, vt_t):
            acc_sc[u] += jnp.dot(vt_t, p_sc[slot, :vt_t.shape[1], :], preferred_element_type=_F32)

        stages = [(u, si, t) for u in range(nq) for si in range(n_src) for t in range(tiles[si])]
        exponentials(0, stages[0][0], k_tile(*stages[0][1:]))
        for j, (u, si, t) in enumerate(stages):
            if j + 1 < len(stages):
                exponentials((j + 1) % 2, stages[j + 1][0], k_tile(*stages[j + 1][1:]))
            weighted_values(j % 2, u, vt_tile(si, t))
            if j + 1 == len(stages) or stages[j + 1][0] != u:
                finalize(u)
    else:
        for u in range(nq):
            m_sc[...] = jnp.full(m_sc.shape, -jnp.inf, _F32)

            def step(k_t, vt_t, u=u):
                s = lax.dot_general(k_t, queries(u), _NT, preferred_element_type=_F32)
                m_prev = m_sc[...]
                m_new = jnp.maximum(m_prev, jnp.max(s, axis=0, keepdims=True))
                alpha = jnp.exp2(m_prev - m_new)
                p = jnp.exp2(s - m_new)
                l_sc[u] = alpha * l_sc[u] + jnp.sum(p, axis=0, keepdims=True)
                acc_sc[u] = alpha * acc_sc[u] + jnp.dot(vt_t, p.astype(_BF16), preferred_element_type=_F32)
                m_sc[...] = m_new

            for si in range(n_src):
                if tiles[si] == 1:
                    step(k_tile(si, 0), vt_tile(si, 0))
                else:
                    def body(t, carry, si=si, step=step):
                        step(k_tile(si, t), vt_tile(si, t))
                        return carry
                    lax.fori_loop(0, tiles[si], body, 0)
            finalize(u)


def _attn_call(lamv, q2, ga, gsub, sources, *, tq, nq, lam_init, bounded):
    b, _, _, n, _ = q2.shape
    rows = nq * tq
    in_specs = [
        pl.BlockSpec(lamv.shape, lambda bi, hi, i: (0, 0)),
        pl.BlockSpec((1, 1, 2, rows, A_V_DIM), lambda bi, hi, i: (bi, hi, 0, i, 0)),
        pl.BlockSpec((1, rows, A_V_DIM), lambda bi, hi, i: (bi, i, hi)),
        pl.BlockSpec((1, A_V_DIM), lambda bi, hi, i: (0, 0)),
    ]
    args = [lamv, q2, ga, gsub]
    tiles = []
    for k_s, vt_s in sources:
        ls = k_s.shape[2]
        nt, _, tk = vt_s.shape[2:]
        tiles.append(nt)
        in_specs.append(pl.BlockSpec((1, 1, ls, A_V_DIM), lambda bi, hi, i: (bi, hi, 0, 0)))
        in_specs.append(pl.BlockSpec((1, 1, nt, A_V_DIM, tk), lambda bi, hi, i: (bi, hi, 0, 0, 0)))
        args += [k_s, vt_s]
    tk_max = max(vt_s.shape[-1] for _, vt_s in sources)
    return pl.pallas_call(
        functools.partial(_attn_kernel, tq=tq, nq=nq, n_src=len(sources), tiles=tuple(tiles),
                          lam_init=lam_init, bounded=bounded),
        grid=(b, A_HEADS, n // rows),
        in_specs=in_specs,
        out_specs=pl.BlockSpec((1, rows, A_V_DIM), lambda bi, hi, i: (bi, i, hi)),
        out_shape=jax.ShapeDtypeStruct((b, n, A_WIDTH), _BF16),
        scratch_shapes=[pltpu.VMEM((2, tk_max, 2 * tq), _BF16),
                        pltpu.VMEM((1, 2 * tq), _F32),
                        pltpu.VMEM((nq, 1, 2 * tq), _F32),
                        pltpu.VMEM((nq, A_V_DIM, 2 * tq), _F32)],
        compiler_params=_params(3),
        name="diff_attn" if bounded else "diff_attn_general",
    )(*args)


SAFE_EXP2_RANGE = 60.0


def _attention(lamv, q2, ga, gsub, sources, gq, gk, *, tq, nq, lam_init):
    bound = A_QK_DIM * Q_SCALE * jnp.max(jnp.abs(gq)) * jnp.max(jnp.abs(gk))
    call = lambda bounded: functools.partial(_attn_call, tq=tq, nq=nq, lam_init=lam_init, bounded=bounded)
    return lax.cond(bound <= SAFE_EXP2_RANGE, call(True), call(False), lamv, q2, ga, gsub, sources)


def _dft1_kernel(f1_ref, x_ref, o_ref, *, rows_per_step):
    n1 = x_ref.shape[1]
    x = x_ref[0].reshape(n1, rows_per_step * B_WIDTH).astype(_BF16)
    z = jnp.dot(f1_ref[...], x, preferred_element_type=_F32)
    o_ref[0] = z.reshape(2 * n1, rows_per_step, B_WIDTH).astype(_BF16)


def _dft1_call(f1, x4d, *, rows_per_step):
    b, n1, n2, _ = x4d.shape
    return pl.pallas_call(
        functools.partial(_dft1_kernel, rows_per_step=rows_per_step),
        grid=(b, n2 // rows_per_step),
        in_specs=[pl.BlockSpec((2 * n1, n1), lambda bi, j: (0, 0)),
                  pl.BlockSpec((1, n1, rows_per_step, B_WIDTH), lambda bi, j: (bi, 0, j, 0))],
        out_specs=pl.BlockSpec((1, 2 * n1, rows_per_step, B_WIDTH), lambda bi, j: (bi, 0, j, 0)),
        out_shape=jax.ShapeDtypeStruct((b, 2 * n1, n2, B_WIDTH), _BF16),
        compiler_params=_params(2),
        name="dft_stage1",
    )(f1, x4d)


def _dft2_kernel(*refs, g, n2, two_stage, norm):
    if two_stage:
        ar_ref, ai_ref, twc_ref, tws_ref, w2c_ref, mix_ref, wf_ref, bf_ref, out_ref = refs
        ar = ar_ref[0, 0].reshape(g * n2, B_WIDTH).astype(_F32)
        ai = ai_ref[0, 0].reshape(g * n2, B_WIDTH).astype(_F32)
        tc = twc_ref[...].reshape(g * n2, V7X_LANES)
        ts = tws_ref[...].reshape(g * n2, V7X_LANES)
        c2 = jnp.concatenate([tc, tc], axis=1)
        s2 = jnp.concatenate([ts, ts], axis=1)
        a = jnp.concatenate([ar * c2 - ai * s2, ar * s2 + ai * c2], axis=1).astype(_BF16)
        uv = jnp.dot(a, mix_ref[...], preferred_element_type=_F32)
    else:
        ar_ref, w2c_ref, mix_ref, wf_ref, bf_ref, out_ref = refs
        uv = jnp.dot(ar_ref[0, 0, 0].astype(_BF16), mix_ref[:B_WIDTH, :], preferred_element_type=_F32)
    u = uv[:, :B_WIDTH].astype(_BF16)
    v = uv[:, B_WIDTH:].astype(_BF16)
    w2c = w2c_ref[...]
    frs = []
    for j in range(g):
        rows = slice(j * n2, (j + 1) * n2)
        frs.append(jnp.dot(w2c, jnp.concatenate([u[rows], v[rows]], axis=0), preferred_element_type=_F32))
    fr = jnp.concatenate(frs, axis=0) * norm
    y = jnp.dot(fr.astype(_BF16), wf_ref[...], preferred_element_type=_F32) + bf_ref[...]
    if two_stage:
        out_ref[0] = jnp.swapaxes(y.reshape(g, n2, B_WIDTH), 0, 1).astype(_BF16)
    else:
        out_ref[0] = y.astype(_BF16)


def _dft_tables(n1, n2):
    n = n1 * n2
    k1 = np.arange(n1)
    a1 = 2.0 * np.pi * ((k1[:, None] * k1[None, :]) % n1) / n1
    f1 = np.concatenate([np.cos(a1), np.sin(a1)], axis=0)
    k2 = np.arange(n2)
    a2 = 2.0 * np.pi * ((k2[:, None] * k2[None, :]) % n2) / n2
    w2c = np.concatenate([np.cos(a2), -np.sin(a2)], axis=1)
    gch = np.arange(B_GROUP_DIM)
    ag = 2.0 * np.pi * ((gch[:, None] * gch[None, :]) % B_GROUP_DIM) / B_GROUP_DIM
    eye = np.eye(B_GROUPS)
    cg = np.kron(eye, np.cos(ag))
    sg = np.kron(eye, np.sin(ag))
    mix = np.block([[cg, sg], [-sg, cg]])
    to_bf16 = lambda t: jnp.asarray(t, _F32).astype(_BF16)
    return to_bf16(f1), to_bf16(w2c), to_bf16(mix), 1.0 / math.sqrt(n * B_GROUP_DIM)


def _twiddles(n1, n2):
    k1 = jnp.arange(n1, dtype=jnp.int32)[:, None]
    m2 = jnp.arange(n2, dtype=jnp.int32)[None, :]
    ang = (k1 * m2).astype(_F32) * (2.0 * math.pi / (n1 * n2))
    shape = (n1, n2, V7X_LANES)
    return (jnp.broadcast_to(jnp.cos(ang)[:, :, None], shape), jnp.broadcast_to(jnp.sin(ang)[:, :, None], shape))


def _fourier(f, wf_bd, bf_row):
    b, n, _ = f.shape
    two_stage = n > 2 * V7X_LANES
    n2 = V7X_LANES if two_stage else n
    n1 = n // n2
    g = min(n1, V7X_BF16_SUBLANES)
    f1, w2c, mix, norm = _dft_tables(n1, n2)
    full = lambda *shape: pl.BlockSpec(shape, lambda bi, j: (0,) * len(shape))
    consts = [w2c, mix, wf_bd, bf_row]
    const_specs = [full(*w2c.shape), full(*mix.shape), full(B_WIDTH, B_WIDTH), full(1, B_WIDTH)]
    if two_stage:
        a = _dft1_call(f1, f.reshape(b, n1, n2, B_WIDTH), rows_per_step=2 * V7X_BF16_SUBLANES)
        a5 = a.reshape(b, 2, n1, n2, B_WIDTH)
        out_shape = jax.ShapeDtypeStruct((b, n2, n1, B_WIDTH), _BF16)
        out_spec = pl.BlockSpec((1, n2, g, B_WIDTH), lambda bi, j: (bi, 0, j, 0))
        twc, tws = _twiddles(n1, n2)
        args = [a5, a5, twc, tws] + consts
        in_specs = [pl.BlockSpec((1, 1, g, n2, B_WIDTH), lambda bi, j: (bi, 0, j, 0, 0)),
                    pl.BlockSpec((1, 1, g, n2, B_WIDTH), lambda bi, j: (bi, 1, j, 0, 0)),
                    pl.BlockSpec((g, n2, V7X_LANES), lambda bi, j: (j, 0, 0)),
                    pl.BlockSpec((g, n2, V7X_LANES), lambda bi, j: (j, 0, 0))] + const_specs
    else:
        args = [f.reshape(b, 1, 1, n2, B_WIDTH)] + consts
        in_specs = [pl.BlockSpec((1, 1, 1, n2, B_WIDTH), lambda bi, j: (bi, 0, 0, 0, 0))] + const_specs
        out_shape = jax.ShapeDtypeStruct((b, n2, B_WIDTH), _BF16)
        out_spec = pl.BlockSpec((1, n2, B_WIDTH), lambda bi, j: (bi, 0, 0))
    out = pl.pallas_call(
        functools.partial(_dft2_kernel, g=g, n2=n2, two_stage=two_stage, norm=norm),
        grid=(b, n1 // g),
        in_specs=in_specs,
        out_specs=out_spec,
        out_shape=out_shape,
        compiler_params=_params(2),
        name="dft_stage2",
    )(*args)
    return out.reshape(b, n, B_WIDTH)


def _merge_kernel(oa_ref, ob_ref, gb_ref, oc_ref, x_ref, mod_ref, gn_ref, wm_ref, wa_ref, wb_ref, wc_ref, wo_ref,
                  o_ref, *, d, sub):
    mod = mod_ref[0]
    for r0 in range(0, x_ref.shape[1], sub):
        rows = slice(r0, r0 + sub)
        x = x_ref[0, rows, :]
        hb = _modulated_norm(x, mod, gn_ref[...], d)
        branches = (jnp.dot(oa_ref[0, rows, :], wa_ref[...], preferred_element_type=_F32),
                    jnp.dot(ob_ref[0, rows, :] * gb_ref[0, rows, :], wb_ref[...], preferred_element_type=_F32),
                    jnp.dot(oc_ref[0, rows, :], wc_ref[...], preferred_element_type=_F32))
        m = _sigmoid(jnp.dot(hb, wm_ref[:, OFF_MERGE:], preferred_element_type=_F32))
        y = None
        for j, yj in enumerate(branches):
            mj = m[:, j * d:(j + 1) * d]
            y = mj * yj if y is None else y + mj * yj
        out = jnp.dot(y.astype(_BF16), wo_ref[...], preferred_element_type=_F32)
        o_ref[0, rows, :] = x + mod[:, 2 * d:] * out


def _merge_call(oa, ob, gb, oc, x, mod, gn, w_in_bf, wa, wb, wc, wo, *, tm, sub, layer):
    b, n, d = x.shape
    row = lambda w: pl.BlockSpec((1, tm, w), lambda bi, i: (bi, i, 0))
    of_layer = lambda *shape: pl.BlockSpec((None,) + shape, lambda bi, i: (layer,) + (0,) * len(shape))
    return pl.pallas_call(
        functools.partial(_merge_kernel, d=d, sub=sub),
        grid=(b, n // tm),
        in_specs=[row(A_WIDTH), row(B_WIDTH), row(B_WIDTH), row(C_WIDTH), row(d),
                  pl.BlockSpec((1, 1, 3 * d), lambda bi, i: (bi, 0, 0)),
                  pl.BlockSpec((1, d), lambda bi, i: (0, 0)),
                  of_layer(d, w_in_bf.shape[-1]), of_layer(A_WIDTH, d), of_layer(B_WIDTH, d),
                  of_layer(C_WIDTH, d), of_layer(d, d)],
        out_specs=row(d),
        out_shape=jax.ShapeDtypeStruct((b, n, d), _F32),
        compiler_params=_params(2),
        name="merge",
    )(oa, ob, gb, oc, x, mod, gn, w_in_bf, wa, wb, wc, wo)


def _rope_tables(n):
    n_rows = n // GRID_W
    lane = np.arange(V7X_LANES) % A_QK_DIM
    row_lane = lane < AXIS_DIM
    first_half = (lane % AXIS_DIM) < AXIS_DIM // 2
    freqs = ROPE_BASE ** (-jnp.arange(0, AXIS_DIM, 2, dtype=_F32) / AXIS_DIM)
    f_lane = freqs[lane % (AXIS_DIM // 2)]
    ang_r = jnp.arange(n_rows, dtype=_F32)[:, None] * f_lane
    ang_c = jnp.arange(GRID_W, dtype=_F32)[:, None] * f_lane

    def on_grid(fn, lanes):
        keep_r = row_lane & lanes
        keep_c = ~row_lane & lanes
        r = jnp.where(keep_r, fn(ang_r), 0.0)
        c = jnp.where(keep_c, fn(ang_c), 0.0)
        return (r[:, None, :] + c[None, :, :]).reshape(n, V7X_LANES)

    every = np.ones(V7X_LANES, bool)
    cos = on_grid(jnp.cos, every)
    sa = on_grid(lambda a: -jnp.sin(a), first_half)
    sb = on_grid(jnp.sin, ~first_half)
    return cos, sa, sb


def _no_rope_tables(n):
    return (jnp.ones((n, V7X_LANES), _F32), jnp.zeros((n, V7X_LANES), _F32), jnp.zeros((n, V7X_LANES), _F32))


def kernel(x, c, ctx, c_ctx, w_ada, b_ada, g_norm, w_in, g_q, g_k, lam_q1, lam_k1, lam_q2, lam_k2, g_sub,
           w_f, b_f, ln_g, ln_b, w_s, b_s, w_br_a, w_br_b, w_br_c, w_out):
    b, n, d = x.shape
    n_ctx = ctx.shape[1]
    depth = w_in.shape[0]
    tm_lat = min(n, 512)
    tm_ctx = min(n_ctx, 512)

    n_rows = V7X_SUBLANES * pl.cdiv(b + 1, V7X_SUBLANES)
    rows = jnp.concatenate([c, c_ctx[None, :], jnp.zeros((n_rows - b - 1, d), _F32)], axis=0)
    mod_all = _mod_call(rows, w_ada, b_ada)

    rope_lat = _rope_tables(n)
    rope_ctx = _no_rope_tables(n_ctx)
    group_mean = np.kron(np.eye(2 * A_V_DIM // A_QK_DIM), np.full((A_QK_DIM, A_QK_DIM), 1.0 / A_QK_DIM))
    bmat = jnp.asarray(group_mean, _BF16)

    w_in_bf = w_in.astype(_BF16)
    w_s_bf = w_s.astype(_BF16)
    merge_w = (w_in_bf, w_br_a.astype(_BF16), w_br_b.astype(_BF16), w_br_c.astype(_BF16), w_out.astype(_BF16))

    for l in range(depth):
        last = l == depth - 1
        lam_init = 0.8 - 0.6 * math.exp(-0.3 * l)
        lamv = jnp.stack([lam_q1[l], lam_k1[l], lam_q2[l], lam_k2[l]], axis=0)
        shared = dict(
            gn=g_norm[l][None, :], w_bf=w_in_bf,
            gq=jnp.tile(g_q[l], 2)[None, :], gk=jnp.tile(g_k[l], 2)[None, :], bmat=bmat,
            lng=ln_g[l][None, :], lnb=ln_b[l][None, :], ws_bf=w_s_bf,
            bsm=jnp.repeat(b_s[l].T, C_GROUP_DIM, axis=1), layer=l)
        gsub = g_sub[l][None, :]
        wf_bd = jax.scipy.linalg.block_diag(*[w_f[l, gi] for gi in range(B_GROUPS)]).astype(_BF16)
        bf_row = b_f[l].reshape(1, B_WIDTH)

        mod_lat = mod_all[l, :b][:, None, :]
        mod_ctx = jnp.broadcast_to(mod_all[l, b][None, None, :], (b, 1, 3 * d))

        ctx_proj = _inproj_call(ctx, mod_ctx, cos=rope_ctx[0], sa=rope_ctx[1], sb=rope_ctx[2], tm=tm_ctx,
                                sub=tm_ctx, tk=tm_ctx, kv_only=last, **shared)
        if last:
            ck, cvt = ctx_proj
        else:
            cq2, ck, cvt, cf, coc, cga, cgb = ctx_proj
        q2, k, vt, f, oc, ga, gb = _inproj_call(
            x, mod_lat, cos=rope_lat[0], sa=rope_lat[1], sb=rope_lat[2], tm=min(n, 2 * tm_lat), sub=tm_lat,
            tk=tm_lat, **shared)

        oa = _attention(lamv, q2, ga, gsub, [(ck, cvt), (k, vt)], g_q[l], g_k[l], tq=tm_lat,
                        nq=min(4, n // tm_lat), lam_init=lam_init)
        ob = _fourier(f, wf_bd, bf_row)
        x_new = _merge_call(oa, ob, gb, oc, x, mod_lat, shared["gn"], *merge_w, tm=min(n, 2 * tm_lat), sub=tm_lat,
                            layer=l)

        if not last:
            coa = _attention(lamv, cq2, cga, gsub, [(ck, cvt)], g_q[l], g_k[l], tq=tm_ctx, nq=1,
                             lam_init=lam_init)
            cob = _fourier(cf, wf_bd, bf_row)
            ctx = _merge_call(coa, cob, cgb, coc, ctx, mod_ctx, shared["gn"], *merge_w, tm=tm_ctx, sub=tm_ctx,
                              layer=l)
        x = x_new
    return x
```

```python
import functools
import math

import numpy as np
import jax
import jax.numpy as jnp
from jax import lax
from jax.experimental import pallas as pl
from jax.experimental.pallas import tpu as pltpu

A_HEADS = 4
A_QK_DIM = 64
A_V_DIM = 2 * A_QK_DIM
A_WIDTH = A_HEADS * A_V_DIM
B_GROUPS = 4
B_GROUP_DIM = 64
B_WIDTH = B_GROUPS * B_GROUP_DIM
C_GROUPS = 4
C_GROUP_DIM = 64
C_WIDTH = C_GROUPS * C_GROUP_DIM
CHUNK = 128
GRID_W = 64
ROPE_BASE = 10000.0
AXIS_DIM = A_QK_DIM // 2
EPS = 1e-6

OFF_Q = 0
OFF_K = OFF_Q + A_WIDTH
OFF_V = OFF_K + A_WIDTH
OFF_F = OFF_V + A_WIDTH
OFF_U = OFF_F + B_WIDTH
OFF_VC = OFF_U + C_WIDTH
OFF_GATE = OFF_VC + C_WIDTH
GATE_WIDTH = A_WIDTH + B_WIDTH + C_WIDTH
OFF_MERGE = OFF_GATE + GATE_WIDTH

V7X_LANES = 128
V7X_SUBLANES = 8
V7X_BF16_SUBLANES = 16
V7X_MXU_DIM = 256
V7X_VMEM_LIMIT_BYTES = 56 * 1024 * 1024

LOG2E = math.log2(math.e)
Q_SCALE = (A_QK_DIM ** -0.5) * LOG2E

_F32 = jnp.float32
_BF16 = jnp.bfloat16
_NT = (((1,), (1,)), ((), ()))


def _sigmoid(x):
    return 0.5 * jnp.tanh(0.5 * x) + 0.5


def _params(n_axes):
    return pltpu.CompilerParams(dimension_semantics=("arbitrary",) * n_axes,
                                vmem_limit_bytes=V7X_VMEM_LIMIT_BYTES)


def _mod_kernel(c_ref, w_ref, b_ref, o_ref):
    cc = c_ref[...]
    s = cc * _sigmoid(cc)
    o_ref[0] = jnp.dot(s, w_ref[0], preferred_element_type=_F32) + b_ref[0]


def _mod_call(rows, w_ada, b_ada):
    depth, d, d3 = w_ada.shape
    nb = d3 // d
    return pl.pallas_call(
        _mod_kernel,
        grid=(depth, nb),
        in_specs=[pl.BlockSpec((rows.shape[0], d), lambda l, j: (0, 0)),
                  pl.BlockSpec((1, d, d), lambda l, j: (l, 0, j)),
                  pl.BlockSpec((1, 1, d), lambda l, j: (l, 0, j))],
        out_specs=pl.BlockSpec((1, rows.shape[0], d), lambda l, j: (l, 0, j)),
        out_shape=jax.ShapeDtypeStruct((depth, rows.shape[0], d3), _F32),
        compiler_params=_params(2),
        name="adaln_mod",
    )(rows, w_ada, b_ada.reshape(depth, 1, d3))


def _modulated_norm(x, mod, gn, d):
    ms = jnp.mean(x * x, axis=-1, keepdims=True)
    gain = gn * (1.0 + mod[:, d:2 * d])
    return (x * lax.rsqrt(ms + EPS) * gain + mod[:, :d]).astype(_BF16)


def _inproj_kernel(x_ref, mod_ref, gn_ref, w_ref, gq_ref, gk_ref, cos_ref, sa_ref, sb_ref,
                   bmat_ref, lng_ref, lnb_ref, ws_ref, bsm_ref, *out_refs, sub, tk, d, kv_only):
    if kv_only:
        k_ref, vt_ref = out_refs
    else:
        q2_ref, k_ref, vt_ref, f_ref, oc_ref, ga_ref, gb_ref = out_refs
    pair = 2 * A_V_DIM
    bmat = bmat_ref[...]
    lo = (lax.broadcasted_iota(jnp.int32, (sub, pair), 1) % A_V_DIM) < A_QK_DIM
    lo_c = lax.broadcasted_iota(jnp.int32, (CHUNK, V7X_LANES), 1) < C_GROUP_DIM
    bsm = bsm_ref[...]
    twice = lambda t: jnp.concatenate([t, t], axis=1)
    gq = twice(gq_ref[...]) * Q_SCALE
    gk = twice(gk_ref[...])

    for si in range(x_ref.shape[1] // sub):
        rows = slice(si * sub, (si + 1) * sub)
        hb = _modulated_norm(x_ref[0, rows, :], mod_ref[0], gn_ref[...], d)

        def proj(off, width, hb=hb):
            return jnp.dot(hb, w_ref[:, off:off + width], preferred_element_type=_F32)

        cos = twice(cos_ref[rows, :])
        sa = twice(sa_ref[rows, :])
        sb = twice(sb_ref[rows, :])

        def qk_heads(p_2h, g, cos=cos, sa=sa, sb=sb):
            msq = jnp.dot((p_2h * p_2h).astype(_BF16), bmat, preferred_element_type=_F32)
            t = p_2h * lax.rsqrt(msq + EPS) * g
            return (t * cos + pltpu.roll(t, pair - AXIS_DIM // 2, 1) * sa
                    + pltpu.roll(t, AXIS_DIM // 2, 1) * sb)

        if not kv_only:
            pq = proj(OFF_Q, A_WIDTH)
            for hp in range(A_HEADS // 2):
                t = qk_heads(pq[:, hp * pair:(hp + 1) * pair], gq)
                t_lo = jnp.where(lo, t, 0.0).astype(_BF16)
                t_hi = jnp.where(lo, 0.0, t).astype(_BF16)
                for h2 in range(2):
                    q2_ref[0, 2 * hp + h2, 0, rows, :] = t_lo[:, h2 * A_V_DIM:(h2 + 1) * A_V_DIM]
                    q2_ref[0, 2 * hp + h2, 1, rows, :] = t_hi[:, h2 * A_V_DIM:(h2 + 1) * A_V_DIM]

        pk = proj(OFF_K, A_WIDTH)
        for hp in range(A_HEADS // 2):
            t = qk_heads(pk[:, hp * pair:(hp + 1) * pair], gk).astype(_BF16)
            for h2 in range(2):
                k_ref[0, 2 * hp + h2, rows, :] = t[:, h2 * A_V_DIM:(h2 + 1) * A_V_DIM]

        pv = proj(OFF_V, A_WIDTH)
        for hh in range(A_HEADS):
            r0 = si * sub
            vt_ref[0, hh, r0 // tk, :, r0 % tk:r0 % tk + sub] = (
                pv[:, hh * A_V_DIM:(hh + 1) * A_V_DIM].T.astype(_BF16))
        if kv_only:
            continue

        f_ref[0, rows, :] = proj(OFF_F, B_WIDTH)

        u = proj(OFF_U, C_WIDTH)
        vc = proj(OFF_VC, C_WIDTH)
        mu = jnp.mean(vc, axis=-1, keepdims=True)
        dv = vc - mu
        var = jnp.mean(dv * dv, axis=-1, keepdims=True)
        vn = (dv * lax.rsqrt(var + EPS) * lng_ref[...] + lnb_ref[...]).astype(_BF16)

        pg = proj(OFF_GATE, GATE_WIDTH)
        gate = pg * _sigmoid(pg)
        ga_ref[0, rows, :] = gate[:, :A_WIDTH].astype(_BF16)
        gb_ref[0, rows, :] = gate[:, A_WIDTH:A_WIDTH + B_WIDTH].astype(_BF16)
        gate_c = gate[:, A_WIDTH + B_WIDTH:]

        for t in range(sub // CHUNK):
            crows = slice(t * CHUNK, (t + 1) * CHUNK)
            vl = vn[crows, :V7X_LANES]
            vr = vn[crows, V7X_LANES:]
            s_l = jnp.where(lo_c, jnp.dot(ws_ref[0], vl, preferred_element_type=_F32),
                            jnp.dot(ws_ref[1], vl, preferred_element_type=_F32))
            s_r = jnp.where(lo_c, jnp.dot(ws_ref[2], vr, preferred_element_type=_F32),
                            jnp.dot(ws_ref[3], vr, preferred_element_type=_F32))
            s = jnp.concatenate([s_l, s_r], axis=1) + bsm
            oc_ref[0, si * sub + t * CHUNK:si * sub + (t + 1) * CHUNK, :] = (
                u[crows] * s * gate_c[crows]).astype(_BF16)


def _inproj_call(x, mod, gn, w_bf, gq, gk, cos, sa, sb, bmat, lng, lnb, ws_bf, bsm, *, tm, sub, tk, layer,
                 kv_only=False):
    b, n, d = x.shape
    nt = n // tm
    full = lambda *shape: pl.BlockSpec(shape, lambda bi, i: (0,) * len(shape))
    of_layer = lambda *shape: pl.BlockSpec((None,) + shape, lambda bi, i: (layer,) + (0,) * len(shape))
    in_specs = [
        pl.BlockSpec((1, tm, d), lambda bi, i: (bi, i, 0)),
        pl.BlockSpec((1, 1, 3 * d), lambda bi, i: (bi, 0, 0)),
        full(1, d),
        of_layer(d, w_bf.shape[-1]),
        full(1, V7X_LANES),
        full(1, V7X_LANES),
        pl.BlockSpec((tm, V7X_LANES), lambda bi, i: (i, 0)),
        pl.BlockSpec((tm, V7X_LANES), lambda bi, i: (i, 0)),
        pl.BlockSpec((tm, V7X_LANES), lambda bi, i: (i, 0)),
        full(2 * A_V_DIM, 2 * A_V_DIM),
        full(1, C_WIDTH),
        full(1, C_WIDTH),
        of_layer(C_GROUPS, CHUNK, CHUNK),
        full(CHUNK, C_WIDTH),
    ]
    out_shape = [
        jax.ShapeDtypeStruct((b, A_HEADS, 2, n, A_V_DIM), _BF16),
        jax.ShapeDtypeStruct((b, A_HEADS, n, A_V_DIM), _BF16),
        jax.ShapeDtypeStruct((b, A_HEADS, n // tk, A_V_DIM, tk), _BF16),
        jax.ShapeDtypeStruct((b, n, B_WIDTH), _F32),
        jax.ShapeDtypeStruct((b, n, C_WIDTH), _BF16),
        jax.ShapeDtypeStruct((b, n, A_WIDTH), _BF16),
        jax.ShapeDtypeStruct((b, n, B_WIDTH), _BF16),
    ]
    out_specs = [
        pl.BlockSpec((1, A_HEADS, 2, tm, A_V_DIM), lambda bi, i: (bi, 0, 0, i, 0)),
        pl.BlockSpec((1, A_HEADS, tm, A_V_DIM), lambda bi, i: (bi, 0, i, 0)),
        pl.BlockSpec((1, A_HEADS, tm // tk, A_V_DIM, tk), lambda bi, i: (bi, 0, i, 0, 0)),
        pl.BlockSpec((1, tm, B_WIDTH), lambda bi, i: (bi, i, 0)),
        pl.BlockSpec((1, tm, C_WIDTH), lambda bi, i: (bi, i, 0)),
        pl.BlockSpec((1, tm, A_WIDTH), lambda bi, i: (bi, i, 0)),
        pl.BlockSpec((1, tm, B_WIDTH), lambda bi, i: (bi, i, 0)),
    ]
    if kv_only:
        out_shape, out_specs = out_shape[1:3], out_specs[1:3]
    return pl.pallas_call(
        functools.partial(_inproj_kernel, sub=sub, tk=tk, d=d, kv_only=kv_only),
        grid=(b, nt),
        in_specs=in_specs,
        out_specs=out_specs,
        out_shape=out_shape,
        compiler_params=_params(2),
        name="inproj",
    )(x, mod, gn, w_bf, gq, gk, cos, sa, sb, bmat, lng, lnb, ws_bf, bsm)


def _attn_kernel(*refs, tq, nq, n_src, tiles, lam_init, bounded):
    lam_ref, q2_ref, ga_ref, gsub_ref = refs[:4]
    src_refs = refs[4:4 + 2 * n_src]
    out_ref = refs[4 + 2 * n_src]
    p_sc, m_sc, l_sc, acc_sc = refs[5 + 2 * n_src:]

    def queries(u):
        rows = slice(u * tq, (u + 1) * tq)
        return jnp.concatenate([q2_ref[0, 0, 0, rows, :], q2_ref[0, 0, 1, rows, :]], axis=0)

    l_sc[...] = jnp.zeros(l_sc.shape, _F32)
    acc_sc[...] = jnp.zeros(acc_sc.shape, _F32)
    lv = lam_ref[...]
    lam = (jnp.exp(jnp.sum(lv[0:1] * lv[1:2], axis=-1, keepdims=True))
           - jnp.exp(jnp.sum(lv[2:3] * lv[3:4], axis=-1, keepdims=True)) + lam_init)

    def finalize(u):
        inv = 1.0 / l_sc[u]
        acc = acc_sc[u]
        o = acc[:, :tq] * inv[:, :tq] - lam * (acc[:, tq:] * inv[:, tq:])
        ms = jnp.mean(o * o, axis=0, keepdims=True)
        on = o * lax.rsqrt(ms + EPS)
        rows = slice(u * tq, (u + 1) * tq)
        out = on.T * (gsub_ref[...] * (1.0 - lam_init)) * ga_ref[0, rows, :].astype(_F32)
        out_ref[0, rows, :] = out.astype(_BF16)

    def tile_len(si):
        return src_refs[2 * si + 1].shape[-1]

    def k_tile(si, t):
        tk = tile_len(si)
        if isinstance(t, int):
            return src_refs[2 * si][0, 0, t * tk:(t + 1) * tk, :]
        return src_refs[2 * si][0, 0, pl.ds(pl.multiple_of(t * tk, tk), tk), :]

    def vt_tile(si, t):
        return src_refs[2 * si + 1][0, 0, t]

    if bounded:
        def exponentials(slot, u, k_t):
            p = jnp.exp2(lax.dot_general(k_t, queries(u), _NT, preferred_element_type=_F32))
            l_sc[u] += jnp.sum(p, axis=0, keepdims=True)
            p_sc[slot, :k_t.shape[0], :] = p.astype(_BF16)

        def weighted_values(slot, u, vt_t):
            acc_sc[u] += jnp.dot(vt_t, p_sc[slot, :vt_t.shape[1], :], preferred_element_type=_F32)

        stages = [(u, si, t) for u in range(nq) for si in range(n_src) for t in range(tiles[si])]
        exponentials(0, stages[0][0], k_tile(*stages[0][1:]))
        for j, (u, si, t) in enumerate(stages):
            if j + 1 < len(stages):
                exponentials((j + 1) % 2, stages[j + 1][0], k_tile(*stages[j + 1][1:]))
            weighted_values(j % 2, u, vt_tile(si, t))
            if j + 1 == len(stages) or stages[j + 1][0] != u:
                finalize(u)
    else:
        for u in range(nq):
            m_sc[...] = jnp.full(m_sc.shape, -jnp.inf, _F32)

            def step(k_t, vt_t, u=u):
                s = lax.dot_general(k_t, queries(u), _NT, preferred_element_type=_F32)
                m_prev = m_sc[...]
                m_new = jnp.maximum(m_prev, jnp.max(s, axis=0, keepdims=True))
                alpha = jnp.exp2(m_prev - m_new)
                p = jnp.exp2(s - m_new)
                l_sc[u] = alpha * l_sc[u] + jnp.sum(p, axis=0, keepdims=True)
                acc_sc[u] = alpha * acc_sc[u] + jnp.dot(vt_t, p.astype(_BF16), preferred_element_type=_F32)
                m_sc[...] = m_new

            for si in range(n_src):
                if tiles[si] == 1:
                    step(k_tile(si, 0), vt_tile(si, 0))
                else:
                    def body(t, carry, si=si, step=step):
                        step(k_tile(si, t), vt_tile(si, t))
                        return carry
                    lax.fori_loop(0, tiles[si], body, 0)
            finalize(u)


def _attn_call(lamv, q2, ga, gsub, sources, *, tq, nq, lam_init, bounded):
    b, _, _, n, _ = q2.shape
    rows = nq * tq
    in_specs = [
        pl.BlockSpec(lamv.shape, lambda bi, hi, i: (0, 0)),
        pl.BlockSpec((1, 1, 2, rows, A_V_DIM), lambda bi, hi, i: (bi, hi, 0, i, 0)),
        pl.BlockSpec((1, rows, A_V_DIM), lambda bi, hi, i: (bi, i, hi)),
        pl.BlockSpec((1, A_V_DIM), lambda bi, hi, i: (0, 0)),
    ]
    args = [lamv, q2, ga, gsub]
    tiles = []
    for k_s, vt_s in sources:
        ls = k_s.shape[2]
        nt, _, tk = vt_s.shape[2:]
        tiles.append(nt)
        in_specs.append(pl.BlockSpec((1, 1, ls, A_V_DIM), lambda bi, hi, i: (bi, hi, 0, 0)))
        in_specs.append(pl.BlockSpec((1, 1, nt, A_V_DIM, tk), lambda bi, hi, i: (bi, hi, 0, 0, 0)))
        args += [k_s, vt_s]
    tk_max = max(vt_s.shape[-1] for _, vt_s in sources)
    return pl.pallas_call(
        functools.partial(_attn_kernel, tq=tq, nq=nq, n_src=len(sources), tiles=tuple(tiles),
                          lam_init=lam_init, bounded=bounded),
        grid=(b, A_HEADS, n // rows),
        in_specs=in_specs,
        out_specs=pl.BlockSpec((1, rows, A_V_DIM), lambda bi, hi, i: (bi, i, hi)),
        out_shape=jax.ShapeDtypeStruct((b, n, A_WIDTH), _BF16),
        scratch_shapes=[pltpu.VMEM((2, tk_max, 2 * tq), _BF16),
                        pltpu.VMEM((1, 2 * tq), _F32),
                        pltpu.VMEM((nq, 1, 2 * tq), _F32),
                        pltpu.VMEM((nq, A_V_DIM, 2 * tq), _F32)],
        compiler_params=_params(3),
        name="diff_attn" if bounded else "diff_attn_general",
    )(*args)


SAFE_EXP2_RANGE = 60.0


def _attention(lamv, q2, ga, gsub, sources, gq, gk, *, tq, nq, lam_init):
    bound = A_QK_DIM * Q_SCALE * jnp.max(jnp.abs(gq)) * jnp.max(jnp.abs(gk))
    call = lambda bounded: functools.partial(_attn_call, tq=tq, nq=nq, lam_init=lam_init, bounded=bounded)
    return lax.cond(bound <= SAFE_EXP2_RANGE, call(True), call(False), lamv, q2, ga, gsub, sources)


def _dft1_kernel(f1_ref, x_ref, o_ref, *, rows_per_step):
    n1 = x_ref.shape[1]
    x = x_ref[0].reshape(n1, rows_per_step * B_WIDTH).astype(_BF16)
    z = jnp.dot(f1_ref[...], x, preferred_element_type=_F32)
    o_ref[0] = z.reshape(2 * n1, rows_per_step, B_WIDTH).astype(_BF16)


def _dft1_call(f1, x4d, *, rows_per_step):
    b, n1, n2, _ = x4d.shape
    return pl.pallas_call(
        functools.partial(_dft1_kernel, rows_per_step=rows_per_step),
        grid=(b, n2 // rows_per_step),
        in_specs=[pl.BlockSpec((2 * n1, n1), lambda bi, j: (0, 0)),
                  pl.BlockSpec((1, n1, rows_per_step, B_WIDTH), lambda bi, j: (bi, 0, j, 0))],
        out_specs=pl.BlockSpec((1, 2 * n1, rows_per_step, B_WIDTH), lambda bi, j: (bi, 0, j, 0)),
        out_shape=jax.ShapeDtypeStruct((b, 2 * n1, n2, B_WIDTH), _BF16),
        compiler_params=_params(2),
        name="dft_stage1",
    )(f1, x4d)


def _dft2_kernel(*refs, g, n2, two_stage, norm):
    if two_stage:
        ar_ref, ai_ref, twc_ref, tws_ref, w2c_ref, mix_ref, wf_ref, bf_ref, out_ref = refs
        ar = ar_ref[0, 0].reshape(g * n2, B_WIDTH).astype(_F32)
        ai = ai_ref[0, 0].reshape(g * n2, B_WIDTH).astype(_F32)
        tc = twc_ref[...].reshape(g * n2, V7X_LANES)
        ts = tws_ref[...].reshape(g * n2, V7X_LANES)
        c2 = jnp.concatenate([tc, tc], axis=1)
        s2 = jnp.concatenate([ts, ts], axis=1)
        a = jnp.concatenate([ar * c2 - ai * s2, ar * s2 + ai * c2], axis=1).astype(_BF16)
        uv = jnp.dot(a, mix_ref[...], preferred_element_type=_F32)
    else:
        ar_ref, w2c_ref, mix_ref, wf_ref, bf_ref, out_ref = refs
        uv = jnp.dot(ar_ref[0, 0, 0].astype(_BF16), mix_ref[:B_WIDTH, :], preferred_element_type=_F32)
    u = uv[:, :B_WIDTH].astype(_BF16)
    v = uv[:, B_WIDTH:].astype(_BF16)
    w2c = w2c_ref[...]
    frs = []
    for j in range(g):
        rows = slice(j * n2, (j + 1) * n2)
        frs.append(jnp.dot(w2c, jnp.concatenate([u[rows], v[rows]], axis=0), preferred_element_type=_F32))
    fr = jnp.concatenate(frs, axis=0) * norm
    y = jnp.dot(fr.astype(_BF16), wf_ref[...], preferred_element_type=_F32) + bf_ref[...]
    if two_stage:
        out_ref[0] = jnp.swapaxes(y.reshape(g, n2, B_WIDTH), 0, 1).astype(_BF16)
    else:
        out_ref[0] = y.astype(_BF16)


def _dft_tables(n1, n2):
    n = n1 * n2
    k1 = np.arange(n1)
    a1 = 2.0 * np.pi * ((k1[:, None] * k1[None, :]) % n1) / n1
    f1 = np.concatenate([np.cos(a1), np.sin(a1)], axis=0)
    k2 = np.arange(n2)
    a2 = 2.0 * np.pi * ((k2[:, None] * k2[None, :]) % n2) / n2
    w2c = np.concatenate([np.cos(a2), -np.sin(a2)], axis=1)
    gch = np.arange(B_GROUP_DIM)
    ag = 2.0 * np.pi * ((gch[:, None] * gch[None, :]) % B_GROUP_DIM) / B_GROUP_DIM
    eye = np.eye(B_GROUPS)
    cg = np.kron(eye, np.cos(ag))
    sg = np.kron(eye, np.sin(ag))
    mix = np.block([[cg, sg], [-sg, cg]])
    to_bf16 = lambda t: jnp.asarray(t, _F32).astype(_BF16)
    return to_bf16(f1), to_bf16(w2c), to_bf16(mix), 1.0 / math.sqrt(n * B_GROUP_DIM)


def _twiddles(n1, n2):
    k1 = jnp.arange(n1, dtype=jnp.int32)[:, None]
    m2 = jnp.arange(n2, dtype=jnp.int32)[None, :]
    ang = (k1 * m2).astype(_F32) * (2.0 * math.pi / (n1 * n2))
    shape = (n1, n2, V7X_LANES)
    return (jnp.broadcast_to(jnp.cos(ang)[:, :, None], shape), jnp.broadcast_to(jnp.sin(ang)[:, :, None], shape))


def _fourier(f, wf_bd, bf_row):
    b, n, _ = f.shape
    two_stage = n > 2 * V7X_LANES
    n2 = V7X_LANES if two_stage else n
    n1 = n // n2
    g = min(n1, V7X_BF16_SUBLANES)
    f1, w2c, mix, norm = _dft_tables(n1, n2)
    full = lambda *shape: pl.BlockSpec(shape, lambda bi, j: (0,) * len(shape))
    consts = [w2c, mix, wf_bd, bf_row]
    const_specs = [full(*w2c.shape), full(*mix.shape), full(B_WIDTH, B_WIDTH), full(1, B_WIDTH)]
    if two_stage:
        a = _dft1_call(f1, f.reshape(b, n1, n2, B_WIDTH), rows_per_step=2 * V7X_BF16_SUBLANES)
        a5 = a.reshape(b, 2, n1, n2, B_WIDTH)
        out_shape = jax.ShapeDtypeStruct((b, n2, n1, B_WIDTH), _BF16)
        out_spec = pl.BlockSpec((1, n2, g, B_WIDTH), lambda bi, j: (bi, 0, j, 0))
        twc, tws = _twiddles(n1, n2)
        args = [a5, a5, twc, tws] + consts
        in_specs = [pl.BlockSpec((1, 1, g, n2, B_WIDTH), lambda bi, j: (bi, 0, j, 0, 0)),
                    pl.BlockSpec((1, 1, g, n2, B_WIDTH), lambda bi, j: (bi, 1, j, 0, 0)),
                    pl.BlockSpec((g, n2, V7X_LANES), lambda bi, j: (j, 0, 0)),
                    pl.BlockSpec((g, n2, V7X_LANES), lambda bi, j: (j, 0, 0))] + const_specs
    else:
        args = [f.reshape(b, 1, 1, n2, B_WIDTH)] + consts
        in_specs = [pl.BlockSpec((1, 1, 1, n2, B_WIDTH), lambda bi, j: (bi, 0, 0, 0, 0))] + const_specs
        out_shape = jax.ShapeDtypeStruct((b, n2, B_WIDTH), _BF16)
        out_spec = pl.BlockSpec((1, n2, B_WIDTH), lambda bi, j: (bi, 0, 0))
    out = pl.pallas_call(
        functools.partial(_dft2_kernel, g=g, n2=n2, two_stage=two_stage, norm=norm),
        grid=(b, n1 // g),
        in_specs=in_specs,
        out_specs=out_spec,
        out_shape=out_shape,
        compiler_params=_params(2),
        name="dft_stage2",
    )(*args)
    return out.reshape(b, n, B_WIDTH)


def _merge_kernel(oa_ref, ob_ref, gb_ref, oc_ref, x_ref, mod_ref, gn_ref, wm_ref, wa_ref, wb_ref, wc_ref, wo_ref,
                  o_ref, *, d, sub):
    mod = mod_ref[0]
    for r0 in range(0, x_ref.shape[1], sub):
        rows = slice(r0, r0 + sub)
        x = x_ref[0, rows, :]
        hb = _modulated_norm(x, mod, gn_ref[...], d)
        branches = (jnp.dot(oa_ref[0, rows, :], wa_ref[...], preferred_element_type=_F32),
                    jnp.dot(ob_ref[0, rows, :] * gb_ref[0, rows, :], wb_ref[...], preferred_element_type=_F32),
                    jnp.dot(oc_ref[0, rows, :], wc_ref[...], preferred_element_type=_F32))
        m = _sigmoid(jnp.dot(hb, wm_ref[:, OFF_MERGE:], preferred_element_type=_F32))
        y = None
        for j, yj in enumerate(branches):
            mj = m[:, j * d:(j + 1) * d]
            y = mj * yj if y is None else y + mj * yj
        out = jnp.dot(y.astype(_BF16), wo_ref[...], preferred_element_type=_F32)
        o_ref[0, rows, :] = x + mod[:, 2 * d:] * out


def _merge_call(oa, ob, gb, oc, x, mod, gn, w_in_bf, wa, wb, wc, wo, *, tm, sub, layer):
    b, n, d = x.shape
    row = lambda w: pl.BlockSpec((1, tm, w), lambda bi, i: (bi, i, 0))
    of_layer = lambda *shape: pl.BlockSpec((None,) + shape, lambda bi, i: (layer,) + (0,) * len(shape))
    return pl.pallas_call(
        functools.partial(_merge_kernel, d=d, sub=sub),
        grid=(b, n // tm),
        in_specs=[row(A_WIDTH), row(B_WIDTH), row(B_WIDTH), row(C_WIDTH), row(d),
                  pl.BlockSpec((1, 1, 3 * d), lambda bi, i: (bi, 0, 0)),
                  pl.BlockSpec((1, d), lambda bi, i: (0, 0)),
                  of_layer(d, w_in_bf.shape[-1]), of_layer(A_WIDTH, d), of_layer(B_WIDTH, d),
                  of_layer(C_WIDTH, d), of_layer(d, d)],
        out_specs=row(d),
        out_shape=jax.ShapeDtypeStruct((b, n, d), _F32),
        compiler_params=_params(2),
        name="merge",
    )(oa, ob, gb, oc, x, mod, gn, w_in_bf, wa, wb, wc, wo)


def _rope_tables(n):
    n_rows = n // GRID_W
    lane = np.arange(V7X_LANES) % A_QK_DIM
    row_lane = lane < AXIS_DIM
    first_half = (lane % AXIS_DIM) < AXIS_DIM // 2
    freqs = ROPE_BASE ** (-jnp.arange(0, AXIS_DIM, 2, dtype=_F32) / AXIS_DIM)
    f_lane = freqs[lane % (AXIS_DIM // 2)]
    ang_r = jnp.arange(n_rows, dtype=_F32)[:, None] * f_lane
    ang_c = jnp.arange(GRID_W, dtype=_F32)[:, None] * f_lane

    def on_grid(fn, lanes):
        keep_r = row_lane & lanes
        keep_c = ~row_lane & lanes
        r = jnp.where(keep_r, fn(ang_r), 0.0)
        c = jnp.where(keep_c, fn(ang_c), 0.0)
        r, c = lax.optimization_barrier((r, c))
        return (r[:, None, :] + c[None, :, :]).reshape(n, V7X_LANES)

    every = np.ones(V7X_LANES, bool)
    cos = on_grid(jnp.cos, every)
    sa = on_grid(lambda a: -jnp.sin(a), first_half)
    sb = on_grid(jnp.sin, ~first_half)
    return cos, sa, sb


def _no_rope_tables(n):
    return (jnp.ones((n, V7X_LANES), _F32), jnp.zeros((n, V7X_LANES), _F32), jnp.zeros((n, V7X_LANES), _F32))


def kernel(x, c, ctx, c_ctx, w_ada, b_ada, g_norm, w_in, g_q, g_k, lam_q1, lam_k1, lam_q2, lam_k2, g_sub,
           w_f, b_f, ln_g, ln_b, w_s, b_s, w_br_a, w_br_b, w_br_c, w_out):
    b, n, d = x.shape
    n_ctx = ctx.shape[1]
    depth = w_in.shape[0]
    tm_lat = min(n, 512)
    tm_ctx = min(n_ctx, 512)

    n_rows = V7X_SUBLANES * pl.cdiv(b + 1, V7X_SUBLANES)
    rows = jnp.concatenate([c, c_ctx[None, :], jnp.zeros((n_rows - b - 1, d), _F32)], axis=0)
    mod_all = _mod_call(rows, w_ada, b_ada)

    rope_lat = _rope_tables(n)
    rope_ctx = _no_rope_tables(n_ctx)
    group_mean = np.kron(np.eye(2 * A_V_DIM // A_QK_DIM), np.full((A_QK_DIM, A_QK_DIM), 1.0 / A_QK_DIM))
    bmat = jnp.asarray(group_mean, _BF16)

    w_in_bf = w_in.astype(_BF16)
    w_s_bf = w_s.astype(_BF16)
    merge_w = (w_in_bf, w_br_a.astype(_BF16), w_br_b.astype(_BF16), w_br_c.astype(_BF16), w_out.astype(_BF16))

    for l in range(depth):
        last = l == depth - 1
        lam_init = 0.8 - 0.6 * math.exp(-0.3 * l)
        lamv = jnp.stack([lam_q1[l], lam_k1[l], lam_q2[l], lam_k2[l]], axis=0)
        shared = dict(
            gn=g_norm[l][None, :], w_bf=w_in_bf,
            gq=jnp.tile(g_q[l], 2)[None, :], gk=jnp.tile(g_k[l], 2)[None, :], bmat=bmat,
            lng=ln_g[l][None, :], lnb=ln_b[l][None, :], ws_bf=w_s_bf,
            bsm=jnp.repeat(b_s[l].T, C_GROUP_DIM, axis=1), layer=l)
        gsub = g_sub[l][None, :]
        wf_bd = jax.scipy.linalg.block_diag(*[w_f[l, gi] for gi in range(B_GROUPS)]).astype(_BF16)
        bf_row = b_f[l].reshape(1, B_WIDTH)

        mod_lat = mod_all[l, :b][:, None, :]
        mod_ctx = jnp.broadcast_to(mod_all[l, b][None, None, :], (b, 1, 3 * d))

        ctx_proj = _inproj_call(ctx, mod_ctx, cos=rope_ctx[0], sa=rope_ctx[1], sb=rope_ctx[2], tm=tm_ctx,
                                sub=tm_ctx, tk=tm_ctx, kv_only=last, **shared)
        if last:
            ck, cvt = ctx_proj
        else:
            cq2, ck, cvt, cf, coc, cga, cgb = ctx_proj
        q2, k, vt, f, oc, ga, gb = _inproj_call(
            x, mod_lat, cos=rope_lat[0], sa=rope_lat[1], sb=rope_lat[2], tm=min(n, 2 * tm_lat), sub=tm_lat,
            tk=tm_lat, **shared)

        oa = _attention(lamv, q2, ga, gsub, [(ck, cvt), (k, vt)], g_q[l], g_k[l], tq=tm_lat,
                        nq=min(4, n // tm_lat), lam_init=lam_init)
        ob = _fourier(f, wf_bd, bf_row)
        x_new = _merge_call(oa, ob, gb, oc, x, mod_lat, shared["gn"], *merge_w, tm=min(n, 2 * tm_lat), sub=tm_lat,
                            layer=l)

        if not last:
            coa = _attention(lamv, cq2, cga, gsub, [(ck, cvt)], g_q[l], g_k[l], tq=tm_ctx, nq=1,
                             lam_init=lam_init)
            cob = _fourier(cf, wf_bd, bf_row)
            ctx = _merge_call(coa, cob, cgb, coc, ctx, mod_ctx, shared["gn"], *merge_w, tm=tm_ctx, sub=tm_ctx,
                              layer=l)
        x = x_new
    return x
```

```python
import functools
import math

import numpy as np
import jax
import jax.numpy as jnp
from jax import lax
from jax.experimental import pallas as pl
from jax.experimental.pallas import tpu as pltpu

A_HEADS = 4
A_QK_DIM = 64
A_V_DIM = 2 * A_QK_DIM
A_WIDTH = A_HEADS * A_V_DIM
B_GROUPS = 4
B_GROUP_DIM = 64
B_WIDTH = B_GROUPS * B_GROUP_DIM
C_GROUPS = 4
C_GROUP_DIM = 64
C_WIDTH = C_GROUPS * C_GROUP_DIM
CHUNK = 128
GRID_W = 64
ROPE_BASE = 10000.0
AXIS_DIM = A_QK_DIM // 2
EPS = 1e-6

OFF_Q = 0
OFF_K = OFF_Q + A_WIDTH
OFF_V = OFF_K + A_WIDTH
OFF_F = OFF_V + A_WIDTH
OFF_U = OFF_F + B_WIDTH
OFF_VC = OFF_U + C_WIDTH
OFF_GATE = OFF_VC + C_WIDTH
GATE_WIDTH = A_WIDTH + B_WIDTH + C_WIDTH
OFF_MERGE = OFF_GATE + GATE_WIDTH

V7X_LANES = 128
V7X_SUBLANES = 8
V7X_BF16_SUBLANES = 16
V7X_MXU_DIM = 256
V7X_VMEM_LIMIT_BYTES = 56 * 1024 * 1024

LOG2E = math.log2(math.e)
Q_SCALE = (A_QK_DIM ** -0.5) * LOG2E

_F32 = jnp.float32
_BF16 = jnp.bfloat16
_NT = (((1,), (1,)), ((), ()))


def _sigmoid(x):
    return 0.5 * jnp.tanh(0.5 * x) + 0.5


def _params(n_axes):
    return pltpu.CompilerParams(dimension_semantics=("arbitrary",) * n_axes,
                                vmem_limit_bytes=V7X_VMEM_LIMIT_BYTES)


def _mod_kernel(c_ref, w_ref, b_ref, o_ref):
    cc = c_ref[...]
    s = cc * _sigmoid(cc)
    o_ref[0] = jnp.dot(s, w_ref[0], preferred_element_type=_F32) + b_ref[0]


def _mod_call(rows, w_ada, b_ada):
    depth, d, d3 = w_ada.shape
    nb = d3 // d
    return pl.pallas_call(
        _mod_kernel,
        grid=(depth, nb),
        in_specs=[pl.BlockSpec((rows.shape[0], d), lambda l, j: (0, 0)),
                  pl.BlockSpec((1, d, d), lambda l, j: (l, 0, j)),
                  pl.BlockSpec((1, 1, d), lambda l, j: (l, 0, j))],
        out_specs=pl.BlockSpec((1, rows.shape[0], d), lambda l, j: (l, 0, j)),
        out_shape=jax.ShapeDtypeStruct((depth, rows.shape[0], d3), _F32),
        compiler_params=_params(2),
        name="adaln_mod",
    )(rows, w_ada, b_ada.reshape(depth, 1, d3))


def _modulated_norm(x, mod, gn, d):
    ms = jnp.mean(x * x, axis=-1, keepdims=True)
    gain = gn * (1.0 + mod[:, d:2 * d])
    return (x * lax.rsqrt(ms + EPS) * gain + mod[:, :d]).astype(_BF16)


def _inproj_kernel(x_ref, mod_ref, gn_ref, w_ref, gq_ref, gk_ref, cos_ref, sa_ref, sb_ref,
                   bmat_ref, lng_ref, lnb_ref, ws_ref, bsm_ref, *out_refs, sub, tk, d, kv_only):
    if kv_only:
        k_ref, vt_ref = out_refs
    else:
        q2_ref, k_ref, vt_ref, f_ref, oc_ref, ga_ref, gb_ref = out_refs
    pair = 2 * A_V_DIM
    bmat = bmat_ref[...]
    lo = (lax.broadcasted_iota(jnp.int32, (sub, pair), 1) % A_V_DIM) < A_QK_DIM
    lo_c = lax.broadcasted_iota(jnp.int32, (CHUNK, V7X_LANES), 1) < C_GROUP_DIM
    bsm = bsm_ref[...]
    twice = lambda t: jnp.concatenate([t, t], axis=1)
    gq = twice(gq_ref[...]) * Q_SCALE
    gk = twice(gk_ref[...])

    for si in range(x_ref.shape[1] // sub):
        rows = slice(si * sub, (si + 1) * sub)
        hb = _modulated_norm(x_ref[0, rows, :], mod_ref[0], gn_ref[...], d)

        def proj(off, width, hb=hb):
            return jnp.dot(hb, w_ref[:, off:off + width], preferred_element_type=_F32)

        cos = twice(cos_ref[rows, :])
        sa = twice(sa_ref[rows, :])
        sb = twice(sb_ref[rows, :])

        def qk_heads(p_2h, g, cos=cos, sa=sa, sb=sb):
            msq = jnp.dot((p_2h * p_2h).astype(_BF16), bmat, preferred_element_type=_F32)
            t = p_2h * lax.rsqrt(msq + EPS) * g
            return (t * cos + pltpu.roll(t, pair - AXIS_DIM // 2, 1) * sa
                    + pltpu.roll(t, AXIS_DIM // 2, 1) * sb)

        if not kv_only:
            pq = proj(OFF_Q, A_WIDTH)
            for hp in range(A_HEADS // 2):
                t = qk_heads(pq[:, hp * pair:(hp + 1) * pair], gq)
                t_lo = jnp.where(lo, t, 0.0).astype(_BF16)
                t_hi = jnp.where(lo, 0.0, t).astype(_BF16)
                for h2 in range(2):
                    q2_ref[0, 2 * hp + h2, 0, rows, :] = t_lo[:, h2 * A_V_DIM:(h2 + 1) * A_V_DIM]
                    q2_ref[0, 2 * hp + h2, 1, rows, :] = t_hi[:, h2 * A_V_DIM:(h2 + 1) * A_V_DIM]

        pk = proj(OFF_K, A_WIDTH)
        for hp in range(A_HEADS // 2):
            t = qk_heads(pk[:, hp * pair:(hp + 1) * pair], gk).astype(_BF16)
            for h2 in range(2):
                k_ref[0, 2 * hp + h2, rows, :] = t[:, h2 * A_V_DIM:(h2 + 1) * A_V_DIM]

        pv = proj(OFF_V, A_WIDTH)
        for hh in range(A_HEADS):
            r0 = si * sub
            vt_ref[0, hh, r0 // tk, :, r0 % tk:r0 % tk + sub] = (
                pv[:, hh * A_V_DIM:(hh + 1) * A_V_DIM].T.astype(_BF16))
        if kv_only:
            continue

        f_ref[0, rows, :] = proj(OFF_F, B_WIDTH)

        u = proj(OFF_U, C_WIDTH)
        vc = proj(OFF_VC, C_WIDTH)
        mu = jnp.mean(vc, axis=-1, keepdims=True)
        dv = vc - mu
        var = jnp.mean(dv * dv, axis=-1, keepdims=True)
        vn = (dv * lax.rsqrt(var + EPS) * lng_ref[...] + lnb_ref[...]).astype(_BF16)

        pg = proj(OFF_GATE, GATE_WIDTH)
        gate = pg * _sigmoid(pg)
        ga_ref[0, rows, :] = gate[:, :A_WIDTH].astype(_BF16)
        gb_ref[0, rows, :] = gate[:, A_WIDTH:A_WIDTH + B_WIDTH].astype(_BF16)
        gate_c = gate[:, A_WIDTH + B_WIDTH:]

        for t in range(sub // CHUNK):
            crows = slice(t * CHUNK, (t + 1) * CHUNK)
            vl = vn[crows, :V7X_LANES]
            vr = vn[crows, V7X_LANES:]
            s_l = jnp.where(lo_c, jnp.dot(ws_ref[0], vl, preferred_element_type=_F32),
                            jnp.dot(ws_ref[1], vl, preferred_element_type=_F32))
            s_r = jnp.where(lo_c, jnp.dot(ws_ref[2], vr, preferred_element_type=_F32),
                            jnp.dot(ws_ref[3], vr, preferred_element_type=_F32))
            s = jnp.concatenate([s_l, s_r], axis=1) + bsm
            oc_ref[0, si * sub + t * CHUNK:si * sub + (t + 1) * CHUNK, :] = (
                u[crows] * s * gate_c[crows]).astype(_BF16)


def _inproj_call(x, mod, gn, w_bf, gq, gk, cos, sa, sb, bmat, lng, lnb, ws_bf, bsm, *, tm, sub, tk, layer,
                 kv_only=False):
    b, n, d = x.shape
    nt = n // tm
    full = lambda *shape: pl.BlockSpec(shape, lambda bi, i: (0,) * len(shape))
    of_layer = lambda *shape: pl.BlockSpec((None,) + shape, lambda bi, i: (layer,) + (0,) * len(shape))
    in_specs = [
        pl.BlockSpec((1, tm, d), lambda bi, i: (bi, i, 0)),
        pl.BlockSpec((1, 1, 3 * d), lambda bi, i: (bi, 0, 0)),
        full(1, d),
        of_layer(d, w_bf.shape[-1]),
        full(1, V7X_LANES),
        full(1, V7X_LANES),
        pl.BlockSpec((tm, V7X_LANES), lambda bi, i: (i, 0)),
        pl.BlockSpec((tm, V7X_LANES), lambda bi, i: (i, 0)),
        pl.BlockSpec((tm, V7X_LANES), lambda bi, i: (i, 0)),
        full(2 * A_V_DIM, 2 * A_V_DIM),
        full(1, C_WIDTH),
        full(1, C_WIDTH),
        of_layer(C_GROUPS, CHUNK, CHUNK),
        full(CHUNK, C_WIDTH),
    ]
    out_shape = [
        jax.ShapeDtypeStruct((b, A_HEADS, 2, n, A_V_DIM), _BF16),
        jax.ShapeDtypeStruct((b, A_HEADS, n, A_V_DIM), _BF16),
        jax.ShapeDtypeStruct((b, A_HEADS, n // tk, A_V_DIM, tk), _BF16),
        jax.ShapeDtypeStruct((b, n, B_WIDTH), _F32),
        jax.ShapeDtypeStruct((b, n, C_WIDTH), _BF16),
        jax.ShapeDtypeStruct((b, n, A_WIDTH), _BF16),
        jax.ShapeDtypeStruct((b, n, B_WIDTH), _BF16),
    ]
    out_specs = [
        pl.BlockSpec((1, A_HEADS, 2, tm, A_V_DIM), lambda bi, i: (bi, 0, 0, i, 0)),
        pl.BlockSpec((1, A_HEADS, tm, A_V_DIM), lambda bi, i: (bi, 0, i, 0)),
        pl.BlockSpec((1, A_HEADS, tm // tk, A_V_DIM, tk), lambda bi, i: (bi, 0, i, 0, 0)),
        pl.BlockSpec((1, tm, B_WIDTH), lambda bi, i: (bi, i, 0)),
        pl.BlockSpec((1, tm, C_WIDTH), lambda bi, i: (bi, i, 0)),
        pl.BlockSpec((1, tm, A_WIDTH), lambda bi, i: (bi, i, 0)),
        pl.BlockSpec((1, tm, B_WIDTH), lambda bi, i: (bi, i, 0)),
    ]
    if kv_only:
        out_shape, out_specs = out_shape[1:3], out_specs[1:3]
    return pl.pallas_call(
        functools.partial(_inproj_kernel, sub=sub, tk=tk, d=d, kv_only=kv_only),
        grid=(b, nt),
        in_specs=in_specs,
        out_specs=out_specs,
        out_shape=out_shape,
        compiler_params=_params(2),
        name="inproj",
    )(x, mod, gn, w_bf, gq, gk, cos, sa, sb, bmat, lng, lnb, ws_bf, bsm)


def _attn_kernel(*refs, tq, nq, n_src, tiles, lam_init, bounded):
    lam_ref, q2_ref, ga_ref, gsub_ref = refs[:4]
    src_refs = refs[4:4 + 2 * n_src]
    out_ref = refs[4 + 2 * n_src]
    p_sc, m_sc, l_sc, acc_sc = refs[5 + 2 * n_src:]

    def queries(u):
        rows = slice(u * tq, (u + 1) * tq)
        return jnp.concatenate([q2_ref[0, 0, 0, rows, :], q2_ref[0, 0, 1, rows, :]], axis=0)

    l_sc[...] = jnp.zeros(l_sc.shape, _F32)
    acc_sc[...] = jnp.zeros(acc_sc.shape, _F32)
    lv = lam_ref[...]
    lam = (jnp.exp(jnp.sum(lv[0:1] * lv[1:2], axis=-1, keepdims=True))
           - jnp.exp(jnp.sum(lv[2:3] * lv[3:4], axis=-1, keepdims=True)) + lam_init)

    def finalize(u):
        inv = 1.0 / l_sc[u]
        acc = acc_sc[u]
        o = acc[:, :tq] * inv[:, :tq] - lam * (acc[:, tq:] * inv[:, tq:])
        ms = jnp.mean(o * o, axis=0, keepdims=True)
        on = o * lax.rsqrt(ms + EPS)
        rows = slice(u * tq, (u + 1) * tq)
        out = on.T * (gsub_ref[...] * (1.0 - lam_init)) * ga_ref[0, rows, :].astype(_F32)
        out_ref[0, rows, :] = out.astype(_BF16)

    def tile_len(si):
        return src_refs[2 * si + 1].shape[-1]

    def k_tile(si, t):
        tk = tile_len(si)
        if isinstance(t, int):
            return src_refs[2 * si][0, 0, t * tk:(t + 1) * tk, :]
        return src_refs[2 * si][0, 0, pl.ds(pl.multiple_of(t * tk, tk), tk), :]

    def vt_tile(si, t):
        return src_refs[2 * si + 1][0, 0, t]

    if bounded:
        def exponentials(slot, u, k_t):
            p = jnp.exp2(lax.dot_general(k_t, queries(u), _NT, preferred_element_type=_F32))
            l_sc[u] += jnp.sum(p, axis=0, keepdims=True)
            p_sc[slot, :k_t.shape[0], :] = p.astype(_BF16)

        def weighted_values(slot, u, vt_t):
            acc_sc[u] += jnp.dot(vt_t, p_sc[slot, :vt_t.shape[1], :], preferred_element_type=_F32)

        stages = [(u, si, t) for u in range(nq) for si in range(n_src) for t in range(tiles[si])]
        exponentials(0, stages[0][0], k_tile(*stages[0][1:]))
        for j, (u, si, t) in enumerate(stages):
            if j + 1 < len(stages):
                exponentials((j + 1) % 2, stages[j + 1][0], k_tile(*stages[j + 1][1:]))
            weighted_values(j % 2, u, vt_tile(si, t))
            if j + 1 == len(stages) or stages[j + 1][0] != u:
                finalize(u)
    else:
        for u in range(nq):
            m_sc[...] = jnp.full(m_sc.shape, -jnp.inf, _F32)

            def step(k_t, vt_t, u=u):
                s = lax.dot_general(k_t, queries(u), _NT, preferred_element_type=_F32)
                m_prev = m_sc[...]
                m_new = jnp.maximum(m_prev, jnp.max(s, axis=0, keepdims=True))
                alpha = jnp.exp2(m_prev - m_new)
                p = jnp.exp2(s - m_new)
                l_sc[u] = alpha * l_sc[u] + jnp.sum(p, axis=0, keepdims=True)
                acc_sc[u] = alpha * acc_sc[u] + jnp.dot(vt_t, p.astype(_BF16), preferred_element_type=_F32)
                m_sc[...] = m_new

            for si in range(n_src):
                if tiles[si] == 1:
                    step(k_tile(si, 0), vt_tile(si, 0))
                else:
                    def body(t, carry, si=si, step=step):
                        step(k_tile(si, t), vt_tile(si, t))
                        return carry
                    lax.fori_loop(0, tiles[si], body, 0)
            finalize(u)


def _attn_call(lamv, q2, ga, gsub, sources, *, tq, nq, lam_init, bounded):
    b, _, _, n, _ = q2.shape
    rows = nq * tq
    in_specs = [
        pl.BlockSpec(lamv.shape, lambda bi, hi, i: (0, 0)),
        pl.BlockSpec((1, 1, 2, rows, A_V_DIM), lambda bi, hi, i: (bi, hi, 0, i, 0)),
        pl.BlockSpec((1, rows, A_V_DIM), lambda bi, hi, i: (bi, i, hi)),
        pl.BlockSpec((1, A_V_DIM), lambda bi, hi, i: (0, 0)),
    ]
    args = [lamv, q2, ga, gsub]
    tiles = []
    for k_s, vt_s in sources:
        ls = k_s.shape[2]
        nt, _, tk = vt_s.shape[2:]
        tiles.append(nt)
        in_specs.append(pl.BlockSpec((1, 1, ls, A_V_DIM), lambda bi, hi, i: (bi, hi, 0, 0)))
        in_specs.append(pl.BlockSpec((1, 1, nt, A_V_DIM, tk), lambda bi, hi, i: (bi, hi, 0, 0, 0)))
        args += [k_s, vt_s]
    tk_max = max(vt_s.shape[-1] for _, vt_s in sources)
    return pl.pallas_call(
        functools.partial(_attn_kernel, tq=tq, nq=nq, n_src=len(sources), tiles=tuple(tiles),
                          lam_init=lam_init, bounded=bounded),
        grid=(b, A_HEADS, n // rows),
        in_specs=in_specs,
        out_specs=pl.BlockSpec((1, rows, A_V_DIM), lambda bi, hi, i: (bi, i, hi)),
        out_shape=jax.ShapeDtypeStruct((b, n, A_WIDTH), _BF16),
        scratch_shapes=[pltpu.VMEM((2, tk_max, 2 * tq), _BF16),
                        pltpu.VMEM((1, 2 * tq), _F32),
                        pltpu.VMEM((nq, 1, 2 * tq), _F32),
                        pltpu.VMEM((nq, A_V_DIM, 2 * tq), _F32)],
        compiler_params=_params(3),
        name="diff_attn" if bounded else "diff_attn_general",
    )(*args)


SAFE_EXP2_RANGE = 60.0


def _attention(lamv, q2, ga, gsub, sources, gq, gk, *, tq, nq, lam_init):
    bound = A_QK_DIM * Q_SCALE * jnp.max(jnp.abs(gq)) * jnp.max(jnp.abs(gk))
    call = lambda bounded: functools.partial(_attn_call, tq=tq, nq=nq, lam_init=lam_init, bounded=bounded)
    return lax.cond(bound <= SAFE_EXP2_RANGE, call(True), call(False), lamv, q2, ga, gsub, sources)


def _dft1_kernel(f1_ref, x_ref, o_ref, *, rows_per_step):
    n1 = x_ref.shape[1]
    x = x_ref[0].reshape(n1, rows_per_step * B_WIDTH).astype(_BF16)
    z = jnp.dot(f1_ref[...], x, preferred_element_type=_F32)
    o_ref[0] = z.reshape(2 * n1, rows_per_step, B_WIDTH).astype(_BF16)


def _dft1_call(f1, x4d, *, rows_per_step):
    b, n1, n2, _ = x4d.shape
    return pl.pallas_call(
        functools.partial(_dft1_kernel, rows_per_step=rows_per_step),
        grid=(b, n2 // rows_per_step),
        in_specs=[pl.BlockSpec((2 * n1, n1), lambda bi, j: (0, 0)),
                  pl.BlockSpec((1, n1, rows_per_step, B_WIDTH), lambda bi, j: (bi, 0, j, 0))],
        out_specs=pl.BlockSpec((1, 2 * n1, rows_per_step, B_WIDTH), lambda bi, j: (bi, 0, j, 0)),
        out_shape=jax.ShapeDtypeStruct((b, 2 * n1, n2, B_WIDTH), _BF16),
        compiler_params=_params(2),
        name="dft_stage1",
    )(f1, x4d)


def _dft2_kernel(*refs, g, n2, two_stage, norm):
    if two_stage:
        ar_ref, ai_ref, twc_ref, tws_ref, w2c_ref, mix_ref, wf_ref, bf_ref, out_ref = refs
        ar = ar_ref[0, 0].reshape(g * n2, B_WIDTH).astype(_F32)
        ai = ai_ref[0, 0].reshape(g * n2, B_WIDTH).astype(_F32)
        tc = twc_ref[...].reshape(g * n2, V7X_LANES)
        ts = tws_ref[...].reshape(g * n2, V7X_LANES)
        c2 = jnp.concatenate([tc, tc], axis=1)
        s2 = jnp.concatenate([ts, ts], axis=1)
        a = jnp.concatenate([ar * c2 - ai * s2, ar * s2 + ai * c2], axis=1).astype(_BF16)
        uv = jnp.dot(a, mix_ref[...], preferred_element_type=_F32)
    else:
        ar_ref, w2c_ref, mix_ref, wf_ref, bf_ref, out_ref = refs
        uv = jnp.dot(ar_ref[0, 0, 0].astype(_BF16), mix_ref[:B_WIDTH, :], preferred_element_type=_F32)
    u = uv[:, :B_WIDTH].astype(_BF16)
    v = uv[:, B_WIDTH:].astype(_BF16)
    w2c = w2c_ref[...]
    frs = []
    for j in range(g):
        rows = slice(j * n2, (j + 1) * n2)
        frs.append(jnp.dot(w2c, jnp.concatenate([u[rows], v[rows]], axis=0), preferred_element_type=_F32))
    fr = jnp.concatenate(frs, axis=0) * norm
    y = jnp.dot(fr.astype(_BF16), wf_ref[...], preferred_element_type=_F32) + bf_ref[...]
    if two_stage:
        out_ref[0] = jnp.swapaxes(y.reshape(g, n2, B_WIDTH), 0, 1).astype(_BF16)
    else:
        out_ref[0] = y.astype(_BF16)


def _dft_tables(n1, n2):
    n = n1 * n2
    k1 = np.arange(n1)
    a1 = 2.0 * np.pi * ((k1[:, None] * k1[None, :]) % n1) / n1
    f1 = np.concatenate([np.cos(a1), np.sin(a1)], axis=0)
    k2 = np.arange(n2)
    a2 = 2.0 * np.pi * ((k2[:, None] * k2[None, :]) % n2) / n2
    w2c = np.concatenate([np.cos(a2), -np.sin(a2)], axis=1)
    gch = np.arange(B_GROUP_DIM)
    ag = 2.0 * np.pi * ((gch[:, None] * gch[None, :]) % B_GROUP_DIM) / B_GROUP_DIM
    eye = np.eye(B_GROUPS)
    cg = np.kron(eye, np.cos(ag))
    sg = np.kron(eye, np.sin(ag))
    mix = np.block([[cg, sg], [-sg, cg]])
    to_bf16 = lambda t: jnp.asarray(t, _F32).astype(_BF16)
    return to_bf16(f1), to_bf16(w2c), to_bf16(mix), 1.0 / math.sqrt(n * B_GROUP_DIM)


def _twiddles(n1, n2):
    k1 = jnp.arange(n1, dtype=jnp.int32)[:, None]
    m2 = jnp.arange(n2, dtype=jnp.int32)[None, :]
    ang = (k1 * m2).astype(_F32) * (2.0 * math.pi / (n1 * n2))
    shape = (n1, n2, V7X_LANES)
    return (jnp.broadcast_to(jnp.cos(ang)[:, :, None], shape), jnp.broadcast_to(jnp.sin(ang)[:, :, None], shape))


def _fourier(f, wf_bd, bf_row):
    b, n, _ = f.shape
    two_stage = n > 2 * V7X_LANES
    n2 = V7X_LANES if two_stage else n
    n1 = n // n2
    g = min(n1, V7X_BF16_SUBLANES)
    f1, w2c, mix, norm = _dft_tables(n1, n2)
    full = lambda *shape: pl.BlockSpec(shape, lambda bi, j: (0,) * len(shape))
    consts = [w2c, mix, wf_bd, bf_row]
    const_specs = [full(*w2c.shape), full(*mix.shape), full(B_WIDTH, B_WIDTH), full(1, B_WIDTH)]
    if two_stage:
        a = _dft1_call(f1, f.reshape(b, n1, n2, B_WIDTH), rows_per_step=2 * V7X_BF16_SUBLANES)
        a5 = a.reshape(b, 2, n1, n2, B_WIDTH)
        out_shape = jax.ShapeDtypeStruct((b, n2, n1, B_WIDTH), _BF16)
        out_spec = pl.BlockSpec((1, n2, g, B_WIDTH), lambda bi, j: (bi, 0, j, 0))
        twc, tws = _twiddles(n1, n2)
        args = [a5, a5, twc, tws] + consts
        in_specs = [pl.BlockSpec((1, 1, g, n2, B_WIDTH), lambda bi, j: (bi, 0, j, 0, 0)),
                    pl.BlockSpec((1, 1, g, n2, B_WIDTH), lambda bi, j: (bi, 1, j, 0, 0)),
                    pl.BlockSpec((g, n2, V7X_LANES), lambda bi, j: (j, 0, 0)),
                    pl.BlockSpec((g, n2, V7X_LANES), lambda bi, j: (j, 0, 0))] + const_specs
    else:
        args = [f.reshape(b, 1, 1, n2, B_WIDTH)] + consts
        in_specs = [pl.BlockSpec((1, 1, 1, n2, B_WIDTH), lambda bi, j: (bi, 0, 0, 0, 0))] + const_specs
        out_shape = jax.ShapeDtypeStruct((b, n2, B_WIDTH), _BF16)
        out_spec = pl.BlockSpec((1, n2, B_WIDTH), lambda bi, j: (bi, 0, 0))
    out = pl.pallas_call(
        functools.partial(_dft2_kernel, g=g, n2=n2, two_stage=two_stage, norm=norm),
        grid=(b, n1 // g),
        in_specs=in_specs,
        out_specs=out_spec,
        out_shape=out_shape,
        compiler_params=_params(2),
        name="dft_stage2",
    )(*args)
    return out.reshape(b, n, B_WIDTH)


def _merge_kernel(oa_ref, ob_ref, gb_ref, oc_ref, x_ref, mod_ref, gn_ref, wm_ref, wa_ref, wb_ref, wc_ref, wo_ref,
                  o_ref, *, d, sub):
    mod = mod_ref[0]
    for r0 in range(0, x_ref.shape[1], sub):
        rows = slice(r0, r0 + sub)
        x = x_ref[0, rows, :]
        hb = _modulated_norm(x, mod, gn_ref[...], d)
        branches = (jnp.dot(oa_ref[0, rows, :], wa_ref[...], preferred_element_type=_F32),
                    jnp.dot(ob_ref[0, rows, :] * gb_ref[0, rows, :], wb_ref[...], preferred_element_type=_F32),
                    jnp.dot(oc_ref[0, rows, :], wc_ref[...], preferred_element_type=_F32))
        m = _sigmoid(jnp.dot(hb, wm_ref[:, OFF_MERGE:], preferred_element_type=_F32))
        y = None
        for j, yj in enumerate(branches):
            mj = m[:, j * d:(j + 1) * d]
            y = mj * yj if y is None else y + mj * yj
        out = jnp.dot(y.astype(_BF16), wo_ref[...], preferred_element_type=_F32)
        o_ref[0, rows, :] = x + mod[:, 2 * d:] * out


def _merge_call(oa, ob, gb, oc, x, mod, gn, w_in_bf, wa, wb, wc, wo, *, tm, sub, layer):
    b, n, d = x.shape
    row = lambda w: pl.BlockSpec((1, tm, w), lambda bi, i: (bi, i, 0))
    of_layer = lambda *shape: pl.BlockSpec((None,) + shape, lambda bi, i: (layer,) + (0,) * len(shape))
    return pl.pallas_call(
        functools.partial(_merge_kernel, d=d, sub=sub),
        grid=(b, n // tm),
        in_specs=[row(A_WIDTH), row(B_WIDTH), row(B_WIDTH), row(C_WIDTH), row(d),
                  pl.BlockSpec((1, 1, 3 * d), lambda bi, i: (bi, 0, 0)),
                  pl.BlockSpec((1, d), lambda bi, i: (0, 0)),
                  of_layer(d, w_in_bf.shape[-1]), of_layer(A_WIDTH, d), of_layer(B_WIDTH, d),
                  of_layer(C_WIDTH, d), of_layer(d, d)],
        out_specs=row(d),
        out_shape=jax.ShapeDtypeStruct((b, n, d), _F32),
        compiler_params=_params(2),
        name="merge",
    )(oa, ob, gb, oc, x, mod, gn, w_in_bf, wa, wb, wc, wo)


def _rope_tables(n):
    n_rows = n // GRID_W
    lane = np.arange(V7X_LANES) % A_QK_DIM
    row_lane = lane < AXIS_DIM
    first_half = (lane % AXIS_DIM) < AXIS_DIM // 2
    freqs = ROPE_BASE ** (-jnp.arange(0, AXIS_DIM, 2, dtype=_F32) / AXIS_DIM)
    f_lane = freqs[lane % (AXIS_DIM // 2)]
    ang_r = jnp.arange(n_rows, dtype=_F32)[:, None] * f_lane
    ang_c = jnp.arange(GRID_W, dtype=_F32)[:, None] * f_lane

    def on_grid(fn, lanes):
        keep_r = row_lane & lanes
        keep_c = ~row_lane & lanes
        r = jnp.where(keep_r, fn(ang_r), 0.0)
        c = jnp.where(keep_c, fn(ang_c), 0.0)
        return (r[:, None, :] + c[None, :, :]).reshape(n, V7X_LANES)

    every = np.ones(V7X_LANES, bool)
    cos = on_grid(jnp.cos, every)
    sa = on_grid(lambda a: -jnp.sin(a), first_half)
    sb = on_grid(jnp.sin, ~first_half)
    return cos, sa, sb


def _no_rope_tables(n):
    return (jnp.ones((n, V7X_LANES), _F32), jnp.zeros((n, V7X_LANES), _F32), jnp.zeros((n, V7X_LANES), _F32))


def kernel(x, c, ctx, c_ctx, w_ada, b_ada, g_norm, w_in, g_q, g_k, lam_q1, lam_k1, lam_q2, lam_k2, g_sub,
           w_f, b_f, ln_g, ln_b, w_s, b_s, w_br_a, w_br_b, w_br_c, w_out):
    b, n, d = x.shape
    n_ctx = ctx.shape[1]
    depth = w_in.shape[0]
    tm_lat = min(n, 512)
    tm_ctx = min(n_ctx, 512)

    n_rows = V7X_SUBLANES * pl.cdiv(b + 1, V7X_SUBLANES)
    rows = jnp.concatenate([c, c_ctx[None, :], jnp.zeros((n_rows - b - 1, d), _F32)], axis=0)
    mod_all = _mod_call(rows, w_ada, b_ada)

    rope_lat = _rope_tables(n)
    rope_ctx = _no_rope_tables(n_ctx)
    group_mean = np.kron(np.eye(2 * A_V_DIM // A_QK_DIM), np.full((A_QK_DIM, A_QK_DIM), 1.0 / A_QK_DIM))
    bmat = jnp.asarray(group_mean, _BF16)

    w_in_bf = w_in.astype(_BF16)
    w_s_bf = w_s.astype(_BF16)
    merge_w = (w_in_bf, w_br_a.astype(_BF16), w_br_b.astype(_BF16), w_br_c.astype(_BF16), w_out.astype(_BF16))

    for l in range(depth):
        last = l == depth - 1
        lam_init = 0.8 - 0.6 * math.exp(-0.3 * l)
        lamv = jnp.stack([lam_q1[l], lam_k1[l], lam_q2[l], lam_k2[l]], axis=0)
        shared = dict(
            gn=g_norm[l][None, :], w_bf=w_in_bf,
            gq=jnp.tile(g_q[l], 2)[None, :], gk=jnp.tile(g_k[l], 2)[None, :], bmat=bmat,
            lng=ln_g[l][None, :], lnb=ln_b[l][None, :], ws_bf=w_s_bf,
            bsm=jnp.repeat(b_s[l].T, C_GROUP_DIM, axis=1), layer=l)
        gsub = g_sub[l][None, :]
        wf_bd = jax.scipy.linalg.block_diag(*[w_f[l, gi] for gi in range(B_GROUPS)]).astype(_BF16)
        bf_row = b_f[l].reshape(1, B_WIDTH)

        mod_lat = mod_all[l, :b][:, None, :]
        mod_ctx = jnp.broadcast_to(mod_all[l, b][None, None, :], (b, 1, 3 * d))

        ctx_proj = _inproj_call(ctx, mod_ctx, cos=rope_ctx[0], sa=rope_ctx[1], sb=rope_ctx[2], tm=tm_ctx,
                                sub=tm_ctx, tk=tm_ctx, kv_only=last, **shared)
        if last:
            ck, cvt = ctx_proj
        else:
            cq2, ck, cvt, cf, coc, cga, cgb = ctx_proj
        q2, k, vt, f, oc, ga, gb = _inproj_call(
            x, mod_lat, cos=rope_lat[0], sa=rope_lat[1], sb=rope_lat[2], tm=min(n, 2 * tm_lat), sub=tm_lat,
            tk=min(n, 2 * tm_lat), **shared)

        oa = _attention(lamv, q2, ga, gsub, [(ck, cvt), (k, vt)], g_q[l], g_k[l], tq=tm_lat,
                        nq=min(4, n // tm_lat), lam_init=lam_init)
        ob = _fourier(f, wf_bd, bf_row)
        x_new = _merge_call(oa, ob, gb, oc, x, mod_lat, shared["gn"], *merge_w, tm=min(n, 2 * tm_lat), sub=tm_lat,
                            layer=l)

        if not last:
            coa = _attention(lamv, cq2, cga, gsub, [(ck, cvt)], g_q[l], g_k[l], tq=tm_ctx, nq=1,
                             lam_init=lam_init)
            cob = _fourier(cf, wf_bd, bf_row)
            ctx = _merge_call(coa, cob, cgb, coc, ctx, mod_ctx, shared["gn"], *merge_w, tm=tm_ctx, sub=tm_ctx,
                              layer=l)
        x = x_new
    return x
```

```python
import functools
import math

import numpy as np
import jax
import jax.numpy as jnp
from jax import lax
from jax.experimental import pallas as pl
from jax.experimental.pallas import tpu as pltpu

A_HEADS = 4
A_QK_DIM = 64
A_V_DIM = 2 * A_QK_DIM
A_WIDTH = A_HEADS * A_V_DIM
B_GROUPS = 4
B_GROUP_DIM = 64
B_WIDTH = B_GROUPS * B_GROUP_DIM
C_GROUPS = 4
C_GROUP_DIM = 64
C_WIDTH = C_GROUPS * C_GROUP_DIM
CHUNK = 128
GRID_W = 64
ROPE_BASE = 10000.0
AXIS_DIM = A_QK_DIM // 2
EPS = 1e-6

OFF_Q = 0
OFF_K = OFF_Q + A_WIDTH
OFF_V = OFF_K + A_WIDTH
OFF_F = OFF_V + A_WIDTH
OFF_U = OFF_F + B_WIDTH
OFF_VC = OFF_U + C_WIDTH
OFF_GATE = OFF_VC + C_WIDTH
GATE_WIDTH = A_WIDTH + B_WIDTH + C_WIDTH
OFF_MERGE = OFF_GATE + GATE_WIDTH

V7X_LANES = 128
V7X_SUBLANES = 8
V7X_BF16_SUBLANES = 16
V7X_MXU_DIM = 256
V7X_VMEM_LIMIT_BYTES = 56 * 1024 * 1024

LOG2E = math.log2(math.e)
Q_SCALE = (A_QK_DIM ** -0.5) * LOG2E

_F32 = jnp.float32
_BF16 = jnp.bfloat16
_NT = (((1,), (1,)), ((), ()))


def _sigmoid(x):
    return 0.5 * jnp.tanh(0.5 * x) + 0.5


def _params(n_axes):
    return pltpu.CompilerParams(dimension_semantics=("arbitrary",) * n_axes,
                                vmem_limit_bytes=V7X_VMEM_LIMIT_BYTES)


def _mod_kernel(c_ref, w_ref, b_ref, o_ref):
    cc = c_ref[...]
    s = cc * _sigmoid(cc)
    o_ref[0] = jnp.dot(s, w_ref[0], preferred_element_type=_F32) + b_ref[0]


def _mod_call(rows, w_ada, b_ada):
    depth, d, d3 = w_ada.shape
    nb = d3 // d
    return pl.pallas_call(
        _mod_kernel,
        grid=(depth, nb),
        in_specs=[pl.BlockSpec((rows.shape[0], d), lambda l, j: (0, 0)),
                  pl.BlockSpec((1, d, d), lambda l, j: (l, 0, j)),
                  pl.BlockSpec((1, 1, d), lambda l, j: (l, 0, j))],
        out_specs=pl.BlockSpec((1, rows.shape[0], d), lambda l, j: (l, 0, j)),
        out_shape=jax.ShapeDtypeStruct((depth, rows.shape[0], d3), _F32),
        compiler_params=_params(2),
        name="adaln_mod",
    )(rows, w_ada, b_ada.reshape(depth, 1, d3))


def _modulated_norm(x, mod, gn, d):
    ms = jnp.mean(x * x, axis=-1, keepdims=True)
    gain = gn * (1.0 + mod[:, d:2 * d])
    return (x * lax.rsqrt(ms + EPS) * gain + mod[:, :d]).astype(_BF16)


def _inproj_kernel(x_ref, mod_ref, gn_ref, w_ref, gq_ref, gk_ref, cos_ref, sa_ref, sb_ref,
                   bmat_ref, lng_ref, lnb_ref, ws_ref, bsm_ref, *out_refs, sub, tk, d, kv_only):
    if kv_only:
        k_ref, vt_ref = out_refs
    else:
        q2_ref, k_ref, vt_ref, f_ref, oc_ref, ga_ref, gb_ref = out_refs
    pair = 2 * A_V_DIM
    bmat = bmat_ref[...]
    lo = (lax.broadcasted_iota(jnp.int32, (sub, pair), 1) % A_V_DIM) < A_QK_DIM
    lo_c = lax.broadcasted_iota(jnp.int32, (CHUNK, V7X_LANES), 1) < C_GROUP_DIM
    bsm = bsm_ref[...]
    twice = lambda t: jnp.concatenate([t, t], axis=1)
    gq = twice(gq_ref[...]) * Q_SCALE
    gk = twice(gk_ref[...])

    for si in range(x_ref.shape[1] // sub):
        rows = slice(si * sub, (si + 1) * sub)
        hb = _modulated_norm(x_ref[0, rows, :], mod_ref[0], gn_ref[...], d)

        def proj(off, width, hb=hb):
            return jnp.dot(hb, w_ref[:, off:off + width], preferred_element_type=_F32)

        cos = twice(cos_ref[rows, :])
        sa = twice(sa_ref[rows, :])
        sb = twice(sb_ref[rows, :])

        def qk_heads(p_2h, g, cos=cos, sa=sa, sb=sb):
            msq = jnp.dot((p_2h * p_2h).astype(_BF16), bmat, preferred_element_type=_F32)
            t = p_2h * lax.rsqrt(msq + EPS) * g
            return (t * cos + pltpu.roll(t, pair - AXIS_DIM // 2, 1) * sa
                    + pltpu.roll(t, AXIS_DIM // 2, 1) * sb)

        if not kv_only:
            pq = proj(OFF_Q, A_WIDTH)
            for hp in range(A_HEADS // 2):
                t = qk_heads(pq[:, hp * pair:(hp + 1) * pair], gq)
                t_lo = jnp.where(lo, t, 0.0).astype(_BF16)
                t_hi = jnp.where(lo, 0.0, t).astype(_BF16)
                for h2 in range(2):
                    q2_ref[0, 2 * hp + h2, 0, rows, :] = t_lo[:, h2 * A_V_DIM:(h2 + 1) * A_V_DIM]
                    q2_ref[0, 2 * hp + h2, 1, rows, :] = t_hi[:, h2 * A_V_DIM:(h2 + 1) * A_V_DIM]

        pk = proj(OFF_K, A_WIDTH)
        for hp in range(A_HEADS // 2):
            t = qk_heads(pk[:, hp * pair:(hp + 1) * pair], gk).astype(_BF16)
            for h2 in range(2):
                k_ref[0, 2 * hp + h2, rows, :] = t[:, h2 * A_V_DIM:(h2 + 1) * A_V_DIM]

        pv = proj(OFF_V, A_WIDTH)
        for hh in range(A_HEADS):
            r0 = si * sub
            vt_ref[0, hh, r0 // tk, :, r0 % tk:r0 % tk + sub] = (
                pv[:, hh * A_V_DIM:(hh + 1) * A_V_DIM].T.astype(_BF16))
        if kv_only:
            continue

        f_ref[0, rows, :] = proj(OFF_F, B_WIDTH)

        u = proj(OFF_U, C_WIDTH)
        vc = proj(OFF_VC, C_WIDTH)
        mu = jnp.mean(vc, axis=-1, keepdims=True)
        dv = vc - mu
        var = jnp.mean(dv * dv, axis=-1, keepdims=True)
        vn = (dv * lax.rsqrt(var + EPS) * lng_ref[...] + lnb_ref[...]).astype(_BF16)

        pg = proj(OFF_GATE, GATE_WIDTH)
        gate = pg * _sigmoid(pg)
        ga_ref[0, rows, :] = gate[:, :A_WIDTH].astype(_BF16)
        gb_ref[0, rows, :] = gate[:, A_WIDTH:A_WIDTH + B_WIDTH].astype(_BF16)
        gate_c = gate[:, A_WIDTH + B_WIDTH:]

        for t in range(sub // CHUNK):
            crows = slice(t * CHUNK, (t + 1) * CHUNK)
            vl = vn[crows, :V7X_LANES]
            vr = vn[crows, V7X_LANES:]
            s_l = jnp.where(lo_c, jnp.dot(ws_ref[0], vl, preferred_element_type=_F32),
                            jnp.dot(ws_ref[1], vl, preferred_element_type=_F32))
            s_r = jnp.where(lo_c, jnp.dot(ws_ref[2], vr, preferred_element_type=_F32),
                            jnp.dot(ws_ref[3], vr, preferred_element_type=_F32))
            s = jnp.concatenate([s_l, s_r], axis=1) + bsm
            oc_ref[0, si * sub + t * CHUNK:si * sub + (t + 1) * CHUNK, :] = (
                u[crows] * s * gate_c[crows]).astype(_BF16)


def _inproj_call(x, mod, gn, w_bf, gq, gk, cos, sa, sb, bmat, lng, lnb, ws_bf, bsm, *, tm, sub, tk, layer,
                 kv_only=False):
    b, n, d = x.shape
    nt = n // tm
    full = lambda *shape: pl.BlockSpec(shape, lambda bi, i: (0,) * len(shape))
    of_layer = lambda *shape: pl.BlockSpec((None,) + shape, lambda bi, i: (layer,) + (0,) * len(shape))
    in_specs = [
        pl.BlockSpec((1, tm, d), lambda bi, i: (bi, i, 0)),
        pl.BlockSpec((1, 1, 3 * d), lambda bi, i: (bi, 0, 0)),
        full(1, d),
        of_layer(d, w_bf.shape[-1]),
        full(1, V7X_LANES),
        full(1, V7X_LANES),
        pl.BlockSpec((tm, V7X_LANES), lambda bi, i: (i, 0)),
        pl.BlockSpec((tm, V7X_LANES), lambda bi, i: (i, 0)),
        pl.BlockSpec((tm, V7X_LANES), lambda bi, i: (i, 0)),
        full(2 * A_V_DIM, 2 * A_V_DIM),
        full(1, C_WIDTH),
        full(1, C_WIDTH),
        of_layer(C_GROUPS, CHUNK, CHUNK),
        full(CHUNK, C_WIDTH),
    ]
    out_shape = [
        jax.ShapeDtypeStruct((b, A_HEADS, 2, n, A_V_DIM), _BF16),
        jax.ShapeDtypeStruct((b, A_HEADS, n, A_V_DIM), _BF16),
        jax.ShapeDtypeStruct((b, A_HEADS, n // tk, A_V_DIM, tk), _BF16),
        jax.ShapeDtypeStruct((b, n, B_WIDTH), _F32),
        jax.ShapeDtypeStruct((b, n, C_WIDTH), _BF16),
        jax.ShapeDtypeStruct((b, n, A_WIDTH), _BF16),
        jax.ShapeDtypeStruct((b, n, B_WIDTH), _BF16),
    ]
    out_specs = [
        pl.BlockSpec((1, A_HEADS, 2, tm, A_V_DIM), lambda bi, i: (bi, 0, 0, i, 0)),
        pl.BlockSpec((1, A_HEADS, tm, A_V_DIM), lambda bi, i: (bi, 0, i, 0)),
        pl.BlockSpec((1, A_HEADS, tm // tk, A_V_DIM, tk), lambda bi, i: (bi, 0, i, 0, 0)),
        pl.BlockSpec((1, tm, B_WIDTH), lambda bi, i: (bi, i, 0)),
        pl.BlockSpec((1, tm, C_WIDTH), lambda bi, i: (bi, i, 0)),
        pl.BlockSpec((1, tm, A_WIDTH), lambda bi, i: (bi, i, 0)),
        pl.BlockSpec((1, tm, B_WIDTH), lambda bi, i: (bi, i, 0)),
    ]
    if kv_only:
        out_shape, out_specs = out_shape[1:3], out_specs[1:3]
    return pl.pallas_call(
        functools.partial(_inproj_kernel, sub=sub, tk=tk, d=d, kv_only=kv_only),
        grid=(b, nt),
        in_specs=in_specs,
        out_specs=out_specs,
        out_shape=out_shape,
        compiler_params=_params(2),
        name="inproj",
    )(x, mod, gn, w_bf, gq, gk, cos, sa, sb, bmat, lng, lnb, ws_bf, bsm)


def _attn_kernel(*refs, tq, nq, n_src, tiles, lam_init, bounded):
    lam_ref, q2_ref, ga_ref, gsub_ref = refs[:4]
    src_refs = refs[4:4 + 2 * n_src]
    out_ref = refs[4 + 2 * n_src]
    p_sc, m_sc, l_sc, acc_sc = refs[5 + 2 * n_src:]

    def queries(u):
        rows = slice(u * tq, (u + 1) * tq)
        return jnp.concatenate([q2_ref[0, 0, 0, rows, :], q2_ref[0, 0, 1, rows, :]], axis=0)

    l_sc[...] = jnp.zeros(l_sc.shape, _F32)
    acc_sc[...] = jnp.zeros(acc_sc.shape, _F32)
    lv = lam_ref[...]
    lam = (jnp.exp(jnp.sum(lv[0:1] * lv[1:2], axis=-1, keepdims=True))
           - jnp.exp(jnp.sum(lv[2:3] * lv[3:4], axis=-1, keepdims=True)) + lam_init)

    def finalize(u):
        inv = 1.0 / l_sc[u]
        acc = acc_sc[u]
        o = acc[:, :tq] * inv[:, :tq] - lam * (acc[:, tq:] * inv[:, tq:])
        ms = jnp.mean(o * o, axis=0, keepdims=True)
        on = o * lax.rsqrt(ms + EPS)
        rows = slice(u * tq, (u + 1) * tq)
        out = on.T * (gsub_ref[...] * (1.0 - lam_init)) * ga_ref[0, rows, :].astype(_F32)
        out_ref[0, rows, :] = out.astype(_BF16)

    def tile_len(si):
        return src_refs[2 * si + 1].shape[-1]

    def k_tile(si, t):
        tk = tile_len(si)
        if isinstance(t, int):
            return src_refs[2 * si][0, 0, t * tk:(t + 1) * tk, :]
        return src_refs[2 * si][0, 0, pl.ds(pl.multiple_of(t * tk, tk), tk), :]

    def vt_tile(si, t):
        return src_refs[2 * si + 1][0, 0, t]

    if bounded:
        def exponentials(slot, u, k_t):
            p = jnp.exp2(lax.dot_general(k_t, queries(u), _NT, preferred_element_type=_F32))
            l_sc[u] += jnp.sum(p, axis=0, keepdims=True)
            p_sc[slot, :k_t.shape[0], :] = p.astype(_BF16)

        def weighted_values(slot, u, vt_t):
            acc_sc[u] += jnp.dot(vt_t, p_sc[slot, :vt_t.shape[1], :], preferred_element_type=_F32)

        stages = [(u, si, t) for u in range(nq) for si in range(n_src) for t in range(tiles[si])]
        exponentials(0, stages[0][0], k_tile(*stages[0][1:]))
        for j, (u, si, t) in enumerate(stages):
            if j + 1 < len(stages):
                exponentials((j + 1) % 2, stages[j + 1][0], k_tile(*stages[j + 1][1:]))
            weighted_values(j % 2, u, vt_tile(si, t))
            if j + 1 == len(stages) or stages[j + 1][0] != u:
                finalize(u)
    else:
        for u in range(nq):
            m_sc[...] = jnp.full(m_sc.shape, -jnp.inf, _F32)

            def step(k_t, vt_t, u=u):
                s = lax.dot_general(k_t, queries(u), _NT, preferred_element_type=_F32)
                m_prev = m_sc[...]
                m_new = jnp.maximum(m_prev, jnp.max(s, axis=0, keepdims=True))
                alpha = jnp.exp2(m_prev - m_new)
                p = jnp.exp2(s - m_new)
                l_sc[u] = alpha * l_sc[u] + jnp.sum(p, axis=0, keepdims=True)
                acc_sc[u] = alpha * acc_sc[u] + jnp.dot(vt_t, p.astype(_BF16), preferred_element_type=_F32)
                m_sc[...] = m_new

            for si in range(n_src):
                if tiles[si] == 1:
                    step(k_tile(si, 0), vt_tile(si, 0))
                else:
                    def body(t, carry, si=si, step=step):
                        step(k_tile(si, t), vt_tile(si, t))
                        return carry
                    lax.fori_loop(0, tiles[si], body, 0)
            finalize(u)


def _attn_call(lamv, q2, ga, gsub, sources, *, tq, nq, lam_init, bounded):
    b, _, _, n, _ = q2.shape
    rows = nq * tq
    in_specs = [
        pl.BlockSpec(lamv.shape, lambda bi, hi, i: (0, 0)),
        pl.BlockSpec((1, 1, 2, rows, A_V_DIM), lambda bi, hi, i: (bi, hi, 0, i, 0)),
        pl.BlockSpec((1, rows, A_V_DIM), lambda bi, hi, i: (bi, i, hi)),
        pl.BlockSpec((1, A_V_DIM), lambda bi, hi, i: (0, 0)),
    ]
    args = [lamv, q2, ga, gsub]
    tiles = []
    for k_s, vt_s in sources:
        ls = k_s.shape[2]
        nt, _, tk = vt_s.shape[2:]
        tiles.append(nt)
        in_specs.append(pl.BlockSpec((1, 1, ls, A_V_DIM), lambda bi, hi, i: (bi, hi, 0, 0)))
        in_specs.append(pl.BlockSpec((1, 1, nt, A_V_DIM, tk), lambda bi, hi, i: (bi, hi, 0, 0, 0)))
        args += [k_s, vt_s]
    tk_max = max(vt_s.shape[-1] for _, vt_s in sources)
    return pl.pallas_call(
        functools.partial(_attn_kernel, tq=tq, nq=nq, n_src=len(sources), tiles=tuple(tiles),
                          lam_init=lam_init, bounded=bounded),
        grid=(b, A_HEADS, n // rows),
        in_specs=in_specs,
        out_specs=pl.BlockSpec((1, rows, A_V_DIM), lambda bi, hi, i: (bi, i, hi)),
        out_shape=jax.ShapeDtypeStruct((b, n, A_WIDTH), _BF16),
        scratch_shapes=[pltpu.VMEM((2, tk_max, 2 * tq), _BF16),
                        pltpu.VMEM((1, 2 * tq), _F32),
                        pltpu.VMEM((nq, 1, 2 * tq), _F32),
                        pltpu.VMEM((nq, A_V_DIM, 2 * tq), _F32)],
        compiler_params=_params(3),
        name="diff_attn" if bounded else "diff_attn_general",
    )(*args)


SAFE_EXP2_RANGE = 60.0


def _attention(lamv, q2, ga, gsub, sources, gq, gk, *, tq, nq, lam_init):
    bound = A_QK_DIM * Q_SCALE * jnp.max(jnp.abs(gq)) * jnp.max(jnp.abs(gk))
    call = lambda bounded: functools.partial(_attn_call, tq=tq, nq=nq, lam_init=lam_init, bounded=bounded)
    return lax.cond(bound <= SAFE_EXP2_RANGE, call(True), call(False), lamv, q2, ga, gsub, sources)


def _dft1_kernel(f1_ref, x_ref, o_ref, *, rows_per_step):
    n1 = x_ref.shape[1]
    x = x_ref[0].reshape(n1, rows_per_step * B_WIDTH).astype(_BF16)
    z = jnp.dot(f1_ref[...], x, preferred_element_type=_F32)
    o_ref[0] = z.reshape(2 * n1, rows_per_step, B_WIDTH).astype(_BF16)


def _dft1_call(f1, x4d, *, rows_per_step):
    b, n1, n2, _ = x4d.shape
    return pl.pallas_call(
        functools.partial(_dft1_kernel, rows_per_step=rows_per_step),
        grid=(b, n2 // rows_per_step),
        in_specs=[pl.BlockSpec((2 * n1, n1), lambda bi, j: (0, 0)),
                  pl.BlockSpec((1, n1, rows_per_step, B_WIDTH), lambda bi, j: (bi, 0, j, 0))],
        out_specs=pl.BlockSpec((1, 2 * n1, rows_per_step, B_WIDTH), lambda bi, j: (bi, 0, j, 0)),
        out_shape=jax.ShapeDtypeStruct((b, 2 * n1, n2, B_WIDTH), _BF16),
        compiler_params=_params(2),
        name="dft_stage1",
    )(f1, x4d)


def _dft2_kernel(*refs, g, n2, two_stage, norm):
    if two_stage:
        ar_ref, ai_ref, twc_ref, tws_ref, w2c_ref, mix_ref, wf_ref, bf_ref, out_ref = refs
        ar = ar_ref[0, 0].reshape(g * n2, B_WIDTH).astype(_F32)
        ai = ai_ref[0, 0].reshape(g * n2, B_WIDTH).astype(_F32)
        tc = twc_ref[...].reshape(g * n2, V7X_LANES)
        ts = tws_ref[...].reshape(g * n2, V7X_LANES)
        c2 = jnp.concatenate([tc, tc], axis=1)
        s2 = jnp.concatenate([ts, ts], axis=1)
        a = jnp.concatenate([ar * c2 - ai * s2, ar * s2 + ai * c2], axis=1).astype(_BF16)
        uv = jnp.dot(a, mix_ref[...], preferred_element_type=_F32)
    else:
        ar_ref, w2c_ref, mix_ref, wf_ref, bf_ref, out_ref = refs
        uv = jnp.dot(ar_ref[0, 0, 0].astype(_BF16), mix_ref[:B_WIDTH, :], preferred_element_type=_F32)
    u = uv[:, :B_WIDTH].astype(_BF16)
    v = uv[:, B_WIDTH:].astype(_BF16)
    w2c = w2c_ref[...]
    frs = []
    for j in range(g):
        rows = slice(j * n2, (j + 1) * n2)
        frs.append(jnp.dot(w2c, jnp.concatenate([u[rows], v[rows]], axis=0), preferred_element_type=_F32))
    fr = jnp.concatenate(frs, axis=0) * norm
    y = jnp.dot(fr.astype(_BF16), wf_ref[...], preferred_element_type=_F32) + bf_ref[...]
    if two_stage:
        out_ref[0] = jnp.swapaxes(y.reshape(g, n2, B_WIDTH), 0, 1).astype(_BF16)
    else:
        out_ref[0] = y.astype(_BF16)


def _dft_tables(n1, n2):
    n = n1 * n2
    k1 = np.arange(n1)
    a1 = 2.0 * np.pi * ((k1[:, None] * k1[None, :]) % n1) / n1
    f1 = np.concatenate([np.cos(a1), np.sin(a1)], axis=0)
    k2 = np.arange(n2)
    a2 = 2.0 * np.pi * ((k2[:, None] * k2[None, :]) % n2) / n2
    w2c = np.concatenate([np.cos(a2), -np.sin(a2)], axis=1)
    gch = np.arange(B_GROUP_DIM)
    ag = 2.0 * np.pi * ((gch[:, None] * gch[None, :]) % B_GROUP_DIM) / B_GROUP_DIM
    eye = np.eye(B_GROUPS)
    cg = np.kron(eye, np.cos(ag))
    sg = np.kron(eye, np.sin(ag))
    mix = np.block([[cg, sg], [-sg, cg]])
    to_bf16 = lambda t: jnp.asarray(t, _F32).astype(_BF16)
    return to_bf16(f1), to_bf16(w2c), to_bf16(mix), 1.0 / math.sqrt(n * B_GROUP_DIM)


def _twiddles(n1, n2):
    k1 = jnp.arange(n1, dtype=jnp.int32)[:, None]
    m2 = jnp.arange(n2, dtype=jnp.int32)[None, :]
    ang = (k1 * m2).astype(_F32) * (2.0 * math.pi / (n1 * n2))
    shape = (n1, n2, V7X_LANES)
    return (jnp.broadcast_to(jnp.cos(ang)[:, :, None], shape), jnp.broadcast_to(jnp.sin(ang)[:, :, None], shape))


def _fourier(f, wf_bd, bf_row):
    b, n, _ = f.shape
    two_stage = n > 2 * V7X_LANES
    n2 = V7X_LANES if two_stage else n
    n1 = n // n2
    g = min(n1, V7X_BF16_SUBLANES)
    f1, w2c, mix, norm = _dft_tables(n1, n2)
    full = lambda *shape: pl.BlockSpec(shape, lambda bi, j: (0,) * len(shape))
    consts = [w2c, mix, wf_bd, bf_row]
    const_specs = [full(*w2c.shape), full(*mix.shape), full(B_WIDTH, B_WIDTH), full(1, B_WIDTH)]
    if two_stage:
        a = _dft1_call(f1, f.reshape(b, n1, n2, B_WIDTH), rows_per_step=2 * V7X_BF16_SUBLANES)
        a5 = a.reshape(b, 2, n1, n2, B_WIDTH)
        out_shape = jax.ShapeDtypeStruct((b, n2, n1, B_WIDTH), _BF16)
        out_spec = pl.BlockSpec((1, n2, g, B_WIDTH), lambda bi, j: (bi, 0, j, 0))
        twc, tws = _twiddles(n1, n2)
        args = [a5, a5, twc, tws] + consts
        in_specs = [pl.BlockSpec((1, 1, g, n2, B_WIDTH), lambda bi, j: (bi, 0, j, 0, 0)),
                    pl.BlockSpec((1, 1, g, n2, B_WIDTH), lambda bi, j: (bi, 1, j, 0, 0)),
                    pl.BlockSpec((g, n2, V7X_LANES), lambda bi, j: (j, 0, 0)),
                    pl.BlockSpec((g, n2, V7X_LANES), lambda bi, j: (j, 0, 0))] + const_specs
    else:
        args = [f.reshape(b, 1, 1, n2, B_WIDTH)] + consts
        in_specs = [pl.BlockSpec((1, 1, 1, n2, B_WIDTH), lambda bi, j: (bi, 0, 0, 0, 0))] + const_specs
        out_shape = jax.ShapeDtypeStruct((b, n2, B_WIDTH), _BF16)
        out_spec = pl.BlockSpec((1, n2, B_WIDTH), lambda bi, j: (bi, 0, 0))
    out = pl.pallas_call(
        functools.partial(_dft2_kernel, g=g, n2=n2, two_stage=two_stage, norm=norm),
        grid=(b, n1 // g),
        in_specs=in_specs,
        out_specs=out_spec,
        out_shape=out_shape,
        compiler_params=_params(2),
        name="dft_stage2",
    )(*args)
    return out.reshape(b, n, B_WIDTH)


def _merge_kernel(oa_ref, ob_ref, gb_ref, oc_ref, x_ref, mod_ref, gn_ref, wm_ref, wa_ref, wb_ref, wc_ref, wo_ref,
                  o_ref, *, d, sub):
    mod = mod_ref[0]
    for r0 in range(0, x_ref.shape[1], sub):
        rows = slice(r0, r0 + sub)
        x = x_ref[0, rows, :]
        hb = _modulated_norm(x, mod, gn_ref[...], d)
        branches = (jnp.dot(oa_ref[0, rows, :], wa_ref[...], preferred_element_type=_F32),
                    jnp.dot(ob_ref[0, rows, :] * gb_ref[0, rows, :], wb_ref[...], preferred_element_type=_F32),
                    jnp.dot(oc_ref[0, rows, :], wc_ref[...], preferred_element_type=_F32))
        m = _sigmoid(jnp.dot(hb, wm_ref[:, OFF_MERGE:], preferred_element_type=_F32))
        y = None
        for j, yj in enumerate(branches):
            mj = m[:, j * d:(j + 1) * d]
            y = mj * yj if y is None else y + mj * yj
        out = jnp.dot(y.astype(_BF16), wo_ref[...], preferred_element_type=_F32)
        o_ref[0, rows, :] = x + mod[:, 2 * d:] * out


def _merge_call(oa, ob, gb, oc, x, mod, gn, w_in_bf, wa, wb, wc, wo, *, tm, sub, layer):
    b, n, d = x.shape
    row = lambda w: pl.BlockSpec((1, tm, w), lambda bi, i: (bi, i, 0))
    of_layer = lambda *shape: pl.BlockSpec((None,) + shape, lambda bi, i: (layer,) + (0,) * len(shape))
    return pl.pallas_call(
        functools.partial(_merge_kernel, d=d, sub=sub),
        grid=(b, n // tm),
        in_specs=[row(A_WIDTH), row(B_WIDTH), row(B_WIDTH), row(C_WIDTH), row(d),
                  pl.BlockSpec((1, 1, 3 * d), lambda bi, i: (bi, 0, 0)),
                  pl.BlockSpec((1, d), lambda bi, i: (0, 0)),
                  of_layer(d, w_in_bf.shape[-1]), of_layer(A_WIDTH, d), of_layer(B_WIDTH, d),
                  of_layer(C_WIDTH, d), of_layer(d, d)],
        out_specs=row(d),
        out_shape=jax.ShapeDtypeStruct((b, n, d), _F32),
        compiler_params=_params(2),
        name="merge",
    )(oa, ob, gb, oc, x, mod, gn, w_in_bf, wa, wb, wc, wo)


def _rope_tables(n):
    n_rows = n // GRID_W
    lane = np.arange(V7X_LANES) % A_QK_DIM
    row_lane = lane < AXIS_DIM
    first_half = (lane % AXIS_DIM) < AXIS_DIM // 2
    freqs = ROPE_BASE ** (-jnp.arange(0, AXIS_DIM, 2, dtype=_F32) / AXIS_DIM)
    f_lane = freqs[lane % (AXIS_DIM // 2)]
    ang_r = jnp.arange(n_rows, dtype=_F32)[:, None] * f_lane
    ang_c = jnp.arange(GRID_W, dtype=_F32)[:, None] * f_lane

    def on_grid(fn, lanes):
        keep_r = row_lane & lanes
        keep_c = ~row_lane & lanes
        r = jnp.where(keep_r, fn(ang_r), 0.0)
        c = jnp.where(keep_c, fn(ang_c), 0.0)
        return (r[:, None, :] + c[None, :, :]).reshape(n, V7X_LANES)

    every = np.ones(V7X_LANES, bool)
    cos = on_grid(jnp.cos, every)
    sa = on_grid(lambda a: -jnp.sin(a), first_half)
    sb = on_grid(jnp.sin, ~first_half)
    return cos, sa, sb


def _no_rope_tables(n):
    return (jnp.ones((n, V7X_LANES), _F32), jnp.zeros((n, V7X_LANES), _F32), jnp.zeros((n, V7X_LANES), _F32))


def kernel(x, c, ctx, c_ctx, w_ada, b_ada, g_norm, w_in, g_q, g_k, lam_q1, lam_k1, lam_q2, lam_k2, g_sub,
           w_f, b_f, ln_g, ln_b, w_s, b_s, w_br_a, w_br_b, w_br_c, w_out):
    b, n, d = x.shape
    n_ctx = ctx.shape[1]
    depth = w_in.shape[0]
    tm_lat = min(n, 512)
    tm_ctx = min(n_ctx, 512)

    n_rows = V7X_SUBLANES * pl.cdiv(b + 1, V7X_SUBLANES)
    rows = jnp.concatenate([c, c_ctx[None, :], jnp.zeros((n_rows - b - 1, d), _F32)], axis=0)
    mod_all = _mod_call(rows, w_ada, b_ada)

    rope_lat = _rope_tables(n)
    rope_ctx = _no_rope_tables(n_ctx)
    group_mean = np.kron(np.eye(2 * A_V_DIM // A_QK_DIM), np.full((A_QK_DIM, A_QK_DIM), 1.0 / A_QK_DIM))
    bmat = jnp.asarray(group_mean, _BF16)

    w_in_bf = w_in.astype(_BF16)
    w_s_bf = w_s.astype(_BF16)
    merge_w = (w_in_bf, w_br_a.astype(_BF16), w_br_b.astype(_BF16), w_br_c.astype(_BF16), w_out.astype(_BF16))

    for l in range(depth):
        last = l == depth - 1
        lam_init = 0.8 - 0.6 * math.exp(-0.3 * l)
        lamv = jnp.stack([lam_q1[l], lam_k1[l], lam_q2[l], lam_k2[l]], axis=0)
        shared = dict(
            gn=g_norm[l][None, :], w_bf=w_in_bf,
            gq=jnp.tile(g_q[l], 2)[None, :], gk=jnp.tile(g_k[l], 2)[None, :], bmat=bmat,
            lng=ln_g[l][None, :], lnb=ln_b[l][None, :], ws_bf=w_s_bf,
            bsm=jnp.repeat(b_s[l].T, C_GROUP_DIM, axis=1), layer=l)
        gsub = g_sub[l][None, :]
        wf_bd = jax.scipy.linalg.block_diag(*[w_f[l, gi] for gi in range(B_GROUPS)]).astype(_BF16)
        bf_row = b_f[l].reshape(1, B_WIDTH)

        mod_lat = mod_all[l, :b][:, None, :]
        mod_ctx = jnp.broadcast_to(mod_all[l, b][None, None, :], (b, 1, 3 * d))

        ctx_proj = _inproj_call(ctx, mod_ctx, cos=rope_ctx[0], sa=rope_ctx[1], sb=rope_ctx[2], tm=tm_ctx,
                                sub=tm_ctx, tk=tm_ctx, kv_only=last, **shared)
        if last:
            ck, cvt = ctx_proj
        else:
            cq2, ck, cvt, cf, coc, cga, cgb = ctx_proj
        q2, k, vt, f, oc, ga, gb = _inproj_call(
            x, mod_lat, cos=rope_lat[0], sa=rope_lat[1], sb=rope_lat[2], tm=min(n, 2 * tm_lat), sub=tm_lat,
            tk=min(n, 2 * tm_lat), **shared)

        tq_lat = min(n, 2 * tm_lat)
        oa = _attention(lamv, q2, ga, gsub, [(ck, cvt), (k, vt)], g_q[l], g_k[l], tq=tq_lat,
                        nq=min(2, n // tq_lat), lam_init=lam_init)
        ob = _fourier(f, wf_bd, bf_row)
        x_new = _merge_call(oa, ob, gb, oc, x, mod_lat, shared["gn"], *merge_w, tm=min(n, 2 * tm_lat), sub=tm_lat,
                            layer=l)

        if not last:
            coa = _attention(lamv, cq2, cga, gsub, [(ck, cvt)], g_q[l], g_k[l], tq=tm_ctx, nq=1,
                             lam_init=lam_init)
            cob = _fourier(cf, wf_bd, bf_row)
            ctx = _merge_call(coa, cob, cgb, coc, ctx, mod_ctx, shared["gn"], *merge_w, tm=tm_ctx, sub=tm_ctx,
                              layer=l)
        x = x_new
    return x
```
